```python
import jax
import jax.numpy as jnp
from jax import lax
import numpy as np

D_MODEL = 2048
BATCH = 1
SEQ = 16384
DEPTH = 2

RW_HEADS = 8
RW_HEAD_DIM = 64
RW_WIDTH = RW_HEADS * RW_HEAD_DIM
RW_LORA_DECAY = 64
RW_LORA_ICLR = 64
RW_LORA_GATE = 128
RW_LORA_VRES = 32
RW_GN_EPS = 64e-5
ML_HEADS = 4
ML_QK_DIM = 128
ML_V_DIM = 256
ML_QK_WIDTH = ML_HEADS * ML_QK_DIM
ML_WIDTH = ML_HEADS * ML_V_DIM
ML_CONV = 4
ML_SOFTCAP = 15.0
HG_HEADS = 4
HG_EXPAND = 128
HG_HEAD_DIM = 128
HG_K_WIDTH = HG_HEADS * HG_EXPAND
HG_WIDTH = HG_HEADS * HG_HEAD_DIM
CHUNK = 64
MOE_GROUPS = 4
MOE_EXPERTS_PER_GROUP = 8
MOE_EXPERTS = MOE_GROUPS * MOE_EXPERTS_PER_GROUP
MOE_TOP_K = 2
MOE_D_FF = 1024
MOE_BLOCK = 128
NORM_EPS = 1e-6

RW_SPLITS = (RW_WIDTH, RW_WIDTH, RW_WIDTH, RW_LORA_DECAY, RW_LORA_ICLR, RW_LORA_GATE)
RW_COLS = sum(RW_SPLITS)
REST_SPLITS = (ML_QK_WIDTH, ML_QK_WIDTH, ML_WIDTH, ML_WIDTH, ML_HEADS, ML_HEADS,
               HG_K_WIDTH, HG_K_WIDTH, HG_WIDTH, HG_WIDTH,
               D_MODEL, D_MODEL, D_MODEL)
N_IN = RW_COLS + sum(REST_SPLITS)

kernel_name = 'hybrid_rwkv7_mlstm_hgrn2_hiermoe_adaln'


def rmsnorm(x, g):
    x32 = x.astype(jnp.float32)
    y = x32 * lax.rsqrt(jnp.mean(x32 * x32, axis=-1, keepdims=True) + NORM_EPS)
    return (y * g.astype(jnp.float32)).astype(x.dtype)


def head_rmsnorm(x, g):
    B, T, H, d = x.shape
    return rmsnorm(x, g.reshape(H, d)).reshape(B, T, H * d)


def split_cols(u, sizes):
    return jnp.split(u, np.cumsum(sizes)[:-1].tolist(), axis=-1)


def shift_right(x, n):
    return jnp.pad(x, ((0, 0), (n, 0), (0, 0)))[:, :x.shape[1]]


def causal_dwconv(x, w, b):
    K = w.shape[0]
    return b + sum(shift_right(x, K - 1 - j) * w[j] for j in range(K))


def softcap(x):
    return ML_SOFTCAP * jnp.tanh(x / ML_SOFTCAP)


def to_chunks(x):
    B, T, H = x.shape[:3]
    x = x.reshape(B, T // CHUNK, CHUNK, H, *x.shape[3:])
    return jnp.moveaxis(jnp.moveaxis(x, 1, 0), 2, 3)


def from_chunks(y):
    NC, B, H, L, d = y.shape
    return jnp.moveaxis(jnp.moveaxis(y, 3, 2), 0, 1).reshape(B, NC * L, H, d)


def rwkv7_recurrence(r, decay, k, v, a, b):
    B, T, H, N = r.shape

    def step(S, inp):
        r_t, w_t, k_t, v_t, a_t, b_t = inp
        sa = jnp.einsum('bhvk,bhk->bhv', S, a_t)
        S = S * w_t[:, :, None, :] + sa[..., None] * b_t[:, :, None, :] + v_t[..., None] * k_t[:, :, None, :]
        return S, jnp.einsum('bhvk,bhk->bhv', S, r_t)

    xs = tuple(jnp.moveaxis(t.astype(jnp.float32), 1, 0) for t in (r, decay, k, v, a, b))
    _, y = lax.scan(step, jnp.zeros((B, H, N, N), jnp.float32), xs)
    return jnp.moveaxis(y, 0, 1)


def rwkv7_branch(r, k, v, xw, xa, xg, v_first, vres, w0, w2, a0, a2, g2, k_k, k_a, r_k, ln_w, ln_b):
    B, T, _ = r.shape
    heads = lambda t: t.reshape(B, T, RW_HEADS, RW_HEAD_DIM)
    log_w = -jax.nn.softplus(-(w0 + jnp.tanh(xw) @ w2).astype(jnp.float32)) - 0.5
    decay = jnp.exp(-jnp.exp(log_w))
    iclr = jax.nn.sigmoid(a0 + xa @ a2)
    gate = jax.nn.sigmoid(xg) @ g2
    if vres is None:
        v_first = v
    else:
        v0, v1, v2 = vres
        v = v + (v_first - v) * jax.nn.sigmoid(v0 + (v @ v1) @ v2)
    kk = heads(k * k_k).astype(jnp.float32)
    kk = kk / jnp.maximum(jnp.sqrt(jnp.sum(kk * kk, axis=-1, keepdims=True)), 1e-12)
    k = k * (1.0 + (iclr - 1.0) * k_a)
    rh, kh, vh = heads(r), heads(k), heads(v)
    y = rwkv7_recurrence(rh, heads(decay), kh, vh, -kk, kk * heads(iclr))
    mu = jnp.mean(y, axis=-1, keepdims=True)
    var = jnp.mean(jnp.square(y - mu), axis=-1, keepdims=True)
    y = ((y - mu) * lax.rsqrt(var + RW_GN_EPS)).reshape(B, T, RW_WIDTH) * ln_w + ln_b
    bonus = jnp.sum(rh * kh * r_k.reshape(RW_HEADS, RW_HEAD_DIM), axis=-1, keepdims=True) * vh
    y = y.astype(r.dtype) + bonus.reshape(B, T, RW_WIDTH)
    return y * gate, v_first


def mlstm_chunkwise(q, k, v, log_i, log_f):
    B, T, H, Dk = q.shape
    Dv = v.shape[-1]
    causal = jnp.tril(jnp.ones((CHUNK, CHUNK), bool))

    def step(carry, inp):
        C, n, m = carry
        qc, kc, vc, li, lf = inp
        b = jnp.cumsum(lf, axis=-1)
        d = jnp.where(causal, b[..., :, None] - b[..., None, :] + li[..., None, :], -jnp.inf)
        m_inter = b + m[..., None]
        m_t = jnp.maximum(m_inter, jnp.max(d, axis=-1))
        s = jnp.einsum('bhtd,bhsd->bhts', qc, kc) * jnp.exp(d - m_t[..., None])
        inter = jnp.exp(m_inter - m_t)
        num = jnp.einsum('bhts,bhsv->bhtv', s, vc) + inter[..., None] * jnp.einsum('bhtd,bhdv->bhtv', qc, C)
        den = jnp.sum(s, axis=-1) + inter * jnp.einsum('bhtd,bhd->bht', qc, n)
        h = num / jnp.maximum(jnp.abs(den), jnp.exp(-m_t))[..., None]
        g = b[..., -1:] - b + li
        m_new = jnp.maximum(b[..., -1] + m, jnp.max(g, axis=-1))
        wgt = jnp.exp(g - m_new[..., None])
        keep = jnp.exp(b[..., -1] + m - m_new)
        C = keep[..., None, None] * C + jnp.einsum('bhs,bhsd,bhsv->bhdv', wgt, kc, vc)
        n = keep[..., None] * n + jnp.einsum('bhs,bhsd->bhd', wgt, kc)
        return (C, n, m_new), h

    xs = tuple(to_chunks(t.astype(jnp.float32)) for t in (q, k, v, log_i, log_f))
    init = (jnp.zeros((B, H, Dk, Dv), jnp.float32), jnp.zeros((B, H, Dk), jnp.float32), jnp.zeros((B, H), jnp.float32))
    _, h = lax.scan(step, init, xs)
    return from_chunks(h)


def mlstm_branch(q, k, v, o, i_pre, f_pre, norm_g):
    B, T, _ = q.shape
    qh = q.reshape(B, T, ML_HEADS, ML_QK_DIM)
    kh = k.reshape(B, T, ML_HEADS, ML_QK_DIM) * ML_QK_DIM ** -0.5
    vh = v.reshape(B, T, ML_HEADS, ML_V_DIM)
    log_i = softcap(i_pre.astype(jnp.float32))
    log_f = jax.nn.log_sigmoid(softcap(f_pre.astype(jnp.float32)))
    h = mlstm_chunkwise(qh, kh, vh, log_i, log_f)
    return head_rmsnorm(h.astype(q.dtype), norm_g) * jax.nn.sigmoid(o)


def hgrn2_chunkwise(q, k, i, log_f):
    B, T, H, Dk = q.shape
    Dv = i.shape[-1]
    causal = jnp.tril(jnp.ones((CHUNK, CHUNK), bool))

    def step(S, inp):
        qc, kc, ic, gc = inp
        cum = jnp.cumsum(gc, axis=2)
        diff = jnp.where(causal[..., None], cum[:, :, :, None, :] - cum[:, :, None, :, :], -jnp.inf)
        att = jnp.einsum('bhtd,bhtsd,bhsd->bhts', qc, jnp.exp(diff), kc)
        o = jnp.einsum('bhts,bhsv->bhtv', att, ic) + jnp.einsum('bhtd,bhdv->bhtv', qc * jnp.exp(cum), S)
        last = cum[:, :, -1:, :]
        S = jnp.exp(last[:, :, 0])[..., None] * S + jnp.einsum('bhsd,bhsv->bhdv', kc * jnp.exp(last - cum), ic)
        return S, o

    xs = tuple(to_chunks(t.astype(jnp.float32)) for t in (q, k, i, log_f))
    _, o = lax.scan(step, jnp.zeros((B, H, Dk, Dv), jnp.float32), xs)
    return from_chunks(o)


def hgrn2_branch(q, f_pre, i, g_pre, lb, norm_g):
    B, T, _ = q.shape
    f_pre = f_pre.astype(jnp.float32)
    lb = lb.astype(jnp.float32)
    log_f = jnp.logaddexp(jnp.log(lb), jnp.log1p(-lb) + jax.nn.log_sigmoid(f_pre))
    k = (1.0 - lb) * jax.nn.sigmoid(-f_pre)
    hk = lambda t: t.reshape(B, T, HG_HEADS, HG_EXPAND)
    o = hgrn2_chunkwise(hk(jax.nn.silu(q)), hk(k), i.reshape(B, T, HG_HEADS, HG_HEAD_DIM), hk(log_f))
    return head_rmsnorm(o.astype(q.dtype), norm_g) * jax.nn.silu(g_pre)


def hier_moe(h, gw, gb, ew, eb, w_gate, w_up, w_down):
    B, T, D = h.shape
    N = B * T
    A = N * MOE_TOP_K
    hf = h.reshape(N, D)
    g_logits = (hf @ gw + gb).astype(jnp.float32)
    g_prob = jax.nn.softmax(g_logits, axis=-1)
    _, g_idx = lax.top_k(g_logits, 1)
    p_group = jnp.take_along_axis(g_prob, g_idx, axis=-1)
    e_logits = (hf @ ew + eb).astype(jnp.float32).reshape(N, MOE_GROUPS, MOE_EXPERTS_PER_GROUP)
    e_logits = jnp.take_along_axis(e_logits, g_idx[:, :, None], axis=1)[:, 0]
    e_top, e_local = lax.top_k(e_logits, MOE_TOP_K)
    weights = p_group * jax.nn.softmax(e_top, axis=-1)
    expert = g_idx * MOE_EXPERTS_PER_GROUP + e_local
    flat = expert.reshape(A)
    order = jnp.argsort(flat)
    sorted_e = flat[order]
    counts = jnp.bincount(flat, length=MOE_EXPERTS)
    padded = (counts + MOE_BLOCK - 1) // MOE_BLOCK * MOE_BLOCK
    pad_end = jnp.cumsum(padded)
    start = jnp.cumsum(counts) - counts
    dest_sorted = pad_end[sorted_e] - padded[sorted_e] + jnp.arange(A, dtype=jnp.int32) - start[sorted_e]
    dest = jnp.zeros((A,), jnp.int32).at[order].set(dest_sorted.astype(jnp.int32))
    n_blocks = -(-A // MOE_BLOCK) + MOE_EXPERTS
    rows = n_blocks * MOE_BLOCK
    row_token = jnp.zeros((rows,), jnp.int32).at[dest].set(jnp.arange(A, dtype=jnp.int32) // MOE_TOP_K)
    block_start = jnp.arange(n_blocks, dtype=jnp.int32) * MOE_BLOCK
    block_expert = jnp.minimum(jnp.searchsorted(pad_end, block_start, side='right'), MOE_EXPERTS - 1)
    xb = hf[row_token].reshape(n_blocks, MOE_BLOCK, D)

    def expert_block(args):
        xe, e = args
        return (jax.nn.silu(xe @ w_gate[e]) * (xe @ w_up[e])) @ w_down[e]

    yb = lax.map(expert_block, (xb, block_expert)).reshape(rows, D)
    y = yb[dest].reshape(N, MOE_TOP_K, D)
    out = jnp.einsum('nk,nkd->nd', weights.astype(y.dtype), y)
    return out.reshape(B, T, D)


def setup_inputs(seed: int = 0) -> dict:
    key = jax.random.key(seed)
    ks = iter(jax.random.split(key, 64))

    def nrm(shape, scale):
        return scale * jax.random.normal(next(ks), shape, jnp.float32)

    def uni(shape, lo, hi):
        return jax.random.uniform(next(ks), shape, jnp.float32, lo, hi)

    def gain(shape):
        return 1.0 + nrm(shape, 0.02)

    L, D = DEPTH, D_MODEL
    return {
        'x': nrm((BATCH, SEQ, D), 1.0),
        'c': nrm((BATCH, D), 1.0),
        'ada_w': nrm((L, D, 6 * D), 0.5 * D ** -0.5),
        'ada_b': nrm((L, 6 * D), 0.01),
        'norm_mix': gain((L, D)),
        'norm_ffn': gain((L, D)),
        'w_in': nrm((L, D, N_IN), D ** -0.5),
        'rw_mu': uni((L, RW_COLS), 0.0, 1.0),
        'rw_w0': uni((L, RW_WIDTH), -6.0, 0.0),
        'rw_w2': nrm((L, RW_LORA_DECAY, RW_WIDTH), 0.5 * RW_LORA_DECAY ** -0.5),
        'rw_a0': nrm((L, RW_WIDTH), 0.1),
        'rw_a2': nrm((L, RW_LORA_ICLR, RW_WIDTH), 0.5 * RW_LORA_ICLR ** -0.5),
        'rw_g2': nrm((L, RW_LORA_GATE, RW_WIDTH), RW_LORA_GATE ** -0.5),
        'rw_kk': 0.85 + nrm((L, RW_WIDTH), 0.02),
        'rw_ka': gain((L, RW_WIDTH)),
        'rw_rk': nrm((L, RW_WIDTH), 0.1),
        'rw_lnw': gain((L, RW_WIDTH)),
        'rw_lnb': nrm((L, RW_WIDTH), 0.02),
        'rw_v0': nrm((L - 1, RW_WIDTH), 0.1),
        'rw_v1': nrm((L - 1, RW_WIDTH, RW_LORA_VRES), RW_WIDTH ** -0.5),
        'rw_v2': nrm((L - 1, RW_LORA_VRES, RW_WIDTH), 0.5 * RW_LORA_VRES ** -0.5),
        'ml_conv_w': nrm((L, ML_CONV, 2 * ML_QK_WIDTH), ML_CONV ** -0.5),
        'ml_conv_b': nrm((L, 2 * ML_QK_WIDTH), 0.02),
        'ml_ib': nrm((L, ML_HEADS), 0.1),
        'ml_fb': uni((L, ML_HEADS), 3.0, 6.0),
        'ml_norm': gain((L, ML_WIDTH)),
        'hg_lb': nrm((L, HG_K_WIDTH), 1.0),
        'hg_norm': gain((L, HG_WIDTH)),
        'p_a': nrm((L, RW_WIDTH, D), RW_WIDTH ** -0.5),
        'p_b': nrm((L, ML_WIDTH, D), ML_WIDTH ** -0.5),
        'p_c': nrm((L, HG_WIDTH, D), HG_WIDTH ** -0.5),
        'w_out': nrm((L, D, D), D ** -0.5),
        'moe_gw': nrm((L, D, MOE_GROUPS), D ** -0.5),
        'moe_gb': nrm((L, MOE_GROUPS), 0.01),
        'moe_ew': nrm((L, D, MOE_EXPERTS), D ** -0.5),
        'moe_eb': nrm((L, MOE_EXPERTS), 0.01),
        'ex_gate': nrm((L, MOE_EXPERTS, D, MOE_D_FF), D ** -0.5),
        'ex_up': nrm((L, MOE_EXPERTS, D, MOE_D_FF), D ** -0.5),
        'ex_down': nrm((L, MOE_EXPERTS, MOE_D_FF, D), MOE_D_FF ** -0.5),
        'final_norm': gain((D,)),
    }


def reference(x, c, ada_w, ada_b, norm_mix, norm_ffn, w_in, rw_mu, rw_w0, rw_w2, rw_a0, rw_a2, rw_g2,
              rw_kk, rw_ka, rw_rk, rw_lnw, rw_lnb, rw_v0, rw_v1, rw_v2, ml_conv_w, ml_conv_b, ml_ib, ml_fb,
              ml_norm, hg_lb, hg_norm, p_a, p_b, p_c, w_out, moe_gw, moe_gb, moe_ew, moe_eb,
              ex_gate, ex_up, ex_down, final_norm):
    lbs = jnp.cumsum(jax.nn.softmax(hg_lb.astype(jnp.float32), axis=0), axis=0)
    lbs = lbs - lbs[:1]
    cond = jax.nn.silu(c)
    v_first = None
    for l in range(DEPTH):
        mod = (cond @ ada_w[l] + ada_b[l])[:, None, :]
        sh_m, sc_m, gt_m, sh_f, sc_f, gt_f = jnp.split(mod, 6, axis=-1)
        h = rmsnorm(x, norm_mix[l]) * (1.0 + sc_m) + sh_m
        u = h @ w_in[l]
        u_rw = u[..., :RW_COLS]
        u_rw = u_rw + rw_mu[l] * (shift_right(u_rw, 1) - u_rw)
        rw_r, rw_k, rw_v, rw_xw, rw_xa, rw_xg = split_cols(u_rw, RW_SPLITS)
        (ml_q, ml_k, ml_v, ml_o, ml_i, ml_f, hg_q, hg_f, hg_i, hg_g,
         g_a, g_b, g_c) = split_cols(u[..., RW_COLS:], REST_SPLITS)
        vres = None if l == 0 else (rw_v0[l - 1], rw_v1[l - 1], rw_v2[l - 1])
        y_a, v_first = rwkv7_branch(rw_r, rw_k, rw_v, rw_xw, rw_xa, rw_xg, v_first, vres,
                                    rw_w0[l], rw_w2[l], rw_a0[l], rw_a2[l], rw_g2[l],
                                    rw_kk[l], rw_ka[l], rw_rk[l], rw_lnw[l], rw_lnb[l])
        ml_qk = jax.nn.silu(causal_dwconv(jnp.concatenate([ml_q, ml_k], axis=-1), ml_conv_w[l], ml_conv_b[l]))
        ml_q, ml_k = jnp.split(ml_qk, 2, axis=-1)
        y_b = mlstm_branch(ml_q, ml_k, ml_v, ml_o, ml_i + ml_ib[l], ml_f + ml_fb[l], ml_norm[l])
        y_c = hgrn2_branch(hg_q, hg_f, hg_i, hg_g, lbs[l], hg_norm[l])
        y = (jax.nn.sigmoid(g_a) * (y_a @ p_a[l])
             + jax.nn.sigmoid(g_b) * (y_b @ p_b[l])
             + jax.nn.sigmoid(g_c) * (y_c @ p_c[l])) @ w_out[l]
        x = x + gt_m * y
        h = rmsnorm(x, norm_ffn[l]) * (1.0 + sc_f) + sh_f
        x = x + gt_f * hier_moe(h, moe_gw[l], moe_gb[l], moe_ew[l], moe_eb[l], ex_gate[l], ex_up[l], ex_down[l])
    return rmsnorm(x, final_norm)
```

```python
import functools

import numpy as np
import jax
import jax.numpy as jnp
from jax import lax
from jax.experimental import pallas as pl
from jax.experimental.pallas import tpu as pltpu

F32 = jnp.float32
BF16 = jnp.bfloat16

D_MODEL = 2048
DEPTH = 2
RW_HEADS, RW_HEAD_DIM, RW_WIDTH = 8, 64, 512
RW_LORA = 256
RW_COLS = 3 * RW_WIDTH + RW_LORA
RW_GN_EPS = 64e-5
ML_HEADS, ML_QK_DIM, ML_V_DIM = 4, 128, 256
ML_QK_WIDTH, ML_WIDTH = 512, 1024
ML_CONV = 4
ML_SOFTCAP = 15.0
HG_HEADS, HG_DIM, HG_WIDTH = 4, 128, 512
CHUNK = 64
MOE_GROUPS, MOE_EPG, MOE_EXPERTS, MOE_TOP_K, MOE_D_FF = 4, 8, 32, 2, 1024
NORM_EPS = 1e-6

LANES = 128
SUBLANES = 8
MOE_ROWS = 256
VMEM_LIMIT = 56 * 1024 * 1024


def _params(sem, limit=VMEM_LIMIT):
    return pltpu.CompilerParams(dimension_semantics=sem, vmem_limit_bytes=limit)


def _bf(x):
    return x.astype(BF16)


def _dot(a, b):
    return jnp.dot(_bf(a), _bf(b), preferred_element_type=F32)


def _dot_nt(a, b):
    return lax.dot_general(_bf(a), _bf(b), (((1,), (1,)), ((), ())), preferred_element_type=F32)


def _dot_tn(a, b):
    return lax.dot_general(_bf(a), _bf(b), (((0,), (0,)), ((), ())), preferred_element_type=F32)


def _split3(x):
    hi = x.astype(BF16)
    r1 = x - hi.astype(F32)
    mid = r1.astype(BF16)
    lo = (r1 - mid.astype(F32)).astype(BF16)
    return hi, mid, lo


def _sel_dot(sel, x):
    hi, mid, lo = _split3(x)
    return (jnp.dot(sel, hi, preferred_element_type=F32) + jnp.dot(sel, mid, preferred_element_type=F32)
            + jnp.dot(sel, lo, preferred_element_type=F32))


def _seg_sum(x, bd):
    hi, mid, lo = _split3(x)
    return (jnp.dot(hi, bd, preferred_element_type=F32) + jnp.dot(mid, bd, preferred_element_type=F32)
            + jnp.dot(lo, bd, preferred_element_type=F32))


def _sigmoid(x):
    return jax.nn.sigmoid(x)


def _silu(x):
    return x * jax.nn.sigmoid(x)


def _log_sigmoid(x):
    return jnp.minimum(x, 0.0) - jnp.log1p(jnp.exp(-jnp.abs(x)))


def _softplus(x):
    return jnp.maximum(x, 0.0) + jnp.log1p(jnp.exp(-jnp.abs(x)))


def _mod_kernel(c_ref, w_ref, b_ref, o_ref):
    cond = _silu(c_ref[...])
    o_ref[0] = _dot(cond, w_ref[0]) + b_ref[0]


def _adaln_mod(c, ada_w, ada_b):
    depth, d, n = ada_w.shape
    tn = 1024
    c8 = jnp.zeros((SUBLANES, d), F32).at[0].set(c[0])
    out = pl.pallas_call(
        _mod_kernel,
        grid=(depth, n // tn),
        in_specs=[pl.BlockSpec((SUBLANES, d), lambda l, j: (0, 0)),
                  pl.BlockSpec((1, d, tn), lambda l, j: (l, 0, j)),
                  pl.BlockSpec((1, 1, tn), lambda l, j: (l, 0, j))],
        out_specs=pl.BlockSpec((1, SUBLANES, tn), lambda l, j: (l, 0, j)),
        out_shape=jax.ShapeDtypeStruct((depth, SUBLANES, n), F32),
        compiler_params=_params(("arbitrary", "arbitrary")),
        name="adaln_mod",
    )(c8, ada_w, ada_b.reshape(depth, 1, n))
    return out[:, 0, :]


def _norm_kernel(x_ref, g_ref, sc_ref, sh_ref, o_ref, *, modulate):
    x = x_ref[...]
    y = x * lax.rsqrt(jnp.mean(x * x, axis=-1, keepdims=True) + NORM_EPS) * g_ref[...]
    if modulate:
        y = y * (1.0 + sc_ref[...]) + sh_ref[...]
    o_ref[...] = y.astype(o_ref.dtype)


def _norm(x, g, sc, sh, *, modulate, out_dtype):
    t, d = x.shape
    tm = min(512, t)
    row = pl.BlockSpec((1, d), lambda i: (0, 0))
    return pl.pallas_call(
        functools.partial(_norm_kernel, modulate=modulate),
        grid=(t // tm,),
        in_specs=[pl.BlockSpec((tm, d), lambda i: (i, 0)), row, row, row],
        out_specs=pl.BlockSpec((tm, d), lambda i: (i, 0)),
        out_shape=jax.ShapeDtypeStruct((t, d), out_dtype),
        compiler_params=_params(("arbitrary",)),
        name="rmsnorm_mod" if modulate else "rmsnorm",
    )(x, g.reshape(1, d), sc.reshape(1, d), sh.reshape(1, d))


def _mm_kernel(a_ref, w_ref, b_ref, o_ref):
    o_ref[...] = (_dot(a_ref[...], w_ref[...]) + b_ref[...]).astype(o_ref.dtype)


def _mm(a, w, bias, *, out_dtype, tn):
    m, k = a.shape
    n = w.shape[1]
    tm = min(512, m)
    return pl.pallas_call(
        _mm_kernel,
        grid=(n // tn, m // tm),
        in_specs=[pl.BlockSpec((tm, k), lambda j, i: (i, 0)),
                  pl.BlockSpec((k, tn), lambda j, i: (0, j)),
                  pl.BlockSpec((1, tn), lambda j, i: (0, j))],
        out_specs=pl.BlockSpec((tm, tn), lambda j, i: (i, j)),
        out_shape=jax.ShapeDtypeStruct((m, n), out_dtype),
        compiler_params=_params(("arbitrary", "arbitrary")),
        name="dense_matmul",
    )(a, w, bias.reshape(1, n))


def _rwkv_kernel(*refs, has_vres):
    if has_vres:
        (u_ref, vf_ref, mu_ref, wl_ref, w0_ref, a0_ref, kk_ref, ka_ref, rk_ref, lnw_ref, lnb_ref,
         v0_ref, v1_ref, v2_ref, bd_ref, tril_ref, y_ref, xbuf, st_ref) = refs
    else:
        (u_ref, mu_ref, wl_ref, w0_ref, a0_ref, kk_ref, ka_ref, rk_ref, lnw_ref, lnb_ref,
         bd_ref, tril_ref, y_ref, vf_out_ref, xbuf, st_ref) = refs
    L = CHUNK
    W = RW_WIDTH

    @pl.when(pl.program_id(0) == 0)
    def _():
        xbuf[0:SUBLANES, :] = jnp.zeros((SUBLANES, RW_COLS), F32)
        st_ref[...] = jnp.zeros(st_ref.shape, F32)

    u = u_ref[...]
    xbuf[SUBLANES:SUBLANES + L, :] = u
    prev = xbuf[SUBLANES - 1:SUBLANES - 1 + L, :]
    xbuf[0:SUBLANES, :] = u[L - SUBLANES:L, :]
    xs = u + mu_ref[...] * (prev - u)
    r = xs[:, 0:W]
    k = xs[:, W:2 * W]
    v = xs[:, 2 * W:3 * W]
    lr = xs[:, 3 * W:3 * W + RW_LORA]
    lane = lax.broadcasted_iota(jnp.int32, lr.shape, 1)
    act = jnp.where(lane < 64, jnp.tanh(lr), jnp.where(lane < 128, lr, _sigmoid(lr)))
    lo = _dot(act, wl_ref[...])
    z = w0_ref[...] + lo[:, 0:W]
    log_w = -_softplus(-z) - 0.5
    ld = -jnp.exp(log_w)
    iclr = _sigmoid(a0_ref[...] + lo[:, W:2 * W])
    gate = lo[:, 2 * W:3 * W]
    if has_vres:
        vv = _dot(_dot(v, v1_ref[...]), v2_ref[...])
        v = v + (vf_ref[...] - v) * _sigmoid(v0_ref[...] + vv)
    else:
        vf_out_ref[...] = v
    bd = bd_ref[...]
    kk = k * kk_ref[...]
    kk = kk / jnp.maximum(jnp.sqrt(_seg_sum(kk * kk, bd)), 1e-12)
    k = k * (1.0 + (iclr - 1.0) * ka_ref[...])
    a = -kk
    b = kk * iclr
    bonus = _seg_sum(r * k * rk_ref[...], bd) * v

    cum = _sel_dot(tril_ref[...], ld)
    cl = cum[L - 1:L, :]
    e_neg = jnp.exp(-cum)
    e_last = jnp.exp(cl - cum)
    a_t = a * jnp.exp(cum - ld)
    r_t = r * jnp.exp(cum)
    b_t = b * e_neg
    k_t = k * e_neg
    b_h = b * e_last
    k_h = k * e_last
    w_last = jnp.exp(cl)

    row = lax.broadcasted_iota(jnp.int32, (L, L), 0)
    col = lax.broadcasted_iota(jnp.int32, (L, L), 1)
    strict = row > col
    incl = row >= col
    eye = (row == col).astype(F32)
    lane128 = lax.broadcasted_iota(jnp.int32, (1, LANES), 1)
    head_masks = ((lane128 < RW_HEAD_DIM).astype(F32), (lane128 >= RW_HEAD_DIM).astype(F32))
    r2 = lax.broadcasted_iota(jnp.int32, (LANES, LANES), 0) // RW_HEAD_DIM
    c2 = lax.broadcasted_iota(jnp.int32, (LANES, LANES), 1) // RW_HEAD_DIM
    bdmask = (r2 == c2).astype(F32)

    ys = []
    for p in range(RW_HEADS // 2):
        sl = slice(p * LANES, (p + 1) * LANES)
        at, rt, bt, kt, bh, kh, vp = a_t[:, sl], r_t[:, sl], b_t[:, sl], k_t[:, sl], b_h[:, sl], k_h[:, sl], v[:, sl]
        pm = jnp.zeros((L, LANES), F32)
        qm = jnp.zeros((L, LANES), F32)
        per_head = []
        for m in head_masks:
            am = at * m
            rm = rt * m
            n_ab = jnp.where(strict, _dot_nt(am, bt), 0.0)
            n_ak = jnp.where(strict, _dot_nt(am, kt), 0.0)
            r_b = jnp.where(incl, _dot_nt(rm, bt), 0.0)
            r_k = jnp.where(incl, _dot_nt(rm, kt), 0.0)
            tinv = eye + n_ab
            pw = n_ab
            for _ in range(5):
                pw = _dot(pw, pw)
                tinv = tinv + _dot(tinv, pw)
            t_a = _dot(tinv, at)
            t_q = _dot(tinv, _dot(n_ak, vp))
            pm = pm + t_a * m
            qm = qm + t_q * m
            per_head.append((m, r_b, r_k))
        y1 = rt
        y0 = jnp.zeros((L, LANES), F32)
        for m, r_b, r_k in per_head:
            y1 = y1 + m * _dot(r_b, pm)
            y0 = y0 + m * (_dot(r_b, qm) + _dot(r_k, vp))
        gt = bdmask * _dot_tn(pm, bh)
        ht = bdmask * (_dot_tn(qm, bh) + _dot_tn(vp, kh))
        st = st_ref[p]
        ys.append(_dot_nt(y1, st) + y0)
        st_ref[p] = st * w_last[:, sl] + _dot(st, gt) + ht
    y = jnp.concatenate(ys, axis=1)

    inv_n = 1.0 / RW_HEAD_DIM
    mean = _seg_sum(y, bd) * inv_n
    dlt = y - mean
    var = _seg_sum(dlt * dlt, bd) * inv_n
    yn = dlt * lax.rsqrt(var + RW_GN_EPS) * lnw_ref[...] + lnb_ref[...]
    y_ref[...] = ((yn + bonus) * gate).astype(y_ref.dtype)


def _blockdiag_ones(width, head):
    idx = np.arange(width) // head
    return jnp.asarray((idx[:, None] == idx[None, :]).astype(np.float32), BF16)


def _tril_ones(n):
    return jnp.asarray(np.tril(np.ones((n, n), np.float32)), BF16)


def _rwkv_branch(u_rw, v_first, lp):
    t = u_rw.shape[0]
    L = CHUNK
    has_vres = v_first is not None
    W = RW_WIDTH
    vec = lambda a: a.reshape(1, -1).astype(F32)
    full = lambda shape: pl.BlockSpec(shape, lambda i: (0,) * len(shape))
    rows = lambda width: pl.BlockSpec((L, width), lambda i: (i, 0))
    wl = jnp.zeros((RW_LORA, 3 * W), F32)
    wl = wl.at[0:64, 0:W].set(lp['rw_w2']).at[64:128, W:2 * W].set(lp['rw_a2']).at[128:256, 2 * W:].set(lp['rw_g2'])
    ins = [u_rw]
    specs = [rows(RW_COLS)]
    if has_vres:
        ins.append(v_first)
        specs.append(rows(W))
    ins += [vec(lp['rw_mu']), wl.astype(BF16), vec(lp['rw_w0']), vec(lp['rw_a0']), vec(lp['rw_kk']),
            vec(lp['rw_ka']), vec(lp['rw_rk']), vec(lp['rw_lnw']), vec(lp['rw_lnb'])]
    specs += [full((1, RW_COLS)), full((RW_LORA, 3 * W))] + [full((1, W))] * 7
    if has_vres:
        v1 = jnp.zeros((W, LANES), F32).at[:, :lp['rw_v1'].shape[1]].set(lp['rw_v1'])
        v2 = jnp.zeros((LANES, W), F32).at[:lp['rw_v2'].shape[0], :].set(lp['rw_v2'])
        ins += [vec(lp['rw_v0']), v1.astype(BF16), v2.astype(BF16)]
        specs += [full((1, W)), full((W, LANES)), full((LANES, W))]
    ins += [_blockdiag_ones(W, RW_HEAD_DIM), _tril_ones(L)]
    specs += [full((W, W)), full((L, L))]
    out_shape = [jax.ShapeDtypeStruct((t, W), BF16)]
    out_specs = [rows(W)]
    if not has_vres:
        out_shape.append(jax.ShapeDtypeStruct((t, W), F32))
        out_specs.append(rows(W))
    outs = pl.pallas_call(
        functools.partial(_rwkv_kernel, has_vres=has_vres),
        grid=(t // L,),
        in_specs=specs,
        out_specs=out_specs,
        out_shape=out_shape,
        scratch_shapes=[pltpu.VMEM((L + SUBLANES, RW_COLS), F32),
                        pltpu.VMEM((RW_HEADS // 2, LANES, LANES), F32)],
        compiler_params=_params(("arbitrary",)),
        name="rwkv7_branch",
    )(*ins)
    if has_vres:
        return outs[0], v_first
    return outs[0], outs[1]


def _mlstm_kernel(qk_ref, v_ref, o_ref, g_ref, cw_ref, cb_ref, gb_ref, ng_ref, tril_ref,
                  y_ref, xbuf, c_ref, m_ref):
    L = CHUNK

    @pl.when(pl.program_id(0) == 0)
    def _():
        xbuf[0:SUBLANES, :] = jnp.zeros((SUBLANES, 2 * ML_QK_WIDTH), F32)
        c_ref[...] = jnp.zeros(c_ref.shape, F32)
        m_ref[...] = jnp.zeros(m_ref.shape, F32)

    x0 = qk_ref[...]
    xbuf[SUBLANES:SUBLANES + L, :] = x0
    conv = cb_ref[...] + cw_ref[ML_CONV - 1:ML_CONV, :] * x0
    for dly in range(1, ML_CONV):
        conv = conv + cw_ref[ML_CONV - 1 - dly:ML_CONV - dly, :] * xbuf[SUBLANES - dly:SUBLANES - dly + L, :]
    xbuf[0:SUBLANES, :] = x0[L - SUBLANES:L, :]
    qk = _silu(conv)
    q = qk[:, :ML_QK_WIDTH]
    k = qk[:, ML_QK_WIDTH:] * (ML_QK_DIM ** -0.5)

    pre = g_ref[...] + gb_ref[...]
    cap = ML_SOFTCAP * jnp.tanh(pre / ML_SOFTCAP)
    lane = lax.broadcasted_iota(jnp.int32, pre.shape, 1)
    gates = jnp.where(lane < ML_HEADS, cap, _log_sigmoid(cap))
    csum = _sel_dot(tril_ref[...], gates)
    gates_t = gates.T
    csum_t = csum.T

    row = lax.broadcasted_iota(jnp.int32, (L, L), 0)
    col = lax.broadcasted_iota(jnp.int32, (L, L), 1)
    causal = row >= col
    ones_col = (lax.broadcasted_iota(jnp.int32, (L, LANES), 1) == 0).astype(F32)
    vv = v_ref[...]
    og = o_ref[...]
    outs = []
    for h in range(ML_HEADS):
        qh = q[:, h * ML_QK_DIM:(h + 1) * ML_QK_DIM]
        kh = k[:, h * ML_QK_DIM:(h + 1) * ML_QK_DIM]
        vh = jnp.concatenate([vv[:, h * ML_V_DIM:(h + 1) * ML_V_DIM], ones_col], axis=1)
        b_col = csum[:, ML_HEADS + h:ML_HEADS + h + 1]
        b_row = csum_t[ML_HEADS + h:ML_HEADS + h + 1, :]
        li_col = gates[:, h:h + 1]
        li_row = gates_t[h:h + 1, :]
        m_prev = m_ref[h][0:1, 0:1]
        dmat = jnp.where(causal, b_col - b_row + li_row, -jnp.inf)
        m_inter = b_col + m_prev
        m_t = jnp.maximum(m_inter, jnp.max(dmat, axis=-1, keepdims=True))
        s = _dot_nt(qh, kh) * jnp.exp(dmat - m_t)
        inter = jnp.exp(m_inter - m_t)
        c_aug = c_ref[h]
        num_aug = _dot(s, vh) + inter * _dot(qh, c_aug)
        num = num_aug[:, :ML_V_DIM]
        den = num_aug[:, ML_V_DIM:ML_V_DIM + 1]
        hh = num / jnp.maximum(jnp.abs(den), jnp.exp(-m_t))
        b_last = b_col[L - 1:L, :]
        g = b_last - b_col + li_col
        m_new = jnp.maximum(b_last + m_prev, jnp.max(g, axis=0, keepdims=True))
        wgt = jnp.exp(g - m_new)
        keep = jnp.exp(b_last + m_prev - m_new)
        c_ref[h] = keep * c_aug + _dot_tn(wgt * kh, vh)
        m_ref[h] = jnp.broadcast_to(m_new, (SUBLANES, LANES))
        ng = ng_ref[:, h * ML_V_DIM:(h + 1) * ML_V_DIM]
        hn = hh * lax.rsqrt(jnp.mean(hh * hh, axis=-1, keepdims=True) + NORM_EPS) * ng
        outs.append(hn * _sigmoid(og[:, h * ML_V_DIM:(h + 1) * ML_V_DIM]))
    y_ref[...] = jnp.concatenate(outs, axis=1).astype(y_ref.dtype)


def _mlstm_branch(u_ml, u_mg, lp):
    t = u_ml.shape[0]
    L = CHUNK
    full = lambda shape: pl.BlockSpec(shape, lambda i: (0,) * len(shape))
    gb = jnp.zeros((1, LANES), F32).at[0, 0:ML_HEADS].set(lp['ml_ib']).at[0, ML_HEADS:2 * ML_HEADS].set(lp['ml_fb'])
    return pl.pallas_call(
        _mlstm_kernel,
        grid=(t // L,),
        in_specs=[pl.BlockSpec((L, ML_WIDTH), lambda i: (i, 0)),
                  pl.BlockSpec((L, ML_WIDTH), lambda i: (i, 1)),
                  pl.BlockSpec((L, ML_WIDTH), lambda i: (i, 2)),
                  pl.BlockSpec((L, LANES), lambda i: (i, 0)),
                  full((ML_CONV, 2 * ML_QK_WIDTH)), full((1, 2 * ML_QK_WIDTH)), full((1, LANES)),
                  full((1, ML_WIDTH)), full((L, L))],
        out_specs=pl.BlockSpec((L, ML_WIDTH), lambda i: (i, 0)),
        out_shape=jax.ShapeDtypeStruct((t, ML_WIDTH), BF16),
        scratch_shapes=[pltpu.VMEM((L + SUBLANES, 2 * ML_QK_WIDTH), F32),
                        pltpu.VMEM((ML_HEADS, ML_QK_DIM, ML_V_DIM + LANES), F32),
                        pltpu.VMEM((ML_HEADS, SUBLANES, LANES), F32)],
        compiler_params=_params(("arbitrary",)),
        name="mlstm_branch",
    )(u_ml, u_ml, u_ml, u_mg, lp['ml_conv_w'], lp['ml_conv_b'].reshape(1, -1), gb,
      lp['ml_norm'].reshape(1, -1), _tril_ones(L))


def _hgrn_levels():
    L = CHUNK
    t = np.arange(L)
    sel = [np.tril(np.ones((L, L), np.float32))]
    masks = []
    size = L
    while size >= 2:
        half = size // 2
        ref_row = (t // size) * size + half - 1
        sel.append((t[None, :] <= ref_row[:, None]).astype(np.float32))
        same = (t[:, None] // size) == (t[None, :] // size)
        masks.append((same & ((t[:, None] % size) >= half) & ((t[None, :] % size) < half)).astype(np.float32))
        size = half
    return np.concatenate(sel, axis=0), np.stack(masks)


def _hgrn_kernel(u_ref, la_ref, lc_ref, lb_ref, ng_ref, sel_ref, mask_ref, y_ref, s_ref, *, levels):
    L = CHUNK
    W = HG_WIDTH

    @pl.when(pl.program_id(0) == 0)
    def _():
        s_ref[...] = jnp.zeros(s_ref.shape, F32)

    u = u_ref[...]
    q = _silu(u[:, 0:W])
    f_pre = u[:, W:2 * W]
    ii = u[:, 2 * W:3 * W]
    g_pre = u[:, 3 * W:4 * W]
    la = la_ref[...]
    lc = lc_ref[...] + _log_sigmoid(f_pre)
    log_f = jnp.maximum(la, lc) + jnp.log1p(jnp.exp(-jnp.abs(la - lc)))
    k = (1.0 - lb_ref[...]) * _sigmoid(-f_pre)

    cr = _sel_dot(sel_ref[...], log_f)
    cum = cr[0:L, :]
    cl = cum[L - 1:L, :]
    q_in = q * jnp.exp(cum)
    k_out = k * jnp.exp(cl - cum)
    w_last = jnp.exp(cl)
    qe = []
    ke = []
    for lv in range(levels):
        ref = cr[(lv + 1) * L:(lv + 2) * L, :]
        qe.append(q * jnp.exp(jnp.minimum(cum - ref, 0.0)))
        ke.append(k * jnp.exp(jnp.minimum(ref - cum, 0.0)))
    outs = []
    for h in range(HG_HEADS):
        sl = slice(h * HG_DIM, (h + 1) * HG_DIM)
        att = jnp.zeros((L, L), F32)
        for lv in range(levels):
            att = att + mask_ref[lv] * _dot_nt(qe[lv][:, sl], ke[lv][:, sl])
        ih = ii[:, sl]
        diag = jnp.sum(q[:, sl] * k[:, sl], axis=-1, keepdims=True)
        st = s_ref[h]
        o = _dot(att, ih) + diag * ih + _dot_nt(q_in[:, sl], st)
        s_ref[h] = st * w_last[:, sl] + _dot_tn(ih, k_out[:, sl])
        on = o * lax.rsqrt(jnp.mean(o * o, axis=-1, keepdims=True) + NORM_EPS) * ng_ref[:, sl]
        outs.append(on * _silu(g_pre[:, sl]))
    y_ref[...] = jnp.concatenate(outs, axis=1).astype(y_ref.dtype)


def _hgrn_branch(u_hg, lb, lp):
    t = u_hg.shape[0]
    L = CHUNK
    sel, masks = _hgrn_levels()
    levels = masks.shape[0]
    full = lambda shape: pl.BlockSpec(shape, lambda i: (0,) * len(shape))
    lb = lb.reshape(1, -1).astype(F32)
    return pl.pallas_call(
        functools.partial(_hgrn_kernel, levels=levels),
        grid=(t // L,),
        in_specs=[pl.BlockSpec((L, 4 * HG_WIDTH), lambda i: (i, 0)),
                  full((1, HG_WIDTH)), full((1, HG_WIDTH)), full((1, HG_WIDTH)), full((1, HG_WIDTH)),
                  full(sel.shape), full(masks.shape)],
        out_specs=pl.BlockSpec((L, HG_WIDTH), lambda i: (i, 0)),
        out_shape=jax.ShapeDtypeStruct((t, HG_WIDTH), BF16),
        scratch_shapes=[pltpu.VMEM((HG_HEADS, HG_DIM, HG_DIM), F32)],
        compiler_params=_params(("arbitrary",)),
        name="hgrn2_branch",
    )(u_hg, jnp.log(lb), jnp.log1p(-lb), lb, lp['hg_norm'].reshape(1, -1),
      jnp.asarray(sel, BF16), jnp.asarray(masks, F32))


def _merge_kernel(h_ref, ya_ref, yb_ref, yc_ref, wga_ref, wgb_ref, wgc_ref, pa_ref, pb_ref, pc_ref, wo_ref,
                  x_ref, gt_ref, o_ref, acc_ref):
    n = pl.program_id(1)

    @pl.when(n == 0)
    def _():
        acc_ref[...] = jnp.zeros(acc_ref.shape, F32)

    h = h_ref[...]
    merged = (_sigmoid(_dot(h, wga_ref[...])) * _dot(ya_ref[...], pa_ref[...])
              + _sigmoid(_dot(h, wgb_ref[...])) * _dot(yb_ref[...], pb_ref[...])
              + _sigmoid(_dot(h, wgc_ref[...])) * _dot(yc_ref[...], pc_ref[...]))
    acc_ref[...] += _dot(merged, wo_ref[...])

    @pl.when(n == pl.num_programs(1) - 1)
    def _():
        o_ref[...] = x_ref[...] + gt_ref[...] * acc_ref[...]


def _merge_out(h, ya, yb, yc, w_gates, pa, pb, pc, wo, x, gt):
    t, d = x.shape
    tm = min(512, t)
    tn = 256
    nb = d // tn
    return pl.pallas_call(
        _merge_kernel,
        grid=(t // tm, nb),
        in_specs=[pl.BlockSpec((tm, d), lambda i, n: (i, 0)),
                  pl.BlockSpec((tm, RW_WIDTH), lambda i, n: (i, 0)),
                  pl.BlockSpec((tm, ML_WIDTH), lambda i, n: (i, 0)),
                  pl.BlockSpec((tm, HG_WIDTH), lambda i, n: (i, 0)),
                  pl.BlockSpec((d, tn), lambda i, n: (0, n)),
                  pl.BlockSpec((d, tn), lambda i, n: (0, nb + n)),
                  pl.BlockSpec((d, tn), lambda i, n: (0, 2 * nb + n)),
                  pl.BlockSpec((RW_WIDTH, tn), lambda i, n: (0, n)),
                  pl.BlockSpec((ML_WIDTH, tn), lambda i, n: (0, n)),
                  pl.BlockSpec((HG_WIDTH, tn), lambda i, n: (0, n)),
                  pl.BlockSpec((tn, d), lambda i, n: (n, 0)),
                  pl.BlockSpec((tm, d), lambda i, n: (i, 0)),
                  pl.BlockSpec((1, d), lambda i, n: (0, 0))],
        out_specs=pl.BlockSpec((tm, d), lambda i, n: (i, 0)),
        out_shape=jax.ShapeDtypeStruct((t, d), F32),
        scratch_shapes=[pltpu.VMEM((tm, d), F32)],
        compiler_params=_params(("arbitrary", "arbitrary")),
        name="merge_out",
    )(h, ya, yb, yc, w_gates, w_gates, w_gates, pa, pb, pc, wo, x, gt.reshape(1, d))


def _gather_kernel(nu_ref, tok_ref, src_ref, o_ref, sem):
    i = pl.program_id(0)
    rows = o_ref.shape[0]

    def row_copy(r, src_row):
        return pltpu.make_async_copy(src_ref.at[pl.ds(src_row, 1)], o_ref.at[pl.ds(r, 1)], sem)

    @pl.when(i < nu_ref[0])
    def _():
        def start(r, c):
            row_copy(r, tok_ref[0, 0, r]).start()
            return c

        def wait(r, c):
            row_copy(r, 0).wait()
            return c

        lax.fori_loop(0, rows, start, 0)
        lax.fori_loop(0, rows, wait, 0)

    @pl.when(i >= nu_ref[0])
    def _():
        o_ref[...] = jnp.zeros(o_ref.shape, o_ref.dtype)


def _moe_gather(h, row_token, n_used, n_blocks):
    d = h.shape[1]
    bm = MOE_ROWS
    return pl.pallas_call(
        _gather_kernel,
        grid_spec=pltpu.PrefetchScalarGridSpec(
            num_scalar_prefetch=1,
            grid=(n_blocks,),
            in_specs=[pl.BlockSpec((1, 1, bm), lambda i, nu: (i, 0, 0), memory_space=pltpu.SMEM),
                      pl.BlockSpec(memory_space=pl.ANY)],
            out_specs=pl.BlockSpec((bm, d), lambda i, nu: (i, 0)),
            scratch_shapes=[pltpu.SemaphoreType.DMA(())]),
        out_shape=jax.ShapeDtypeStruct((n_blocks * bm, d), h.dtype),
        compiler_params=_params(("arbitrary",)),
        name="moe_gather",
    )(n_used, row_token.reshape(n_blocks, 1, bm), h)


def _moe_up_kernel(be_ref, nu_ref, x_ref, wg_ref, wu_ref, o_ref, wg_bf, wu_bf):
    i = pl.program_id(0)
    prev = be_ref[jnp.maximum(i - 1, 0)]

    @pl.when((i == 0) | (be_ref[i] != prev))
    def _():
        wg_bf[...] = wg_ref[0].astype(BF16)
        wu_bf[...] = wu_ref[0].astype(BF16)

    @pl.when(i < nu_ref[0])
    def _():
        x = _bf(x_ref[...])
        g = jnp.dot(x, wg_bf[...], preferred_element_type=F32)
        u = jnp.dot(x, wu_bf[...], preferred_element_type=F32)
        o_ref[...] = (_silu(g) * u).astype(o_ref.dtype)

    @pl.when(i >= nu_ref[0])
    def _():
        o_ref[...] = jnp.zeros(o_ref.shape, o_ref.dtype)


def _moe_up(xs, w_gate, w_up, block_expert, n_used, n_blocks):
    d = xs.shape[1]
    ff = w_gate.shape[2]
    bm = MOE_ROWS
    return pl.pallas_call(
        _moe_up_kernel,
        grid_spec=pltpu.PrefetchScalarGridSpec(
            num_scalar_prefetch=2,
            grid=(n_blocks,),
            in_specs=[pl.BlockSpec((bm, d), lambda i, be, nu: (i, 0)),
                      pl.BlockSpec((1, d, ff), lambda i, be, nu: (be[i], 0, 0)),
                      pl.BlockSpec((1, d, ff), lambda i, be, nu: (be[i], 0, 0))],
            out_specs=pl.BlockSpec((bm, ff), lambda i, be, nu: (i, 0)),
            scratch_shapes=[pltpu.VMEM((d, ff), BF16), pltpu.VMEM((d, ff), BF16)]),
        out_shape=jax.ShapeDtypeStruct((n_blocks * bm, ff), BF16),
        compiler_params=_params(("arbitrary",)),
        name="moe_up",
    )(block_expert, n_used, xs, w_gate, w_up)


def _moe_down_kernel(be_ref, nu_ref, x_ref, wd_ref, o_ref, wd_bf):
    i = pl.program_id(0)
    prev = be_ref[jnp.maximum(i - 1, 0)]

    @pl.when((i == 0) | (be_ref[i] != prev))
    def _():
        wd_bf[...] = wd_ref[0].astype(BF16)

    @pl.when(i < nu_ref[0])
    def _():
        o_ref[...] = jnp.dot(x_ref[...], wd_bf[...], preferred_element_type=F32).astype(o_ref.dtype)

    @pl.when(i >= nu_ref[0])
    def _():
        o_ref[...] = jnp.zeros(o_ref.shape, o_ref.dtype)


def _moe_down(hmid, w_down, block_expert, n_used, n_blocks):
    ff = hmid.shape[1]
    d = w_down.shape[2]
    bm = MOE_ROWS
    return pl.pallas_call(
        _moe_down_kernel,
        grid_spec=pltpu.PrefetchScalarGridSpec(
            num_scalar_prefetch=2,
            grid=(n_blocks,),
            in_specs=[pl.BlockSpec((bm, ff), lambda i, be, nu: (i, 0)),
                      pl.BlockSpec((1, ff, d), lambda i, be, nu: (be[i], 0, 0))],
            out_specs=pl.BlockSpec((bm, d), lambda i, be, nu: (i, 0)),
            scratch_shapes=[pltpu.VMEM((ff, d), BF16)]),
        out_shape=jax.ShapeDtypeStruct((n_blocks * bm, d), F32),
        compiler_params=_params(("arbitrary",)),
        name="moe_down",
    )(block_expert, n_used, hmid, w_down)


def _combine_kernel(dest_ref, w_ref, x_ref, gt_ref, ys_ref, o_ref, buf, sem):
    tc = x_ref.shape[0]

    def row_copy(j, src_row):
        return pltpu.make_async_copy(ys_ref.at[pl.ds(src_row, 1)], buf.at[j % MOE_TOP_K, pl.ds(j // MOE_TOP_K, 1)], sem)

    def start(j, c):
        row_copy(j, dest_ref[0, 0, j]).start()
        return c

    def wait(j, c):
        row_copy(j, 0).wait()
        return c

    lax.fori_loop(0, MOE_TOP_K * tc, start, 0)
    lax.fori_loop(0, MOE_TOP_K * tc, wait, 0)
    w = w_ref[...]
    y = w[:, 0:1] * buf[0] + w[:, 1:2] * buf[1]
    o_ref[...] = x_ref[...] + gt_ref[...] * y


def _moe_combine(x, ys, dest, weights, gt):
    t, d = x.shape
    tc = min(256, t)
    return pl.pallas_call(
        _combine_kernel,
        grid=(t // tc,),
        in_specs=[pl.BlockSpec((1, 1, MOE_TOP_K * tc), lambda i: (i, 0, 0), memory_space=pltpu.SMEM),
                  pl.BlockSpec((tc, MOE_TOP_K), lambda i: (i, 0)),
                  pl.BlockSpec((tc, d), lambda i: (i, 0)),
                  pl.BlockSpec((1, d), lambda i: (0, 0)),
                  pl.BlockSpec(memory_space=pl.ANY)],
        out_specs=pl.BlockSpec((tc, d), lambda i: (i, 0)),
        out_shape=jax.ShapeDtypeStruct((t, d), F32),
        scratch_shapes=[pltpu.VMEM((MOE_TOP_K, tc, d), F32), pltpu.SemaphoreType.DMA(())],
        compiler_params=_params(("arbitrary",)),
        name="moe_combine",
    )(dest.reshape(t // tc, 1, MOE_TOP_K * tc), weights, x, gt.reshape(1, d), ys)


def _route(logits):
    t = logits.shape[0]
    g_logits = logits[:, :MOE_GROUPS]
    g_prob = jax.nn.softmax(g_logits, axis=-1)
    _, g_idx = lax.top_k(g_logits, 1)
    p_group = jnp.take_along_axis(g_prob, g_idx, axis=-1)
    e_logits = logits[:, MOE_GROUPS:MOE_GROUPS + MOE_EXPERTS].reshape(t, MOE_GROUPS, MOE_EPG)
    e_logits = jnp.take_along_axis(e_logits, g_idx[:, :, None], axis=1)[:, 0]
    e_top, e_local = lax.top_k(e_logits, MOE_TOP_K)
    weights = p_group * jax.nn.softmax(e_top, axis=-1)
    expert = g_idx * MOE_EPG + e_local
    return weights, expert.astype(jnp.int32)


def _dispatch_plan(expert, n_blocks):
    bm = MOE_ROWS
    flat = expert.reshape(-1)
    n_assign = flat.shape[0]
    onehot = (flat[:, None] == jnp.arange(MOE_EXPERTS, dtype=jnp.int32)[None, :]).astype(jnp.int32)
    rank = jnp.take_along_axis(jnp.cumsum(onehot, axis=0), flat[:, None], axis=1)[:, 0] - 1
    counts = jnp.sum(onehot, axis=0)
    padded = (counts + bm - 1) // bm * bm
    pad_end = jnp.cumsum(padded)
    dest = (pad_end - padded)[flat] + rank
    row_token = jnp.zeros((n_blocks * bm,), jnp.int32).at[dest].set(
        jnp.arange(n_assign, dtype=jnp.int32) // MOE_TOP_K)
    block_start = jnp.arange(n_blocks, dtype=jnp.int32) * bm
    block_expert = jnp.minimum(jnp.searchsorted(pad_end, block_start, side='right'), MOE_EXPERTS - 1)
    n_used = (pad_end[-1] // bm).reshape(1)
    return dest.astype(jnp.int32), row_token, block_expert.astype(jnp.int32), n_used.astype(jnp.int32)


def _hier_moe(x, h, lp, gt):
    t, d = x.shape
    w_r = jnp.zeros((d, LANES), F32).at[:, :MOE_GROUPS].set(lp['moe_gw'])
    w_r = w_r.at[:, MOE_GROUPS:MOE_GROUPS + MOE_EXPERTS].set(lp['moe_ew'])
    b_r = jnp.zeros((LANES,), F32).at[:MOE_GROUPS].set(lp['moe_gb'])
    b_r = b_r.at[MOE_GROUPS:MOE_GROUPS + MOE_EXPERTS].set(lp['moe_eb'])
    logits = _mm(h, w_r.astype(BF16), b_r, out_dtype=F32, tn=LANES)
    weights, expert = _route(logits)
    n_blocks = (t * MOE_TOP_K) // MOE_ROWS + MOE_EXPERTS
    dest, row_token, block_expert, n_used = _dispatch_plan(expert, n_blocks)
    xs = _moe_gather(h, row_token, n_used, n_blocks)
    hmid = _moe_up(xs, lp['ex_gate'], lp['ex_up'], block_expert, n_used, n_blocks)
    ys = _moe_down(hmid, lp['ex_down'], block_expert, n_used, n_blocks)
    return _moe_combine(x, ys, dest, weights, gt)


def kernel(x, c, ada_w, ada_b, norm_mix, norm_ffn, w_in, rw_mu, rw_w0, rw_w2, rw_a0, rw_a2, rw_g2, rw_kk, rw_ka, rw_rk, rw_lnw, rw_lnb, rw_v0, rw_v1, rw_v2, ml_conv_w, ml_conv_b, ml_ib, ml_fb, ml_norm, hg_lb, hg_norm, p_a, p_b, p_c, w_out, moe_gw, moe_gb, moe_ew, moe_eb, ex_gate, ex_up, ex_down, final_norm):
    batch, t, d = x.shape
    assert batch == 1 and d == D_MODEL and t % 512 == 0
    depth = ada_w.shape[0]
    x = x.reshape(t, d)
    lbs = jnp.cumsum(jax.nn.softmax(hg_lb.astype(F32), axis=0), axis=0)
    lbs = lbs - lbs[:1]
    mod = _adaln_mod(c, ada_w, ada_b)
    zeros_d = jnp.zeros((d,), F32)
    c_ml = RW_COLS
    c_mg = c_ml + 2 * ML_QK_WIDTH + 2 * ML_WIDTH
    c_hg = c_mg + 2 * ML_HEADS
    c_gt = c_hg + 4 * HG_WIDTH
    v_first = None
    for l in range(depth):
        sh_m, sc_m, gt_m, sh_f, sc_f, gt_f = jnp.split(mod[l], 6)
        lp = dict(rw_mu=rw_mu[l], rw_w0=rw_w0[l], rw_w2=rw_w2[l], rw_a0=rw_a0[l], rw_a2=rw_a2[l], rw_g2=rw_g2[l],
                  rw_kk=rw_kk[l], rw_ka=rw_ka[l], rw_rk=rw_rk[l], rw_lnw=rw_lnw[l], rw_lnb=rw_lnb[l],
                  ml_conv_w=ml_conv_w[l], ml_conv_b=ml_conv_b[l], ml_ib=ml_ib[l], ml_fb=ml_fb[l],
                  ml_norm=ml_norm[l], hg_norm=hg_norm[l],
                  moe_gw=moe_gw[l], moe_gb=moe_gb[l], moe_ew=moe_ew[l], moe_eb=moe_eb[l],
                  ex_gate=ex_gate[l], ex_up=ex_up[l], ex_down=ex_down[l])
        if l > 0:
            lp.update(rw_v0=rw_v0[l - 1], rw_v1=rw_v1[l - 1], rw_v2=rw_v2[l - 1])
        h = _norm(x, norm_mix[l], sc_m, sh_m, modulate=True, out_dtype=BF16)
        wl = w_in[l]
        u_rw = _mm(h, wl[:, :c_ml].astype(BF16), jnp.zeros((c_ml,), F32), out_dtype=F32, tn=c_ml)
        u_ml = _mm(h, wl[:, c_ml:c_mg].astype(BF16), jnp.zeros((c_mg - c_ml,), F32), out_dtype=F32, tn=ML_WIDTH)
        w_mg = jnp.zeros((d, LANES), F32).at[:, :2 * ML_HEADS].set(wl[:, c_mg:c_hg])
        u_mg = _mm(h, w_mg.astype(BF16), jnp.zeros((LANES,), F32), out_dtype=F32, tn=LANES)
        u_hg = _mm(h, wl[:, c_hg:c_gt].astype(BF16), jnp.zeros((c_gt - c_hg,), F32), out_dtype=F32, tn=4 * HG_WIDTH)
        y_a, v_first = _rwkv_branch(u_rw, v_first, lp)
        y_b = _mlstm_branch(u_ml, u_mg, lp)
        y_c = _hgrn_branch(u_hg, lbs[l], lp)
        x = _merge_out(h, y_a, y_b, y_c, wl[:, c_gt:].astype(BF16), p_a[l].astype(BF16), p_b[l].astype(BF16),
                       p_c[l].astype(BF16), w_out[l].astype(BF16), x, gt_m)
        h = _norm(x, norm_ffn[l], sc_f, sh_f, modulate=True, out_dtype=F32)
        x = _hier_moe(x, h, lp, gt_f)
    out = _norm(x, final_norm, zeros_d, zeros_d, modulate=False, out_dtype=F32)
    return out.reshape(batch, t, d)
```

```python
import functools

import numpy as np
import jax
import jax.numpy as jnp
from jax import lax
from jax.experimental import pallas as pl
from jax.experimental.pallas import tpu as pltpu

F32 = jnp.float32
BF16 = jnp.bfloat16

D_MODEL = 2048
DEPTH = 2
RW_HEADS, RW_HEAD_DIM, RW_WIDTH = 8, 64, 512
RW_LORA = 256
RW_COLS = 3 * RW_WIDTH + RW_LORA
RW_GN_EPS = 64e-5
RW_GROUP = 4
ML_HEADS, ML_QK_DIM, ML_V_DIM = 4, 128, 256
ML_QK_WIDTH, ML_WIDTH = 512, 1024
ML_CONV = 4
ML_SOFTCAP = 15.0
HG_HEADS, HG_DIM, HG_WIDTH = 4, 128, 512
CHUNK = 64
MOE_GROUPS, MOE_EPG, MOE_EXPERTS, MOE_TOP_K, MOE_D_FF = 4, 8, 32, 2, 1024
NORM_EPS = 1e-6

LANES = 128
SUBLANES = 8
MOE_ROWS = 256
DMA_PRIORITIES = 2
VMEM_LIMIT = 56 * 1024 * 1024


def _params(sem, limit=VMEM_LIMIT):
    return pltpu.CompilerParams(dimension_semantics=sem, vmem_limit_bytes=limit)


def _bf(x):
    return x.astype(BF16)


def _dot(a, b):
    return jnp.dot(_bf(a), _bf(b), preferred_element_type=F32)


def _dot_nt(a, b):
    return lax.dot_general(_bf(a), _bf(b), (((1,), (1,)), ((), ())), preferred_element_type=F32)


def _dot_tn(a, b):
    return lax.dot_general(_bf(a), _bf(b), (((0,), (0,)), ((), ())), preferred_element_type=F32)


def _split3(x):
    hi = x.astype(BF16)
    r1 = x - hi.astype(F32)
    mid = r1.astype(BF16)
    lo = (r1 - mid.astype(F32)).astype(BF16)
    return hi, mid, lo


def _sel_dot(sel, x):
    hi, mid, lo = _split3(x)
    return (jnp.dot(sel, hi, preferred_element_type=F32) + jnp.dot(sel, mid, preferred_element_type=F32)
            + jnp.dot(sel, lo, preferred_element_type=F32))


def _seg_sum(x, bd):
    hi, mid, lo = _split3(x)
    return (jnp.dot(hi, bd, preferred_element_type=F32) + jnp.dot(mid, bd, preferred_element_type=F32)
            + jnp.dot(lo, bd, preferred_element_type=F32))


def _sigmoid(x):
    return jax.nn.sigmoid(x)


def _silu(x):
    return x * jax.nn.sigmoid(x)


def _log_sigmoid(x):
    return jnp.minimum(x, 0.0) - jnp.log1p(jnp.exp(-jnp.abs(x)))


def _softplus(x):
    return jnp.maximum(x, 0.0) + jnp.log1p(jnp.exp(-jnp.abs(x)))


def _mod_kernel(c_ref, w_ref, b_ref, o_ref):
    cond = _silu(c_ref[...])
    o_ref[0] = _dot(cond, w_ref[0]) + b_ref[0]


def _adaln_mod(c, ada_w, ada_b):
    depth, d, n = ada_w.shape
    tn = 1024
    c8 = jnp.zeros((SUBLANES, d), F32).at[0].set(c[0])
    out = pl.pallas_call(
        _mod_kernel,
        grid=(depth, n // tn),
        in_specs=[pl.BlockSpec((SUBLANES, d), lambda l, j: (0, 0)),
                  pl.BlockSpec((1, d, tn), lambda l, j: (l, 0, j)),
                  pl.BlockSpec((1, 1, tn), lambda l, j: (l, 0, j))],
        out_specs=pl.BlockSpec((1, SUBLANES, tn), lambda l, j: (l, 0, j)),
        out_shape=jax.ShapeDtypeStruct((depth, SUBLANES, n), F32),
        compiler_params=_params(("arbitrary", "arbitrary")),
        name="adaln_mod",
    )(c8, ada_w, ada_b.reshape(depth, 1, n))
    return out[:, 0, :]


def _norm_kernel(x_ref, g_ref, sc_ref, sh_ref, o_ref, *, modulate):
    x = x_ref[...]
    y = x * lax.rsqrt(jnp.mean(x * x, axis=-1, keepdims=True) + NORM_EPS) * g_ref[...]
    if modulate:
        y = y * (1.0 + sc_ref[...]) + sh_ref[...]
    o_ref[...] = y.astype(o_ref.dtype)


def _norm(x, g, sc, sh, *, modulate, out_dtype):
    t, d = x.shape
    tm = min(512, t)
    row = pl.BlockSpec((1, d), lambda i: (0, 0))
    return pl.pallas_call(
        functools.partial(_norm_kernel, modulate=modulate),
        grid=(t // tm,),
        in_specs=[pl.BlockSpec((tm, d), lambda i: (i, 0)), row, row, row],
        out_specs=pl.BlockSpec((tm, d), lambda i: (i, 0)),
        out_shape=jax.ShapeDtypeStruct((t, d), out_dtype),
        compiler_params=_params(("arbitrary",)),
        name="rmsnorm_mod" if modulate else "rmsnorm",
    )(x, g.reshape(1, d), sc.reshape(1, d), sh.reshape(1, d))


def _mm_kernel(a_ref, w_ref, b_ref, o_ref):
    o_ref[...] = (_dot(a_ref[...], w_ref[...]) + b_ref[...]).astype(o_ref.dtype)


def _mm(a, w, bias, *, out_dtype, tn):
    m, k = a.shape
    n = w.shape[1]
    tm = min(512, m)
    return pl.pallas_call(
        _mm_kernel,
        grid=(n // tn, m // tm),
        in_specs=[pl.BlockSpec((tm, k), lambda j, i: (i, 0)),
                  pl.BlockSpec((k, tn), lambda j, i: (0, j)),
                  pl.BlockSpec((1, tn), lambda j, i: (0, j))],
        out_specs=pl.BlockSpec((tm, tn), lambda j, i: (i, j)),
        out_shape=jax.ShapeDtypeStruct((m, n), out_dtype),
        compiler_params=_params(("arbitrary", "arbitrary")),
        name="dense_matmul",
    )(a, w, bias.reshape(1, n))


def _rwkv_kernel(*refs, has_vres):
    if has_vres:
        (u_ref, vf_ref, mu_ref, wl_ref, w0_ref, a0_ref, kk_ref, ka_ref, rk_ref, lnw_ref, lnb_ref,
         v0_ref, v1_ref, v2_ref, bd_ref, tril_ref, y_ref, xbuf, st_ref) = refs
    else:
        (u_ref, mu_ref, wl_ref, w0_ref, a0_ref, kk_ref, ka_ref, rk_ref, lnw_ref, lnb_ref,
         bd_ref, tril_ref, y_ref, vf_out_ref, xbuf, st_ref) = refs
    L = CHUNK
    W = RW_WIDTH

    @pl.when(pl.program_id(0) == 0)
    def _():
        xbuf[0:SUBLANES, :] = jnp.zeros((SUBLANES, RW_COLS), F32)
        st_ref[...] = jnp.zeros(st_ref.shape, F32)

    u = u_ref[...]
    xbuf[SUBLANES:SUBLANES + L, :] = u
    prev = xbuf[SUBLANES - 1:SUBLANES - 1 + L, :]
    xbuf[0:SUBLANES, :] = u[L - SUBLANES:L, :]
    xs = u + mu_ref[...] * (prev - u)
    r = xs[:, 0:W]
    k = xs[:, W:2 * W]
    v = xs[:, 2 * W:3 * W]
    lr = xs[:, 3 * W:3 * W + RW_LORA]
    lane = lax.broadcasted_iota(jnp.int32, lr.shape, 1)
    act = jnp.where(lane < 64, jnp.tanh(lr), jnp.where(lane < 128, lr, _sigmoid(lr)))
    lo = _dot(act, wl_ref[...])
    z = w0_ref[...] + lo[:, 0:W]
    log_w = -_softplus(-z) - 0.5
    ld = -jnp.exp(log_w)
    iclr = _sigmoid(a0_ref[...] + lo[:, W:2 * W])
    gate = lo[:, 2 * W:3 * W]
    if has_vres:
        vv = _dot(_dot(v, v1_ref[...]), v2_ref[...])
        v = v + (vf_ref[...] - v) * _sigmoid(v0_ref[...] + vv)
    else:
        vf_out_ref[...] = v
    bd = bd_ref[...]
    kk = k * kk_ref[...]
    k = k * (1.0 + (iclr - 1.0) * ka_ref[...])
    sums = _seg_sum(jnp.concatenate([kk * kk, r * k * rk_ref[...]], axis=0), bd)
    kk = kk / jnp.maximum(jnp.sqrt(sums[0:L, :]), 1e-12)
    a = -kk
    b = kk * iclr
    bonus = sums[L:2 * L, :] * v

    cum = _sel_dot(tril_ref[...], ld)
    cl = cum[L - 1:L, :]
    e_neg = jnp.exp(-cum)
    e_last = jnp.exp(cl - cum)
    a_t = a * jnp.exp(cum - ld)
    r_t = r * jnp.exp(cum)
    b_t = b * e_neg
    k_t = k * e_neg
    b_h = b * e_last
    k_h = k * e_last
    w_last = jnp.exp(cl)

    G = RW_GROUP
    S = G * L
    row = lax.broadcasted_iota(jnp.int32, (S, S), 0)
    col = lax.broadcasted_iota(jnp.int32, (S, S), 1)
    same = (row // L) == (col // L)
    strict = same & (row > col)
    incl = same & (row >= col)
    eye = (row == col).astype(F32)
    lane_head = lax.broadcasted_iota(jnp.int32, (1, S), 1) // RW_HEAD_DIM
    head_masks = [(lane_head == j).astype(F32) for j in range(G)]

    ys = []
    for g in range(RW_HEADS // G):
        sl = slice(g * S, (g + 1) * S)

        def stack(x):
            return jnp.concatenate([x[:, sl] * m for m in head_masks], axis=0)

        def tile(x):
            return jnp.concatenate([x[:, sl]] * G, axis=0)

        a_s, r_s, v_s, bh_s, kh_s = stack(a_t), stack(r_t), stack(v), stack(b_h), stack(k_h)
        x = _dot_nt(jnp.concatenate([a_s, r_s], axis=0), jnp.concatenate([tile(b_t), tile(k_t)], axis=0))
        n_ab = jnp.where(strict, x[0:S, 0:S], 0.0)
        n_ak = jnp.where(strict, x[0:S, S:2 * S], 0.0)
        r_b = jnp.where(incl, x[S:2 * S, 0:S], 0.0)
        r_k = jnp.where(incl, x[S:2 * S, S:2 * S], 0.0)
        tinv = eye + n_ab
        pw = n_ab
        for _ in range(5):
            pw = _dot(pw, pw)
            tinv = tinv + _dot(tinv, pw)
        pq = _dot(tinv, jnp.concatenate([a_s, _dot(n_ak, v_s)], axis=1))
        p_s = pq[:, 0:S]
        qv = jnp.concatenate([pq[:, S:2 * S], v_s], axis=0)
        y1 = r_s + _dot(r_b, p_s)
        y0 = _dot(jnp.concatenate([r_b, r_k], axis=1), qv)
        gt = _dot_tn(p_s, bh_s)
        ht = _dot_tn(qv, jnp.concatenate([bh_s, kh_s], axis=0))
        st = st_ref[g]
        y_s = _dot_nt(y1, st) + y0
        st_ref[g] = st * w_last[:, sl] + _dot(st, gt) + ht
        y_g = y_s[0:L, :]
        for j in range(1, G):
            y_g = y_g + y_s[j * L:(j + 1) * L, :]
        ys.append(y_g)
    y = jnp.concatenate(ys, axis=1)

    inv_n = 1.0 / RW_HEAD_DIM
    mean = _seg_sum(y, bd) * inv_n
    dlt = y - mean
    var = _seg_sum(dlt * dlt, bd) * inv_n
    yn = dlt * lax.rsqrt(var + RW_GN_EPS) * lnw_ref[...] + lnb_ref[...]
    y_ref[...] = ((yn + bonus) * gate).astype(y_ref.dtype)


def _blockdiag_ones(width, head):
    idx = np.arange(width) // head
    return jnp.asarray((idx[:, None] == idx[None, :]).astype(np.float32), BF16)


def _tril_ones(n):
    return jnp.asarray(np.tril(np.ones((n, n), np.float32)), BF16)


def _rwkv_branch(u_rw, v_first, lp):
    t = u_rw.shape[0]
    L = CHUNK
    has_vres = v_first is not None
    W = RW_WIDTH
    vec = lambda a: a.reshape(1, -1).astype(F32)
    full = lambda shape: pl.BlockSpec(shape, lambda i: (0,) * len(shape))
    rows = lambda width: pl.BlockSpec((L, width), lambda i: (i, 0))
    wl = jnp.zeros((RW_LORA, 3 * W), F32)
    wl = wl.at[0:64, 0:W].set(lp['rw_w2']).at[64:128, W:2 * W].set(lp['rw_a2']).at[128:256, 2 * W:].set(lp['rw_g2'])
    ins = [u_rw]
    specs = [rows(RW_COLS)]
    if has_vres:
        ins.append(v_first)
        specs.append(rows(W))
    ins += [vec(lp['rw_mu']), wl.astype(BF16), vec(lp['rw_w0']), vec(lp['rw_a0']), vec(lp['rw_kk']),
            vec(lp['rw_ka']), vec(lp['rw_rk']), vec(lp['rw_lnw']), vec(lp['rw_lnb'])]
    specs += [full((1, RW_COLS)), full((RW_LORA, 3 * W))] + [full((1, W))] * 7
    if has_vres:
        v1 = jnp.zeros((W, LANES), F32).at[:, :lp['rw_v1'].shape[1]].set(lp['rw_v1'])
        v2 = jnp.zeros((LANES, W), F32).at[:lp['rw_v2'].shape[0], :].set(lp['rw_v2'])
        ins += [vec(lp['rw_v0']), v1.astype(BF16), v2.astype(BF16)]
        specs += [full((1, W)), full((W, LANES)), full((LANES, W))]
    ins += [_blockdiag_ones(W, RW_HEAD_DIM), _tril_ones(L)]
    specs += [full((W, W)), full((L, L))]
    out_shape = [jax.ShapeDtypeStruct((t, W), BF16)]
    out_specs = [rows(W)]
    if not has_vres:
        out_shape.append(jax.ShapeDtypeStruct((t, W), F32))
        out_specs.append(rows(W))
    outs = pl.pallas_call(
        functools.partial(_rwkv_kernel, has_vres=has_vres),
        grid=(t // L,),
        in_specs=specs,
        out_specs=out_specs,
        out_shape=out_shape,
        scratch_shapes=[pltpu.VMEM((L + SUBLANES, RW_COLS), F32),
                        pltpu.VMEM((RW_HEADS // RW_GROUP, RW_GROUP * RW_HEAD_DIM, RW_GROUP * RW_HEAD_DIM), F32)],
        compiler_params=_params(("arbitrary",)),
        name="rwkv7_branch",
    )(*ins)
    if has_vres:
        return outs[0], v_first
    return outs[0], outs[1]


def _mlstm_kernel(qk_ref, v_ref, o_ref, g_ref, cw_ref, cb_ref, gb_ref, ng_ref, tril_ref,
                  y_ref, xbuf, c_ref, m_ref):
    L = CHUNK

    @pl.when(pl.program_id(0) == 0)
    def _():
        xbuf[0:SUBLANES, :] = jnp.zeros((SUBLANES, 2 * ML_QK_WIDTH), F32)
        c_ref[...] = jnp.zeros(c_ref.shape, F32)
        m_ref[...] = jnp.zeros(m_ref.shape, F32)

    x0 = qk_ref[...]
    xbuf[SUBLANES:SUBLANES + L, :] = x0
    conv = cb_ref[...] + cw_ref[ML_CONV - 1:ML_CONV, :] * x0
    for dly in range(1, ML_CONV):
        conv = conv + cw_ref[ML_CONV - 1 - dly:ML_CONV - dly, :] * xbuf[SUBLANES - dly:SUBLANES - dly + L, :]
    xbuf[0:SUBLANES, :] = x0[L - SUBLANES:L, :]
    qk = _silu(conv)
    q = qk[:, :ML_QK_WIDTH]
    k = qk[:, ML_QK_WIDTH:] * (ML_QK_DIM ** -0.5)

    pre = g_ref[...] + gb_ref[...]
    cap = ML_SOFTCAP * jnp.tanh(pre / ML_SOFTCAP)
    lane = lax.broadcasted_iota(jnp.int32, pre.shape, 1)
    gates = jnp.where(lane < ML_HEADS, cap, _log_sigmoid(cap))
    csum = _sel_dot(tril_ref[...], gates)
    gates_t = gates.T
    csum_t = csum.T

    row = lax.broadcasted_iota(jnp.int32, (L, L), 0)
    col = lax.broadcasted_iota(jnp.int32, (L, L), 1)
    causal = row >= col
    ones_col = (lax.broadcasted_iota(jnp.int32, (L, LANES), 1) == 0).astype(F32)
    vv = v_ref[...]
    og = o_ref[...]
    outs = []
    for h in range(ML_HEADS):
        qh = q[:, h * ML_QK_DIM:(h + 1) * ML_QK_DIM]
        kh = k[:, h * ML_QK_DIM:(h + 1) * ML_QK_DIM]
        vh = jnp.concatenate([vv[:, h * ML_V_DIM:(h + 1) * ML_V_DIM], ones_col], axis=1)
        b_col = csum[:, ML_HEADS + h:ML_HEADS + h + 1]
        b_row = csum_t[ML_HEADS + h:ML_HEADS + h + 1, :]
        li_col = gates[:, h:h + 1]
        li_row = gates_t[h:h + 1, :]
        m_prev = m_ref[h][0:1, 0:1]
        dmat = jnp.where(causal, b_col - b_row + li_row, -jnp.inf)
        m_inter = b_col + m_prev
        m_t = jnp.maximum(m_inter, jnp.max(dmat, axis=-1, keepdims=True))
        s = _dot_nt(qh, kh) * jnp.exp(dmat - m_t)
        inter = jnp.exp(m_inter - m_t)
        c_aug = c_ref[h]
        num_aug = _dot(s, vh) + inter * _dot(qh, c_aug)
        num = num_aug[:, :ML_V_DIM]
        den = num_aug[:, ML_V_DIM:ML_V_DIM + 1]
        hh = num / jnp.maximum(jnp.abs(den), jnp.exp(-m_t))
        b_last = b_col[L - 1:L, :]
        g = b_last - b_col + li_col
        m_new = jnp.maximum(b_last + m_prev, jnp.max(g, axis=0, keepdims=True))
        wgt = jnp.exp(g - m_new)
        keep = jnp.exp(b_last + m_prev - m_new)
        c_ref[h] = keep * c_aug + _dot_tn(wgt * kh, vh)
        m_ref[h] = jnp.broadcast_to(m_new, (SUBLANES, LANES))
        ng = ng_ref[:, h * ML_V_DIM:(h + 1) * ML_V_DIM]
        hn = hh * lax.rsqrt(jnp.mean(hh * hh, axis=-1, keepdims=True) + NORM_EPS) * ng
        outs.append(hn * _sigmoid(og[:, h * ML_V_DIM:(h + 1) * ML_V_DIM]))
    y_ref[...] = jnp.concatenate(outs, axis=1).astype(y_ref.dtype)


def _mlstm_branch(u_ml, u_mg, lp):
    t = u_ml.shape[0]
    L = CHUNK
    full = lambda shape: pl.BlockSpec(shape, lambda i: (0,) * len(shape))
    gb = jnp.zeros((1, LANES), F32).at[0, 0:ML_HEADS].set(lp['ml_ib']).at[0, ML_HEADS:2 * ML_HEADS].set(lp['ml_fb'])
    return pl.pallas_call(
        _mlstm_kernel,
        grid=(t // L,),
        in_specs=[pl.BlockSpec((L, ML_WIDTH), lambda i: (i, 0)),
                  pl.BlockSpec((L, ML_WIDTH), lambda i: (i, 1)),
                  pl.BlockSpec((L, ML_WIDTH), lambda i: (i, 2)),
                  pl.BlockSpec((L, LANES), lambda i: (i, 0)),
                  full((ML_CONV, 2 * ML_QK_WIDTH)), full((1, 2 * ML_QK_WIDTH)), full((1, LANES)),
                  full((1, ML_WIDTH)), full((L, L))],
        out_specs=pl.BlockSpec((L, ML_WIDTH), lambda i: (i, 0)),
        out_shape=jax.ShapeDtypeStruct((t, ML_WIDTH), BF16),
        scratch_shapes=[pltpu.VMEM((L + SUBLANES, 2 * ML_QK_WIDTH), F32),
                        pltpu.VMEM((ML_HEADS, ML_QK_DIM, ML_V_DIM + LANES), F32),
                        pltpu.VMEM((ML_HEADS, SUBLANES, LANES), F32)],
        compiler_params=_params(("arbitrary",)),
        name="mlstm_branch",
    )(u_ml, u_ml, u_ml, u_mg, lp['ml_conv_w'], lp['ml_conv_b'].reshape(1, -1), gb,
      lp['ml_norm'].reshape(1, -1), _tril_ones(L))


def _hgrn_levels():
    L = CHUNK
    t = np.arange(L)
    sel = [np.tril(np.ones((L, L), np.float32))]
    masks = []
    size = L
    while size >= 2:
        half = size // 2
        ref_row = (t // size) * size + half - 1
        sel.append((t[None, :] <= ref_row[:, None]).astype(np.float32))
        same = (t[:, None] // size) == (t[None, :] // size)
        masks.append((same & ((t[:, None] % size) >= half) & ((t[None, :] % size) < half)).astype(np.float32))
        size = half
    return np.concatenate(sel, axis=0), np.stack(masks)


def _hgrn_kernel(u_ref, la_ref, lc_ref, lb_ref, ng_ref, sel_ref, mask_ref, y_ref, s_ref, *, levels):
    L = CHUNK
    W = HG_WIDTH

    @pl.when(pl.program_id(0) == 0)
    def _():
        s_ref[...] = jnp.zeros(s_ref.shape, F32)

    u = u_ref[...]
    q = _silu(u[:, 0:W])
    f_pre = u[:, W:2 * W]
    ii = u[:, 2 * W:3 * W]
    g_pre = u[:, 3 * W:4 * W]
    la = la_ref[...]
    lc = lc_ref[...] + _log_sigmoid(f_pre)
    log_f = jnp.maximum(la, lc) + jnp.log1p(jnp.exp(-jnp.abs(la - lc)))
    k = (1.0 - lb_ref[...]) * _sigmoid(-f_pre)

    cr = _sel_dot(sel_ref[...], log_f)
    cum = cr[0:L, :]
    cl = cum[L - 1:L, :]
    q_in = q * jnp.exp(cum)
    k_out = k * jnp.exp(cl - cum)
    w_last = jnp.exp(cl)
    qe = []
    ke = []
    for lv in range(levels):
        ref = cr[(lv + 1) * L:(lv + 2) * L, :]
        qe.append(q * jnp.exp(jnp.minimum(cum - ref, 0.0)))
        ke.append(k * jnp.exp(jnp.minimum(ref - cum, 0.0)))
    outs = []
    for h in range(HG_HEADS):
        sl = slice(h * HG_DIM, (h + 1) * HG_DIM)
        att = jnp.zeros((L, L), F32)
        for lv in range(levels):
            att = att + mask_ref[lv] * _dot_nt(qe[lv][:, sl], ke[lv][:, sl])
        ih = ii[:, sl]
        diag = jnp.sum(q[:, sl] * k[:, sl], axis=-1, keepdims=True)
        st = s_ref[h]
        o = _dot(att, ih) + diag * ih + _dot_nt(q_in[:, sl], st)
        s_ref[h] = st * w_last[:, sl] + _dot_tn(ih, k_out[:, sl])
        on = o * lax.rsqrt(jnp.mean(o * o, axis=-1, keepdims=True) + NORM_EPS) * ng_ref[:, sl]
        outs.append(on * _silu(g_pre[:, sl]))
    y_ref[...] = jnp.concatenate(outs, axis=1).astype(y_ref.dtype)


def _hgrn_branch(u_hg, lb, lp):
    t = u_hg.shape[0]
    L = CHUNK
    sel, masks = _hgrn_levels()
    levels = masks.shape[0]
    full = lambda shape: pl.BlockSpec(shape, lambda i: (0,) * len(shape))
    lb = lb.reshape(1, -1).astype(F32)
    return pl.pallas_call(
        functools.partial(_hgrn_kernel, levels=levels),
        grid=(t // L,),
        in_specs=[pl.BlockSpec((L, 4 * HG_WIDTH), lambda i: (i, 0)),
                  full((1, HG_WIDTH)), full((1, HG_WIDTH)), full((1, HG_WIDTH)), full((1, HG_WIDTH)),
                  full(sel.shape), full(masks.shape)],
        out_specs=pl.BlockSpec((L, HG_WIDTH), lambda i: (i, 0)),
        out_shape=jax.ShapeDtypeStruct((t, HG_WIDTH), BF16),
        scratch_shapes=[pltpu.VMEM((HG_HEADS, HG_DIM, HG_DIM), F32)],
        compiler_params=_params(("arbitrary",)),
        name="hgrn2_branch",
    )(u_hg, jnp.log(lb), jnp.log1p(-lb), lb, lp['hg_norm'].reshape(1, -1),
      jnp.asarray(sel, BF16), jnp.asarray(masks, F32))


def _merge_kernel(h_ref, ya_ref, yb_ref, yc_ref, wga_ref, wgb_ref, wgc_ref, pa_ref, pb_ref, pc_ref, wo_ref,
                  x_ref, gt_ref, o_ref, acc_ref):
    n = pl.program_id(1)

    @pl.when(n == 0)
    def _():
        acc_ref[...] = jnp.zeros(acc_ref.shape, F32)

    h = h_ref[...]
    merged = (_sigmoid(_dot(h, wga_ref[...])) * _dot(ya_ref[...], pa_ref[...])
              + _sigmoid(_dot(h, wgb_ref[...])) * _dot(yb_ref[...], pb_ref[...])
              + _sigmoid(_dot(h, wgc_ref[...])) * _dot(yc_ref[...], pc_ref[...]))
    acc_ref[...] += _dot(merged, wo_ref[...])

    @pl.when(n == pl.num_programs(1) - 1)
    def _():
        o_ref[...] = x_ref[...] + gt_ref[...] * acc_ref[...]


def _merge_out(h, ya, yb, yc, w_gates, pa, pb, pc, wo, x, gt):
    t, d = x.shape
    tm = min(512, t)
    tn = 256
    nb = d // tn
    return pl.pallas_call(
        _merge_kernel,
        grid=(t // tm, nb),
        in_specs=[pl.BlockSpec((tm, d), lambda i, n: (i, 0)),
                  pl.BlockSpec((tm, RW_WIDTH), lambda i, n: (i, 0)),
                  pl.BlockSpec((tm, ML_WIDTH), lambda i, n: (i, 0)),
                  pl.BlockSpec((tm, HG_WIDTH), lambda i, n: (i, 0)),
                  pl.BlockSpec((d, tn), lambda i, n: (0, n)),
                  pl.BlockSpec((d, tn), lambda i, n: (0, nb + n)),
                  pl.BlockSpec((d, tn), lambda i, n: (0, 2 * nb + n)),
                  pl.BlockSpec((RW_WIDTH, tn), lambda i, n: (0, n)),
                  pl.BlockSpec((ML_WIDTH, tn), lambda i, n: (0, n)),
                  pl.BlockSpec((HG_WIDTH, tn), lambda i, n: (0, n)),
                  pl.BlockSpec((tn, d), lambda i, n: (n, 0)),
                  pl.BlockSpec((tm, d), lambda i, n: (i, 0)),
                  pl.BlockSpec((1, d), lambda i, n: (0, 0))],
        out_specs=pl.BlockSpec((tm, d), lambda i, n: (i, 0)),
        out_shape=jax.ShapeDtypeStruct((t, d), F32),
        scratch_shapes=[pltpu.VMEM((tm, d), F32)],
        compiler_params=_params(("arbitrary", "arbitrary")),
        name="merge_out",
    )(h, ya, yb, yc, w_gates, w_gates, w_gates, pa, pb, pc, wo, x, gt.reshape(1, d))


def _gather_kernel(nu_ref, tok_ref, src_ref, o_ref, sem):
    i = pl.program_id(0)
    rows = o_ref.shape[0]

    def row_copy(r, src_row):
        return pltpu.make_async_copy(src_ref.at[pl.ds(src_row, 1)], o_ref.at[pl.ds(r, 1)], sem)

    @pl.when(i < nu_ref[0])
    def _():
        def start(q, c):
            for prio in range(DMA_PRIORITIES):
                r = DMA_PRIORITIES * q + prio
                row_copy(r, tok_ref[0, 0, r]).start(priority=prio)
            return c

        def wait(r, c):
            row_copy(r, 0).wait()
            return c

        lax.fori_loop(0, rows // DMA_PRIORITIES, start, 0, unroll=4)
        lax.fori_loop(0, rows, wait, 0, unroll=8)

    @pl.when(i >= nu_ref[0])
    def _():
        o_ref[...] = jnp.zeros(o_ref.shape, o_ref.dtype)


def _moe_gather(h, row_token, n_used, n_blocks):
    d = h.shape[1]
    bm = MOE_ROWS
    return pl.pallas_call(
        _gather_kernel,
        grid_spec=pltpu.PrefetchScalarGridSpec(
            num_scalar_prefetch=1,
            grid=(n_blocks,),
            in_specs=[pl.BlockSpec((1, 1, bm), lambda i, nu: (i, 0, 0), memory_space=pltpu.SMEM),
                      pl.BlockSpec(memory_space=pl.ANY)],
            out_specs=pl.BlockSpec((bm, d), lambda i, nu: (i, 0)),
            scratch_shapes=[pltpu.SemaphoreType.DMA(())]),
        out_shape=jax.ShapeDtypeStruct((n_blocks * bm, d), h.dtype),
        compiler_params=_params(("arbitrary",)),
        name="moe_gather",
    )(n_used, row_token.reshape(n_blocks, 1, bm), h)


def _moe_up_kernel(be_ref, nu_ref, x_ref, wg_ref, wu_ref, o_ref, wg_bf, wu_bf):
    i = pl.program_id(0)
    prev = be_ref[jnp.maximum(i - 1, 0)]

    @pl.when((i == 0) | (be_ref[i] != prev))
    def _():
        wg_bf[...] = wg_ref[0, 0].astype(BF16)
        wu_bf[...] = wu_ref[0, 0].astype(BF16)

    @pl.when(i < nu_ref[0])
    def _():
        x = _bf(x_ref[...])
        g = jnp.dot(x, wg_bf[...], preferred_element_type=F32)
        u = jnp.dot(x, wu_bf[...], preferred_element_type=F32)
        o_ref[...] = (_silu(g) * u).astype(o_ref.dtype)

    @pl.when(i >= nu_ref[0])
    def _():
        o_ref[...] = jnp.zeros(o_ref.shape, o_ref.dtype)


def _moe_up(xs, w_gate, w_up, layer, block_expert, n_used, n_blocks):
    d = xs.shape[1]
    ff = w_gate.shape[3]
    bm = MOE_ROWS
    return pl.pallas_call(
        _moe_up_kernel,
        grid_spec=pltpu.PrefetchScalarGridSpec(
            num_scalar_prefetch=2,
            grid=(n_blocks,),
            in_specs=[pl.BlockSpec((bm, d), lambda i, be, nu: (i, 0)),
                      pl.BlockSpec((1, 1, d, ff), lambda i, be, nu: (layer, be[i], 0, 0)),
                      pl.BlockSpec((1, 1, d, ff), lambda i, be, nu: (layer, be[i], 0, 0))],
            out_specs=pl.BlockSpec((bm, ff), lambda i, be, nu: (i, 0)),
            scratch_shapes=[pltpu.VMEM((d, ff), BF16), pltpu.VMEM((d, ff), BF16)]),
        out_shape=jax.ShapeDtypeStruct((n_blocks * bm, ff), BF16),
        compiler_params=_params(("arbitrary",)),
        name="moe_up",
    )(block_expert, n_used, xs, w_gate, w_up)


def _moe_down_kernel(be_ref, nu_ref, x_ref, wd_ref, o_ref, wd_bf):
    i = pl.program_id(0)
    prev = be_ref[jnp.maximum(i - 1, 0)]

    @pl.when((i == 0) | (be_ref[i] != prev))
    def _():
        wd_bf[...] = wd_ref[0, 0].astype(BF16)

    @pl.when(i < nu_ref[0])
    def _():
        o_ref[...] = jnp.dot(x_ref[...], wd_bf[...], preferred_element_type=F32).astype(o_ref.dtype)

    @pl.when(i >= nu_ref[0])
    def _():
        o_ref[...] = jnp.zeros(o_ref.shape, o_ref.dtype)


def _moe_down(hmid, w_down, layer, block_expert, n_used, n_blocks):
    ff = hmid.shape[1]
    d = w_down.shape[3]
    bm = MOE_ROWS
    return pl.pallas_call(
        _moe_down_kernel,
        grid_spec=pltpu.PrefetchScalarGridSpec(
            num_scalar_prefetch=2,
            grid=(n_blocks,),
            in_specs=[pl.BlockSpec((bm, ff), lambda i, be, nu: (i, 0)),
                      pl.BlockSpec((1, 1, ff, d), lambda i, be, nu: (layer, be[i], 0, 0))],
            out_specs=pl.BlockSpec((bm, d), lambda i, be, nu: (i, 0)),
            scratch_shapes=[pltpu.VMEM((ff, d), BF16)]),
        out_shape=jax.ShapeDtypeStruct((n_blocks * bm, d), F32),
        compiler_params=_params(("arbitrary",)),
        name="moe_down",
    )(block_expert, n_used, hmid, w_down)


def _combine_kernel(dest_ref, w_ref, x_ref, gt_ref, ys_ref, o_ref, buf, sem):
    tc = x_ref.shape[0]

    def row_copy(tok, choice, src_row):
        return pltpu.make_async_copy(ys_ref.at[pl.ds(src_row, 1)], buf.at[choice, pl.ds(tok, 1)], sem)

    def start(tok, c):
        for choice in range(MOE_TOP_K):
            row_copy(tok, choice, dest_ref[0, 0, MOE_TOP_K * tok + choice]).start(priority=choice)
        return c

    def wait(tok, c):
        for choice in range(MOE_TOP_K):
            row_copy(tok, choice, 0).wait()
        return c

    lax.fori_loop(0, tc, start, 0, unroll=4)
    lax.fori_loop(0, tc, wait, 0, unroll=4)
    w = w_ref[...]
    y = w[:, 0:1] * buf[0] + w[:, 1:2] * buf[1]
    o_ref[...] = x_ref[...] + gt_ref[...] * y


def _moe_combine(x, ys, dest, weights, gt):
    t, d = x.shape
    tc = min(256, t)
    return pl.pallas_call(
        _combine_kernel,
        grid=(t // tc,),
        in_specs=[pl.BlockSpec((1, 1, MOE_TOP_K * tc), lambda i: (i, 0, 0), memory_space=pltpu.SMEM),
                  pl.BlockSpec((tc, MOE_TOP_K), lambda i: (i, 0)),
                  pl.BlockSpec((tc, d), lambda i: (i, 0)),
                  pl.BlockSpec((1, d), lambda i: (0, 0)),
                  pl.BlockSpec(memory_space=pl.ANY)],
        out_specs=pl.BlockSpec((tc, d), lambda i: (i, 0)),
        out_shape=jax.ShapeDtypeStruct((t, d), F32),
        scratch_shapes=[pltpu.VMEM((MOE_TOP_K, tc, d), F32), pltpu.SemaphoreType.DMA(())],
        compiler_params=_params(("arbitrary",)),
        name="moe_combine",
    )(dest.reshape(t // tc, 1, MOE_TOP_K * tc), weights, x, gt.reshape(1, d), ys)


def _router_kernel(h_ref, w_ref, b_ref, tril_ref, route_ref, count_ref, run_ref):
    @pl.when(pl.program_id(0) == 0)
    def _():
        run_ref[...] = jnp.zeros(run_ref.shape, F32)

    logits = _dot(h_ref[...], w_ref[...]) + b_ref[...]
    lane = lax.broadcasted_iota(jnp.int32, logits.shape, 1)
    neg = -jnp.inf
    far = jnp.int32(LANES)

    def first_max(vals):
        top = jnp.max(vals, axis=-1, keepdims=True)
        return top, jnp.min(jnp.where(vals == top, lane, far), axis=-1, keepdims=True)

    is_group = lane < MOE_GROUPS
    g_top, g_idx = first_max(jnp.where(is_group, logits, neg))
    p_group = 1.0 / jnp.sum(jnp.where(is_group, jnp.exp(logits - g_top), 0.0), axis=-1, keepdims=True)
    first = MOE_GROUPS + MOE_EPG * g_idx
    cand = jnp.where((lane >= first) & (lane < first + MOE_EPG), logits, neg)
    e1, i1 = first_max(cand)
    e2, i2 = first_max(jnp.where(lane == i1, neg, cand))
    z = jnp.exp(e2 - e1)
    w1 = p_group / (1.0 + z)
    w2 = p_group * z / (1.0 + z)
    x1 = i1 - MOE_GROUPS
    x2 = i2 - MOE_GROUPS
    hit1 = lane == x1
    hit2 = lane == x2
    onehot = (hit1 | hit2).astype(F32)
    cum = jnp.dot(tril_ref[...], _bf(onehot), preferred_element_type=F32) + run_ref[0:1, :]
    rank1 = jnp.sum(jnp.where(hit1, cum, 0.0), axis=-1, keepdims=True) - 1.0
    rank2 = jnp.sum(jnp.where(hit2, cum, 0.0), axis=-1, keepdims=True) - 1.0
    last = cum[cum.shape[0] - 1:cum.shape[0], :]
    run_ref[...] = jnp.broadcast_to(last, run_ref.shape)
    count_ref[...] = jnp.broadcast_to(last, count_ref.shape)
    cols = (w1, w2, x1.astype(F32), x2.astype(F32), rank1, rank2)
    packed = jnp.zeros(logits.shape, F32)
    for j, cvals in enumerate(cols):
        packed = jnp.where(lane == j, cvals, packed)
    route_ref[...] = packed


def _router(h, w_r, b_r):
    t, d = h.shape
    tm = MOE_ROWS
    return pl.pallas_call(
        _router_kernel,
        grid=(t // tm,),
        in_specs=[pl.BlockSpec((tm, d), lambda i: (i, 0)),
                  pl.BlockSpec((d, LANES), lambda i: (0, 0)),
                  pl.BlockSpec((1, LANES), lambda i: (0, 0)),
                  pl.BlockSpec((tm, tm), lambda i: (0, 0))],
        out_specs=[pl.BlockSpec((tm, LANES), lambda i: (i, 0)),
                   pl.BlockSpec((SUBLANES, LANES), lambda i: (0, 0))],
        out_shape=[jax.ShapeDtypeStruct((t, LANES), F32), jax.ShapeDtypeStruct((SUBLANES, LANES), F32)],
        scratch_shapes=[pltpu.VMEM((SUBLANES, LANES), F32)],
        compiler_params=_params(("arbitrary",)),
        name="moe_router",
    )(h, w_r, b_r.reshape(1, LANES), _tril_ones(tm))


def _dispatch_plan(expert, rank, counts, n_blocks):
    bm = MOE_ROWS
    flat = expert.reshape(-1)
    n_assign = flat.shape[0]
    padded = (counts + bm - 1) // bm * bm
    pad_end = jnp.cumsum(padded)
    dest = (pad_end - padded)[flat] + rank.reshape(-1)
    row_token = jnp.zeros((n_blocks * bm,), jnp.int32).at[dest].set(
        jnp.arange(n_assign, dtype=jnp.int32) // MOE_TOP_K)
    block_start = jnp.arange(n_blocks, dtype=jnp.int32) * bm
    block_expert = jnp.minimum(jnp.sum((pad_end[None, :] <= block_start[:, None]).astype(jnp.int32), axis=1),
                               MOE_EXPERTS - 1)
    n_used = (pad_end[-1] // bm).reshape(1)
    return dest.astype(jnp.int32), row_token, block_expert.astype(jnp.int32), n_used.astype(jnp.int32)


def _hier_moe(x, h, lp, gt, layer, ex_gate, ex_up, ex_down):
    t, d = x.shape
    w_r = jnp.zeros((d, LANES), F32).at[:, :MOE_GROUPS].set(lp['moe_gw'])
    w_r = w_r.at[:, MOE_GROUPS:MOE_GROUPS + MOE_EXPERTS].set(lp['moe_ew'])
    b_r = jnp.zeros((LANES,), F32).at[:MOE_GROUPS].set(lp['moe_gb'])
    b_r = b_r.at[MOE_GROUPS:MOE_GROUPS + MOE_EXPERTS].set(lp['moe_eb'])
    route, count_rows = _router(h, w_r.astype(BF16), b_r)
    weights = route[:, 0:2]
    expert = route[:, 2:4].astype(jnp.int32)
    rank = route[:, 4:6].astype(jnp.int32)
    counts = count_rows[0, :MOE_EXPERTS].astype(jnp.int32)
    n_blocks = (t * MOE_TOP_K) // MOE_ROWS + MOE_EXPERTS
    dest, row_token, block_expert, n_used = _dispatch_plan(expert, rank, counts, n_blocks)
    xs = _moe_gather(h, row_token, n_used, n_blocks)
    hmid = _moe_up(xs, ex_gate, ex_up, layer, block_expert, n_used, n_blocks)
    ys = _moe_down(hmid, ex_down, layer, block_expert, n_used, n_blocks)
    return _moe_combine(x, ys, dest, weights, gt)


def kernel(x, c, ada_w, ada_b, norm_mix, norm_ffn, w_in, rw_mu, rw_w0, rw_w2, rw_a0, rw_a2, rw_g2, rw_kk, rw_ka, rw_rk, rw_lnw, rw_lnb, rw_v0, rw_v1, rw_v2, ml_conv_w, ml_conv_b, ml_ib, ml_fb, ml_norm, hg_lb, hg_norm, p_a, p_b, p_c, w_out, moe_gw, moe_gb, moe_ew, moe_eb, ex_gate, ex_up, ex_down, final_norm):
    batch, t, d = x.shape
    assert batch == 1 and d == D_MODEL and t % 512 == 0
    depth = ada_w.shape[0]
    x = x.reshape(t, d)
    lbs = jnp.cumsum(jax.nn.softmax(hg_lb.astype(F32), axis=0), axis=0)
    lbs = lbs - lbs[:1]
    mod = _adaln_mod(c, ada_w, ada_b)
    zeros_d = jnp.zeros((d,), F32)
    c_ml = RW_COLS
    c_mg = c_ml + 2 * ML_QK_WIDTH + 2 * ML_WIDTH
    c_hg = c_mg + 2 * ML_HEADS
    c_gt = c_hg + 4 * HG_WIDTH
    v_first = None
    for l in range(depth):
        sh_m, sc_m, gt_m, sh_f, sc_f, gt_f = jnp.split(mod[l], 6)
        lp = dict(rw_mu=rw_mu[l], rw_w0=rw_w0[l], rw_w2=rw_w2[l], rw_a0=rw_a0[l], rw_a2=rw_a2[l], rw_g2=rw_g2[l],
                  rw_kk=rw_kk[l], rw_ka=rw_ka[l], rw_rk=rw_rk[l], rw_lnw=rw_lnw[l], rw_lnb=rw_lnb[l],
                  ml_conv_w=ml_conv_w[l], ml_conv_b=ml_conv_b[l], ml_ib=ml_ib[l], ml_fb=ml_fb[l],
                  ml_norm=ml_norm[l], hg_norm=hg_norm[l],
                  moe_gw=moe_gw[l], moe_gb=moe_gb[l], moe_ew=moe_ew[l], moe_eb=moe_eb[l])
        if l > 0:
            lp.update(rw_v0=rw_v0[l - 1], rw_v1=rw_v1[l - 1], rw_v2=rw_v2[l - 1])
        h = _norm(x, norm_mix[l], sc_m, sh_m, modulate=True, out_dtype=BF16)
        wl = w_in[l]
        u_rw = _mm(h, wl[:, :c_ml].astype(BF16), jnp.zeros((c_ml,), F32), out_dtype=F32, tn=c_ml)
        u_ml = _mm(h, wl[:, c_ml:c_mg].astype(BF16), jnp.zeros((c_mg - c_ml,), F32), out_dtype=F32, tn=ML_WIDTH)
        w_mg = jnp.zeros((d, LANES), F32).at[:, :2 * ML_HEADS].set(wl[:, c_mg:c_hg])
        u_mg = _mm(h, w_mg.astype(BF16), jnp.zeros((LANES,), F32), out_dtype=F32, tn=LANES)
        u_hg = _mm(h, wl[:, c_hg:c_gt].astype(BF16), jnp.zeros((c_gt - c_hg,), F32), out_dtype=F32, tn=4 * HG_WIDTH)
        y_a, v_first = _rwkv_branch(u_rw, v_first, lp)
        y_b = _mlstm_branch(u_ml, u_mg, lp)
        y_c = _hgrn_branch(u_hg, lbs[l], lp)
        x = _merge_out(h, y_a, y_b, y_c, wl[:, c_gt:].astype(BF16), p_a[l].astype(BF16), p_b[l].astype(BF16),
                       p_c[l].astype(BF16), w_out[l].astype(BF16), x, gt_m)
        h = _norm(x, norm_ffn[l], sc_f, sh_f, modulate=True, out_dtype=F32)
        x = _hier_moe(x, h, lp, gt_f, l, ex_gate, ex_up, ex_down)
    out = _norm(x, final_norm, zeros_d, zeros_d, modulate=False, out_dtype=F32)
    return out.reshape(batch, t, d)
```

```python
import functools

import numpy as np
import jax
import jax.numpy as jnp
from jax import lax
from jax.experimental import pallas as pl
from jax.experimental.pallas import tpu as pltpu

F32 = jnp.float32
BF16 = jnp.bfloat16

D_MODEL = 2048
DEPTH = 2
RW_HEADS, RW_HEAD_DIM, RW_WIDTH = 8, 64, 512
RW_LORA = 256
RW_COLS = 3 * RW_WIDTH + RW_LORA
RW_GN_EPS = 64e-5
RW_GROUP = 4
RW_CHUNKS_PER_STEP = 4
ML_HEADS, ML_QK_DIM, ML_V_DIM = 4, 128, 256
ML_QK_WIDTH, ML_WIDTH = 512, 1024
ML_CONV = 4
ML_SOFTCAP = 15.0
HG_HEADS, HG_DIM, HG_WIDTH = 4, 128, 512
CHUNK = 64
MOE_GROUPS, MOE_EPG, MOE_EXPERTS, MOE_TOP_K, MOE_D_FF = 4, 8, 32, 2, 1024
NORM_EPS = 1e-6

LANES = 128
SUBLANES = 8
MOE_ROWS = 256
DMA_PRIORITIES = 2
VMEM_LIMIT = 56 * 1024 * 1024


def _params(sem, limit=VMEM_LIMIT):
    return pltpu.CompilerParams(dimension_semantics=sem, vmem_limit_bytes=limit)


def _bf(x):
    return x.astype(BF16)


def _dot(a, b):
    return jnp.dot(_bf(a), _bf(b), preferred_element_type=F32)


def _dot_nt(a, b):
    return lax.dot_general(_bf(a), _bf(b), (((1,), (1,)), ((), ())), preferred_element_type=F32)


def _dot_tn(a, b):
    return lax.dot_general(_bf(a), _bf(b), (((0,), (0,)), ((), ())), preferred_element_type=F32)


def _split3(x):
    hi = x.astype(BF16)
    r1 = x - hi.astype(F32)
    mid = r1.astype(BF16)
    lo = (r1 - mid.astype(F32)).astype(BF16)
    return hi, mid, lo


def _sel_dot(sel, x):
    hi, mid, lo = _split3(x)
    return (jnp.dot(sel, hi, preferred_element_type=F32) + jnp.dot(sel, mid, preferred_element_type=F32)
            + jnp.dot(sel, lo, preferred_element_type=F32))


def _seg_sum(x, bd):
    hi, mid, lo = _split3(x)
    return (jnp.dot(hi, bd, preferred_element_type=F32) + jnp.dot(mid, bd, preferred_element_type=F32)
            + jnp.dot(lo, bd, preferred_element_type=F32))


def _sigmoid(x):
    return jax.nn.sigmoid(x)


def _silu(x):
    return x * jax.nn.sigmoid(x)


def _log_sigmoid(x):
    return jnp.minimum(x, 0.0) - jnp.log1p(jnp.exp(-jnp.abs(x)))


def _softplus(x):
    return jnp.maximum(x, 0.0) + jnp.log1p(jnp.exp(-jnp.abs(x)))


def _mod_kernel(c_ref, w_ref, b_ref, o_ref):
    cond = _silu(c_ref[...])
    o_ref[0] = _dot(cond, w_ref[0]) + b_ref[0]


def _adaln_mod(c, ada_w, ada_b):
    depth, d, n = ada_w.shape
    tn = 1024
    c8 = jnp.zeros((SUBLANES, d), F32).at[0].set(c[0])
    out = pl.pallas_call(
        _mod_kernel,
        grid=(depth, n // tn),
        in_specs=[pl.BlockSpec((SUBLANES, d), lambda l, j: (0, 0)),
                  pl.BlockSpec((1, d, tn), lambda l, j: (l, 0, j)),
                  pl.BlockSpec((1, 1, tn), lambda l, j: (l, 0, j))],
        out_specs=pl.BlockSpec((1, SUBLANES, tn), lambda l, j: (l, 0, j)),
        out_shape=jax.ShapeDtypeStruct((depth, SUBLANES, n), F32),
        compiler_params=_params(("arbitrary", "arbitrary")),
        name="adaln_mod",
    )(c8, ada_w, ada_b.reshape(depth, 1, n))
    return out[:, 0, :]


def _norm_kernel(x_ref, g_ref, sc_ref, sh_ref, o_ref, *, modulate):
    x = x_ref[...]
    y = x * lax.rsqrt(jnp.mean(x * x, axis=-1, keepdims=True) + NORM_EPS) * g_ref[...]
    if modulate:
        y = y * (1.0 + sc_ref[...]) + sh_ref[...]
    o_ref[...] = y.astype(o_ref.dtype)


def _norm(x, g, sc, sh, *, modulate, out_dtype):
    t, d = x.shape
    tm = min(512, t)
    row = pl.BlockSpec((1, d), lambda i: (0, 0))
    return pl.pallas_call(
        functools.partial(_norm_kernel, modulate=modulate),
        grid=(t // tm,),
        in_specs=[pl.BlockSpec((tm, d), lambda i: (i, 0)), row, row, row],
        out_specs=pl.BlockSpec((tm, d), lambda i: (i, 0)),
        out_shape=jax.ShapeDtypeStruct((t, d), out_dtype),
        compiler_params=_params(("arbitrary",)),
        name="rmsnorm_mod" if modulate else "rmsnorm",
    )(x, g.reshape(1, d), sc.reshape(1, d), sh.reshape(1, d))


def _mm_kernel(a_ref, w_ref, b_ref, o_ref):
    o_ref[...] = (_dot(a_ref[...], w_ref[...]) + b_ref[...]).astype(o_ref.dtype)


def _row_tile(m):
    return 1024 if m % 1024 == 0 else min(512, m)


def _mm(a, w, bias, *, out_dtype, tn):
    m, k = a.shape
    n = w.shape[1]
    tm = _row_tile(m)
    return pl.pallas_call(
        _mm_kernel,
        grid=(n // tn, m // tm),
        in_specs=[pl.BlockSpec((tm, k), lambda j, i: (i, 0)),
                  pl.BlockSpec((k, tn), lambda j, i: (0, j)),
                  pl.BlockSpec((1, tn), lambda j, i: (0, j))],
        out_specs=pl.BlockSpec((tm, tn), lambda j, i: (i, j)),
        out_shape=jax.ShapeDtypeStruct((m, n), out_dtype),
        compiler_params=_params(("arbitrary", "arbitrary")),
        name="dense_matmul",
    )(a, w, bias.reshape(1, n))


def _rwkv_kernel(*refs, has_vres):
    if has_vres:
        (u_ref, vf_ref, mu_ref, wl_ref, w0_ref, a0_ref, kk_ref, ka_ref, rk_ref, lnw_ref, lnb_ref,
         v0_ref, v1_ref, v2_ref, bd_ref, tril_ref, y_ref, xbuf, st_ref) = refs
    else:
        (u_ref, mu_ref, wl_ref, w0_ref, a0_ref, kk_ref, ka_ref, rk_ref, lnw_ref, lnb_ref,
         bd_ref, tril_ref, y_ref, vf_out_ref, xbuf, st_ref) = refs
    L = CHUNK
    W = RW_WIDTH
    TB = u_ref.shape[0]
    n_chunks = TB // L

    @pl.when(pl.program_id(0) == 0)
    def _():
        xbuf[0:SUBLANES, :] = jnp.zeros((SUBLANES, RW_COLS), F32)
        st_ref[...] = jnp.zeros(st_ref.shape, F32)

    u = u_ref[...].astype(F32)
    xbuf[SUBLANES:SUBLANES + TB, :] = u
    prev = xbuf[SUBLANES - 1:SUBLANES - 1 + TB, :]
    xbuf[0:SUBLANES, :] = u[TB - SUBLANES:TB, :]
    xs = u + mu_ref[...] * (prev - u)
    r = xs[:, 0:W]
    k = xs[:, W:2 * W]
    v = xs[:, 2 * W:3 * W]
    lr = xs[:, 3 * W:3 * W + RW_LORA]
    lane = lax.broadcasted_iota(jnp.int32, lr.shape, 1)
    act = jnp.where(lane < 64, jnp.tanh(lr), jnp.where(lane < 128, lr, _sigmoid(lr)))
    lo = _dot(act, wl_ref[...])
    z = w0_ref[...] + lo[:, 0:W]
    log_w = -_softplus(-z) - 0.5
    ld = -jnp.exp(log_w)
    iclr = _sigmoid(a0_ref[...] + lo[:, W:2 * W])
    gate = lo[:, 2 * W:3 * W]
    if has_vres:
        vv = _dot(_dot(v, v1_ref[...]), v2_ref[...])
        v = v + (vf_ref[...] - v) * _sigmoid(v0_ref[...] + vv)
    else:
        vf_out_ref[...] = v
    bd = bd_ref[...]
    kk = k * kk_ref[...]
    k = k * (1.0 + (iclr - 1.0) * ka_ref[...])
    sums = _seg_sum(jnp.concatenate([kk * kk, r * k * rk_ref[...]], axis=0), bd)
    kk = kk / jnp.maximum(jnp.sqrt(sums[0:TB, :]), 1e-12)
    a = -kk
    b = kk * iclr
    bonus = sums[TB:2 * TB, :] * v

    cum = _sel_dot(tril_ref[...], ld)
    cl_rows = [cum[(c + 1) * L - 1:(c + 1) * L, :] for c in range(n_chunks)]
    cl = jnp.concatenate([jnp.broadcast_to(x, (L, W)) for x in cl_rows], axis=0)
    e_neg = jnp.exp(-cum)
    e_last = jnp.exp(cl - cum)
    a_t = a * jnp.exp(cum - ld)
    r_t = r * jnp.exp(cum)
    b_t = b * e_neg
    k_t = k * e_neg
    b_h = b * e_last
    k_h = k * e_last

    G = RW_GROUP
    S = G * L
    row = lax.broadcasted_iota(jnp.int32, (S, S), 0)
    col = lax.broadcasted_iota(jnp.int32, (S, S), 1)
    same = (row // L) == (col // L)
    strict = same & (row > col)
    incl = same & (row >= col)
    eye = (row == col).astype(F32)
    lane_head = lax.broadcasted_iota(jnp.int32, (1, S), 1) // RW_HEAD_DIM
    head_masks = [(lane_head == j).astype(F32) for j in range(G)]

    n_groups = RW_HEADS // G
    units = [(c, g) for c in range(n_chunks) for g in range(n_groups)]

    def unit_rows(x, c, g):
        return x[c * L:(c + 1) * L, g * S:(g + 1) * S]

    def stack(x, c, g):
        xs_ = unit_rows(x, c, g)
        return jnp.concatenate([xs_ * m for m in head_masks], axis=0)

    def tile(x, c, g):
        return jnp.concatenate([unit_rows(x, c, g)] * G, axis=0)

    a_s = {u: stack(a_t, *u) for u in units}
    r_s = {u: stack(r_t, *u) for u in units}
    v_s = {u: stack(v, *u) for u in units}
    bk_h = {u: jnp.concatenate([stack(b_h, *u), stack(k_h, *u)], axis=0) for u in units}
    xm = {u: _dot_nt(jnp.concatenate([a_s[u], r_s[u]], axis=0),
                     jnp.concatenate([tile(b_t, *u), tile(k_t, *u)], axis=0)) for u in units}
    n_ab = {u: jnp.where(strict, xm[u][0:S, 0:S], 0.0) for u in units}
    n_ak = {u: jnp.where(strict, xm[u][0:S, S:2 * S], 0.0) for u in units}
    r_bk = {u: jnp.concatenate([jnp.where(incl, xm[u][S:2 * S, 0:S], 0.0),
                                jnp.where(incl, xm[u][S:2 * S, S:2 * S], 0.0)], axis=1) for u in units}
    tinv = {u: eye + n_ab[u] for u in units}
    pw = dict(n_ab)
    for _ in range(5):
        pw = {u: _dot(pw[u], pw[u]) for u in units}
        tinv = {u: tinv[u] + _dot(tinv[u], pw[u]) for u in units}
    nv = {u: _dot(n_ak[u], v_s[u]) for u in units}
    pq = {u: _dot(tinv[u], jnp.concatenate([a_s[u], nv[u]], axis=1)) for u in units}
    p_s = {u: pq[u][:, 0:S] for u in units}
    qv = {u: jnp.concatenate([pq[u][:, S:2 * S], v_s[u]], axis=0) for u in units}
    y1 = {u: r_s[u] + _dot(r_bk[u][:, 0:S], p_s[u]) for u in units}
    y0 = {u: _dot(r_bk[u], qv[u]) for u in units}
    gt = {u: _dot_tn(p_s[u], bk_h[u][0:S, :]) for u in units}
    ht = {u: _dot_tn(qv[u], bk_h[u]) for u in units}

    state = [st_ref[g] for g in range(n_groups)]
    y_rows = []
    for c in range(n_chunks):
        w_last = jnp.exp(cl_rows[c])
        ys = []
        for g in range(n_groups):
            u = (c, g)
            st = state[g]
            y_s = _dot_nt(y1[u], st) + y0[u]
            state[g] = st * w_last[:, g * S:(g + 1) * S] + _dot(st, gt[u]) + ht[u]
            y_g = y_s[0:L, :]
            for j in range(1, G):
                y_g = y_g + y_s[j * L:(j + 1) * L, :]
            ys.append(y_g)
        y_rows.append(jnp.concatenate(ys, axis=1))
    for g in range(n_groups):
        st_ref[g] = state[g]
    y = jnp.concatenate(y_rows, axis=0)

    inv_n = 1.0 / RW_HEAD_DIM
    mean = _seg_sum(y, bd) * inv_n
    dlt = y - mean
    var = _seg_sum(dlt * dlt, bd) * inv_n
    yn = dlt * lax.rsqrt(var + RW_GN_EPS) * lnw_ref[...] + lnb_ref[...]
    y_ref[...] = ((yn + bonus) * gate).astype(y_ref.dtype)


def _blockdiag_ones(width, head):
    idx = np.arange(width) // head
    return jnp.asarray((idx[:, None] == idx[None, :]).astype(np.float32), BF16)


def _tril_ones(n, blocks=1):
    return jnp.asarray(np.kron(np.eye(blocks, dtype=np.float32), np.tril(np.ones((n, n), np.float32))), BF16)


def _rwkv_branch(u_rw, v_first, lp):
    t = u_rw.shape[0]
    L = CHUNK
    has_vres = v_first is not None
    W = RW_WIDTH
    vec = lambda a: a.reshape(1, -1).astype(F32)
    full = lambda shape: pl.BlockSpec(shape, lambda i: (0,) * len(shape))
    tb = RW_CHUNKS_PER_STEP * L
    rows = lambda width: pl.BlockSpec((tb, width), lambda i: (i, 0))
    wl = jnp.zeros((RW_LORA, 3 * W), F32)
    wl = wl.at[0:64, 0:W].set(lp['rw_w2']).at[64:128, W:2 * W].set(lp['rw_a2']).at[128:256, 2 * W:].set(lp['rw_g2'])
    ins = [u_rw]
    specs = [rows(RW_COLS)]
    if has_vres:
        ins.append(v_first)
        specs.append(rows(W))
    ins += [vec(lp['rw_mu']), wl.astype(BF16), vec(lp['rw_w0']), vec(lp['rw_a0']), vec(lp['rw_kk']),
            vec(lp['rw_ka']), vec(lp['rw_rk']), vec(lp['rw_lnw']), vec(lp['rw_lnb'])]
    specs += [full((1, RW_COLS)), full((RW_LORA, 3 * W))] + [full((1, W))] * 7
    if has_vres:
        v1 = jnp.zeros((W, LANES), F32).at[:, :lp['rw_v1'].shape[1]].set(lp['rw_v1'])
        v2 = jnp.zeros((LANES, W), F32).at[:lp['rw_v2'].shape[0], :].set(lp['rw_v2'])
        ins += [vec(lp['rw_v0']), v1.astype(BF16), v2.astype(BF16)]
        specs += [full((1, W)), full((W, LANES)), full((LANES, W))]
    ins += [_blockdiag_ones(W, RW_HEAD_DIM), _tril_ones(L, RW_CHUNKS_PER_STEP)]
    specs += [full((W, W)), full((tb, tb))]
    out_shape = [jax.ShapeDtypeStruct((t, W), BF16)]
    out_specs = [rows(W)]
    if not has_vres:
        out_shape.append(jax.ShapeDtypeStruct((t, W), F32))
        out_specs.append(rows(W))
    outs = pl.pallas_call(
        functools.partial(_rwkv_kernel, has_vres=has_vres),
        grid=(t // tb,),
        in_specs=specs,
        out_specs=out_specs,
        out_shape=out_shape,
        scratch_shapes=[pltpu.VMEM((tb + SUBLANES, RW_COLS), F32),
                        pltpu.VMEM((RW_HEADS // RW_GROUP, RW_GROUP * RW_HEAD_DIM, RW_GROUP * RW_HEAD_DIM), F32)],
        compiler_params=_params(("arbitrary",)),
        name="rwkv7_branch",
    )(*ins)
    if has_vres:
        return outs[0], v_first
    return outs[0], outs[1]


def _mlstm_kernel(qk_ref, v_ref, o_ref, g_ref, cw_ref, cb_ref, gb_ref, ng_ref, tril_ref,
                  y_ref, xbuf, c_ref, m_ref):
    L = CHUNK

    @pl.when(pl.program_id(0) == 0)
    def _():
        xbuf[0:SUBLANES, :] = jnp.zeros((SUBLANES, 2 * ML_QK_WIDTH), F32)
        c_ref[...] = jnp.zeros(c_ref.shape, F32)
        m_ref[...] = jnp.zeros(m_ref.shape, F32)

    x0 = qk_ref[...].astype(F32)
    xbuf[SUBLANES:SUBLANES + L, :] = x0
    conv = cb_ref[...] + cw_ref[ML_CONV - 1:ML_CONV, :] * x0
    for dly in range(1, ML_CONV):
        conv = conv + cw_ref[ML_CONV - 1 - dly:ML_CONV - dly, :] * xbuf[SUBLANES - dly:SUBLANES - dly + L, :]
    xbuf[0:SUBLANES, :] = x0[L - SUBLANES:L, :]
    qk = _silu(conv)
    q = qk[:, :ML_QK_WIDTH]
    k = qk[:, ML_QK_WIDTH:] * (ML_QK_DIM ** -0.5)

    pre = g_ref[...] + gb_ref[...]
    cap = ML_SOFTCAP * jnp.tanh(pre / ML_SOFTCAP)
    lane = lax.broadcasted_iota(jnp.int32, pre.shape, 1)
    gates = jnp.where(lane < ML_HEADS, cap, _log_sigmoid(cap))
    csum = _sel_dot(tril_ref[...], gates)
    gates_t = gates.T
    csum_t = csum.T

    row = lax.broadcasted_iota(jnp.int32, (L, L), 0)
    col = lax.broadcasted_iota(jnp.int32, (L, L), 1)
    causal = row >= col
    ones_col = (lax.broadcasted_iota(jnp.int32, (L, LANES), 1) == 0).astype(F32)
    vv = v_ref[...].astype(F32)
    og = o_ref[...].astype(F32)
    outs = []
    for h in range(ML_HEADS):
        qh = q[:, h * ML_QK_DIM:(h + 1) * ML_QK_DIM]
        kh = k[:, h * ML_QK_DIM:(h + 1) * ML_QK_DIM]
        vh = jnp.concatenate([vv[:, h * ML_V_DIM:(h + 1) * ML_V_DIM], ones_col], axis=1)
        b_col = csum[:, ML_HEADS + h:ML_HEADS + h + 1]
        b_row = csum_t[ML_HEADS + h:ML_HEADS + h + 1, :]
        li_col = gates[:, h:h + 1]
        li_row = gates_t[h:h + 1, :]
        m_prev = m_ref[h][0:1, 0:1]
        dmat = jnp.where(causal, b_col - b_row + li_row, -jnp.inf)
        m_inter = b_col + m_prev
        m_t = jnp.maximum(m_inter, jnp.max(dmat, axis=-1, keepdims=True))
        s = _dot_nt(qh, kh) * jnp.exp(dmat - m_t)
        inter = jnp.exp(m_inter - m_t)
        c_aug = c_ref[h]
        num_aug = _dot(s, vh) + inter * _dot(qh, c_aug)
        num = num_aug[:, :ML_V_DIM]
        den = num_aug[:, ML_V_DIM:ML_V_DIM + 1]
        hh = num / jnp.maximum(jnp.abs(den), jnp.exp(-m_t))
        b_last = b_col[L - 1:L, :]
        g = b_last - b_col + li_col
        m_new = jnp.maximum(b_last + m_prev, jnp.max(g, axis=0, keepdims=True))
        wgt = jnp.exp(g - m_new)
        keep = jnp.exp(b_last + m_prev - m_new)
        c_ref[h] = keep * c_aug + _dot_tn(wgt * kh, vh)
        m_ref[h] = jnp.broadcast_to(m_new, (SUBLANES, LANES))
        ng = ng_ref[:, h * ML_V_DIM:(h + 1) * ML_V_DIM]
        hn = hh * lax.rsqrt(jnp.mean(hh * hh, axis=-1, keepdims=True) + NORM_EPS) * ng
        outs.append(hn * _sigmoid(og[:, h * ML_V_DIM:(h + 1) * ML_V_DIM]))
    y_ref[...] = jnp.concatenate(outs, axis=1).astype(y_ref.dtype)


def _mlstm_branch(u_ml, u_mg, lp):
    t = u_ml.shape[0]
    L = CHUNK
    full = lambda shape: pl.BlockSpec(shape, lambda i: (0,) * len(shape))
    gb = jnp.zeros((1, LANES), F32).at[0, 0:ML_HEADS].set(lp['ml_ib']).at[0, ML_HEADS:2 * ML_HEADS].set(lp['ml_fb'])
    return pl.pallas_call(
        _mlstm_kernel,
        grid=(t // L,),
        in_specs=[pl.BlockSpec((L, ML_WIDTH), lambda i: (i, 0)),
                  pl.BlockSpec((L, ML_WIDTH), lambda i: (i, 1)),
                  pl.BlockSpec((L, ML_WIDTH), lambda i: (i, 2)),
                  pl.BlockSpec((L, LANES), lambda i: (i, 0)),
                  full((ML_CONV, 2 * ML_QK_WIDTH)), full((1, 2 * ML_QK_WIDTH)), full((1, LANES)),
                  full((1, ML_WIDTH)), full((L, L))],
        out_specs=pl.BlockSpec((L, ML_WIDTH), lambda i: (i, 0)),
        out_shape=jax.ShapeDtypeStruct((t, ML_WIDTH), BF16),
        scratch_shapes=[pltpu.VMEM((L + SUBLANES, 2 * ML_QK_WIDTH), F32),
                        pltpu.VMEM((ML_HEADS, ML_QK_DIM, ML_V_DIM + LANES), F32),
                        pltpu.VMEM((ML_HEADS, SUBLANES, LANES), F32)],
        compiler_params=_params(("arbitrary",)),
        name="mlstm_branch",
    )(u_ml, u_ml, u_ml, u_mg, lp['ml_conv_w'], lp['ml_conv_b'].reshape(1, -1), gb,
      lp['ml_norm'].reshape(1, -1), _tril_ones(L))


def _hgrn_levels():
    L = CHUNK
    t = np.arange(L)
    sel = [np.tril(np.ones((L, L), np.float32))]
    masks = []
    size = L
    while size >= 2:
        half = size // 2
        ref_row = (t // size) * size + half - 1
        sel.append((t[None, :] <= ref_row[:, None]).astype(np.float32))
        same = (t[:, None] // size) == (t[None, :] // size)
        masks.append((same & ((t[:, None] % size) >= half) & ((t[None, :] % size) < half)).astype(np.float32))
        size = half
    return np.concatenate(sel, axis=0), np.stack(masks)


def _hgrn_kernel(u_ref, la_ref, lc_ref, lb_ref, ng_ref, sel_ref, mask_ref, y_ref, s_ref, *, levels):
    L = CHUNK
    W = HG_WIDTH

    @pl.when(pl.program_id(0) == 0)
    def _():
        s_ref[...] = jnp.zeros(s_ref.shape, F32)

    u = u_ref[...].astype(F32)
    q = _silu(u[:, 0:W])
    f_pre = u[:, W:2 * W]
    ii = u[:, 2 * W:3 * W]
    g_pre = u[:, 3 * W:4 * W]
    la = la_ref[...]
    lc = lc_ref[...] + _log_sigmoid(f_pre)
    log_f = jnp.maximum(la, lc) + jnp.log1p(jnp.exp(-jnp.abs(la - lc)))
    k = (1.0 - lb_ref[...]) * _sigmoid(-f_pre)

    cr = _sel_dot(sel_ref[...], log_f)
    cum = cr[0:L, :]
    cl = cum[L - 1:L, :]
    q_in = q * jnp.exp(cum)
    k_out = k * jnp.exp(cl - cum)
    w_last = jnp.exp(cl)
    qe = []
    ke = []
    for lv in range(levels):
        ref = cr[(lv + 1) * L:(lv + 2) * L, :]
        qe.append(q * jnp.exp(jnp.minimum(cum - ref, 0.0)))
        ke.append(k * jnp.exp(jnp.minimum(ref - cum, 0.0)))
    outs = []
    for h in range(HG_HEADS):
        sl = slice(h * HG_DIM, (h + 1) * HG_DIM)
        att = jnp.zeros((L, L), F32)
        for lv in range(levels):
            att = att + mask_ref[lv] * _dot_nt(qe[lv][:, sl], ke[lv][:, sl])
        ih = ii[:, sl]
        diag = jnp.sum(q[:, sl] * k[:, sl], axis=-1, keepdims=True)
        st = s_ref[h]
        o = _dot(att, ih) + diag * ih + _dot_nt(q_in[:, sl], st)
        s_ref[h] = st * w_last[:, sl] + _dot_tn(ih, k_out[:, sl])
        on = o * lax.rsqrt(jnp.mean(o * o, axis=-1, keepdims=True) + NORM_EPS) * ng_ref[:, sl]
        outs.append(on * _silu(g_pre[:, sl]))
    y_ref[...] = jnp.concatenate(outs, axis=1).astype(y_ref.dtype)


def _hgrn_branch(u_hg, lb, lp):
    t = u_hg.shape[0]
    L = CHUNK
    sel, masks = _hgrn_levels()
    levels = masks.shape[0]
    full = lambda shape: pl.BlockSpec(shape, lambda i: (0,) * len(shape))
    lb = lb.reshape(1, -1).astype(F32)
    return pl.pallas_call(
        functools.partial(_hgrn_kernel, levels=levels),
        grid=(t // L,),
        in_specs=[pl.BlockSpec((L, 4 * HG_WIDTH), lambda i: (i, 0)),
                  full((1, HG_WIDTH)), full((1, HG_WIDTH)), full((1, HG_WIDTH)), full((1, HG_WIDTH)),
                  full(sel.shape), full(masks.shape)],
        out_specs=pl.BlockSpec((L, HG_WIDTH), lambda i: (i, 0)),
        out_shape=jax.ShapeDtypeStruct((t, HG_WIDTH), BF16),
        scratch_shapes=[pltpu.VMEM((HG_HEADS, HG_DIM, HG_DIM), F32)],
        compiler_params=_params(("arbitrary",)),
        name="hgrn2_branch",
    )(u_hg, jnp.log(lb), jnp.log1p(-lb), lb, lp['hg_norm'].reshape(1, -1),
      jnp.asarray(sel, BF16), jnp.asarray(masks, F32))


def _gate_merge_kernel(h_ref, ya_ref, yb_ref, yc_ref, wga_ref, wgb_ref, wgc_ref, pa_ref, pb_ref, pc_ref, o_ref):
    h = h_ref[...]
    merged = (_sigmoid(_dot(h, wga_ref[...])) * _dot(ya_ref[...], pa_ref[...])
              + _sigmoid(_dot(h, wgb_ref[...])) * _dot(yb_ref[...], pb_ref[...])
              + _sigmoid(_dot(h, wgc_ref[...])) * _dot(yc_ref[...], pc_ref[...]))
    o_ref[...] = merged.astype(o_ref.dtype)


def _residual_proj_kernel(a_ref, w_ref, x_ref, gt_ref, o_ref):
    o_ref[...] = x_ref[...] + gt_ref[...] * _dot(a_ref[...], w_ref[...])


def _merge_out(h, ya, yb, yc, w_gates, pa, pb, pc, wo, x, gt):
    t, d = x.shape
    tm = _row_tile(t)
    tn = 512
    nb = d // tn
    merged = pl.pallas_call(
        _gate_merge_kernel,
        grid=(nb, t // tm),
        in_specs=[pl.BlockSpec((tm, d), lambda n, i: (i, 0)),
                  pl.BlockSpec((tm, RW_WIDTH), lambda n, i: (i, 0)),
                  pl.BlockSpec((tm, ML_WIDTH), lambda n, i: (i, 0)),
                  pl.BlockSpec((tm, HG_WIDTH), lambda n, i: (i, 0)),
                  pl.BlockSpec((d, tn), lambda n, i: (0, n)),
                  pl.BlockSpec((d, tn), lambda n, i: (0, nb + n)),
                  pl.BlockSpec((d, tn), lambda n, i: (0, 2 * nb + n)),
                  pl.BlockSpec((RW_WIDTH, tn), lambda n, i: (0, n)),
                  pl.BlockSpec((ML_WIDTH, tn), lambda n, i: (0, n)),
                  pl.BlockSpec((HG_WIDTH, tn), lambda n, i: (0, n))],
        out_specs=pl.BlockSpec((tm, tn), lambda n, i: (i, n)),
        out_shape=jax.ShapeDtypeStruct((t, d), BF16),
        compiler_params=_params(("arbitrary", "arbitrary")),
        name="gate_merge",
    )(h, ya, yb, yc, w_gates, w_gates, w_gates, pa, pb, pc)
    tn2 = 1024
    return pl.pallas_call(
        _residual_proj_kernel,
        grid=(d // tn2, t // tm),
        in_specs=[pl.BlockSpec((tm, d), lambda n, i: (i, 0)),
                  pl.BlockSpec((d, tn2), lambda n, i: (0, n)),
                  pl.BlockSpec((tm, tn2), lambda n, i: (i, n)),
                  pl.BlockSpec((1, tn2), lambda n, i: (0, n))],
        out_specs=pl.BlockSpec((tm, tn2), lambda n, i: (i, n)),
        out_shape=jax.ShapeDtypeStruct((t, d), F32),
        compiler_params=_params(("arbitrary", "arbitrary")),
        name="residual_proj",
    )(merged, wo, x, gt.reshape(1, d))


def _gather_kernel(nu_ref, tok_ref, src_ref, o_ref, sem):
    i = pl.program_id(0)
    rows = o_ref.shape[0]

    def row_copy(r, src_row):
        return pltpu.make_async_copy(src_ref.at[pl.ds(src_row, 1)], o_ref.at[pl.ds(r, 1)], sem)

    @pl.when(i < nu_ref[0])
    def _():
        def start(q, c):
            for prio in range(DMA_PRIORITIES):
                r = DMA_PRIORITIES * q + prio
                row_copy(r, tok_ref[0, 0, r]).start(priority=prio)
            return c

        def wait(r, c):
            row_copy(r, 0).wait()
            return c

        lax.fori_loop(0, rows // DMA_PRIORITIES, start, 0, unroll=4)
        lax.fori_loop(0, rows, wait, 0, unroll=8)

    @pl.when(i >= nu_ref[0])
    def _():
        o_ref[...] = jnp.zeros(o_ref.shape, o_ref.dtype)


def _moe_gather(h, row_token, n_used, n_blocks):
    d = h.shape[1]
    bm = MOE_ROWS
    return pl.pallas_call(
        _gather_kernel,
        grid_spec=pltpu.PrefetchScalarGridSpec(
            num_scalar_prefetch=1,
            grid=(n_blocks,),
            in_specs=[pl.BlockSpec((1, 1, bm), lambda i, nu: (i, 0, 0), memory_space=pltpu.SMEM),
                      pl.BlockSpec(memory_space=pl.ANY)],
            out_specs=pl.BlockSpec((bm, d), lambda i, nu: (i, 0)),
            scratch_shapes=[pltpu.SemaphoreType.DMA(())]),
        out_shape=jax.ShapeDtypeStruct((n_blocks * bm, d), h.dtype),
        compiler_params=_params(("arbitrary",)),
        name="moe_gather",
    )(n_used, row_token.reshape(n_blocks, 1, bm), h)


def _moe_up_kernel(be_ref, nu_ref, x_ref, wg_ref, wu_ref, o_ref, wg_bf, wu_bf):
    i = pl.program_id(0)
    prev = be_ref[jnp.maximum(i - 1, 0)]

    @pl.when((i == 0) | (be_ref[i] != prev))
    def _():
        wg_bf[...] = wg_ref[0, 0].astype(BF16)
        wu_bf[...] = wu_ref[0, 0].astype(BF16)

    @pl.when(i < nu_ref[0])
    def _():
        x = _bf(x_ref[...])
        g = jnp.dot(x, wg_bf[...], preferred_element_type=F32)
        u = jnp.dot(x, wu_bf[...], preferred_element_type=F32)
        o_ref[...] = (_silu(g) * u).astype(o_ref.dtype)

    @pl.when(i >= nu_ref[0])
    def _():
        o_ref[...] = jnp.zeros(o_ref.shape, o_ref.dtype)


def _moe_up(xs, w_gate, w_up, layer, block_expert, n_used, n_blocks):
    d = xs.shape[1]
    ff = w_gate.shape[3]
    bm = MOE_ROWS
    return pl.pallas_call(
        _moe_up_kernel,
        grid_spec=pltpu.PrefetchScalarGridSpec(
            num_scalar_prefetch=2,
            grid=(n_blocks,),
            in_specs=[pl.BlockSpec((bm, d), lambda i, be, nu: (i, 0)),
                      pl.BlockSpec((1, 1, d, ff), lambda i, be, nu: (layer, be[i], 0, 0)),
                      pl.BlockSpec((1, 1, d, ff), lambda i, be, nu: (layer, be[i], 0, 0))],
            out_specs=pl.BlockSpec((bm, ff), lambda i, be, nu: (i, 0)),
            scratch_shapes=[pltpu.VMEM((d, ff), BF16), pltpu.VMEM((d, ff), BF16)]),
        out_shape=jax.ShapeDtypeStruct((n_blocks * bm, ff), BF16),
        compiler_params=_params(("arbitrary",)),
        name="moe_up",
    )(block_expert, n_used, xs, w_gate, w_up)


def _moe_down_kernel(be_ref, nu_ref, x_ref, wd_ref, o_ref, wd_bf):
    i = pl.program_id(0)
    prev = be_ref[jnp.maximum(i - 1, 0)]

    @pl.when((i == 0) | (be_ref[i] != prev))
    def _():
        wd_bf[...] = wd_ref[0, 0].astype(BF16)

    @pl.when(i < nu_ref[0])
    def _():
        o_ref[...] = jnp.dot(x_ref[...], wd_bf[...], preferred_element_type=F32).astype(o_ref.dtype)

    @pl.when(i >= nu_ref[0])
    def _():
        o_ref[...] = jnp.zeros(o_ref.shape, o_ref.dtype)


def _moe_down(hmid, w_down, layer, block_expert, n_used, n_blocks):
    ff = hmid.shape[1]
    d = w_down.shape[3]
    bm = MOE_ROWS
    return pl.pallas_call(
        _moe_down_kernel,
        grid_spec=pltpu.PrefetchScalarGridSpec(
            num_scalar_prefetch=2,
            grid=(n_blocks,),
            in_specs=[pl.BlockSpec((bm, ff), lambda i, be, nu: (i, 0)),
                      pl.BlockSpec((1, 1, ff, d), lambda i, be, nu: (layer, be[i], 0, 0))],
            out_specs=pl.BlockSpec((bm, d), lambda i, be, nu: (i, 0)),
            scratch_shapes=[pltpu.VMEM((ff, d), BF16)]),
        out_shape=jax.ShapeDtypeStruct((n_blocks * bm, d), F32),
        compiler_params=_params(("arbitrary",)),
        name="moe_down",
    )(block_expert, n_used, hmid, w_down)


def _combine_kernel(dest_ref, w_ref, x_ref, gt_ref, ys_ref, o_ref, buf, sem):
    tc = x_ref.shape[0]

    def row_copy(tok, choice, src_row):
        return pltpu.make_async_copy(ys_ref.at[pl.ds(src_row, 1)], buf.at[choice, pl.ds(tok, 1)], sem)

    def start(tok, c):
        for choice in range(MOE_TOP_K):
            row_copy(tok, choice, dest_ref[0, 0, MOE_TOP_K * tok + choice]).start(priority=choice)
        return c

    def wait(tok, c):
        for choice in range(MOE_TOP_K):
            row_copy(tok, choice, 0).wait()
        return c

    lax.fori_loop(0, tc, start, 0, unroll=4)
    lax.fori_loop(0, tc, wait, 0, unroll=4)
    w = w_ref[...]
    y = w[:, 0:1] * buf[0] + w[:, 1:2] * buf[1]
    o_ref[...] = x_ref[...] + gt_ref[...] * y


def _moe_combine(x, ys, dest, weights, gt):
    t, d = x.shape
    tc = min(256, t)
    return pl.pallas_call(
        _combine_kernel,
        grid=(t // tc,),
        in_specs=[pl.BlockSpec((1, 1, MOE_TOP_K * tc), lambda i: (i, 0, 0), memory_space=pltpu.SMEM),
                  pl.BlockSpec((tc, MOE_TOP_K), lambda i: (i, 0)),
                  pl.BlockSpec((tc, d), lambda i: (i, 0)),
                  pl.BlockSpec((1, d), lambda i: (0, 0)),
                  pl.BlockSpec(memory_space=pl.ANY)],
        out_specs=pl.BlockSpec((tc, d), lambda i: (i, 0)),
        out_shape=jax.ShapeDtypeStruct((t, d), F32),
        scratch_shapes=[pltpu.VMEM((MOE_TOP_K, tc, d), F32), pltpu.SemaphoreType.DMA(())],
        compiler_params=_params(("arbitrary",)),
        name="moe_combine",
    )(dest.reshape(t // tc, 1, MOE_TOP_K * tc), weights, x, gt.reshape(1, d), ys)


def _router_kernel(h_ref, w_ref, b_ref, tril_ref, route_ref, count_ref, run_ref):
    @pl.when(pl.program_id(0) == 0)
    def _():
        run_ref[...] = jnp.zeros(run_ref.shape, F32)

    logits = _dot(h_ref[...], w_ref[...]) + b_ref[...]
    lane = lax.broadcasted_iota(jnp.int32, logits.shape, 1)
    neg = -jnp.inf
    far = jnp.int32(LANES)

    def first_max(vals):
        top = jnp.max(vals, axis=-1, keepdims=True)
        return top, jnp.min(jnp.where(vals == top, lane, far), axis=-1, keepdims=True)

    is_group = lane < MOE_GROUPS
    g_top, g_idx = first_max(jnp.where(is_group, logits, neg))
    p_group = 1.0 / jnp.sum(jnp.where(is_group, jnp.exp(logits - g_top), 0.0), axis=-1, keepdims=True)
    first = MOE_GROUPS + MOE_EPG * g_idx
    cand = jnp.where((lane >= first) & (lane < first + MOE_EPG), logits, neg)
    e1, i1 = first_max(cand)
    e2, i2 = first_max(jnp.where(lane == i1, neg, cand))
    z = jnp.exp(e2 - e1)
    w1 = p_group / (1.0 + z)
    w2 = p_group * z / (1.0 + z)
    x1 = i1 - MOE_GROUPS
    x2 = i2 - MOE_GROUPS
    hit1 = lane == x1
    hit2 = lane == x2
    onehot = (hit1 | hit2).astype(F32)
    cum = jnp.dot(tril_ref[...], _bf(onehot), preferred_element_type=F32) + run_ref[0:1, :]
    rank1 = jnp.sum(jnp.where(hit1, cum, 0.0), axis=-1, keepdims=True) - 1.0
    rank2 = jnp.sum(jnp.where(hit2, cum, 0.0), axis=-1, keepdims=True) - 1.0
    last = cum[cum.shape[0] - 1:cum.shape[0], :]
    run_ref[...] = jnp.broadcast_to(last, run_ref.shape)
    count_ref[...] = jnp.broadcast_to(last, count_ref.shape)
    cols = (w1, w2, x1.astype(F32), x2.astype(F32), rank1, rank2)
    packed = jnp.zeros(logits.shape, F32)
    for j, cvals in enumerate(cols):
        packed = jnp.where(lane == j, cvals, packed)
    route_ref[...] = packed


def _router(h, w_r, b_r):
    t, d = h.shape
    tm = MOE_ROWS
    return pl.pallas_call(
        _router_kernel,
        grid=(t // tm,),
        in_specs=[pl.BlockSpec((tm, d), lambda i: (i, 0)),
                  pl.BlockSpec((d, LANES), lambda i: (0, 0)),
                  pl.BlockSpec((1, LANES), lambda i: (0, 0)),
                  pl.BlockSpec((tm, tm), lambda i: (0, 0))],
        out_specs=[pl.BlockSpec((tm, LANES), lambda i: (i, 0)),
                   pl.BlockSpec((SUBLANES, LANES), lambda i: (0, 0))],
        out_shape=[jax.ShapeDtypeStruct((t, LANES), F32), jax.ShapeDtypeStruct((SUBLANES, LANES), F32)],
        scratch_shapes=[pltpu.VMEM((SUBLANES, LANES), F32)],
        compiler_params=_params(("arbitrary",)),
        name="moe_router",
    )(h, w_r, b_r.reshape(1, LANES), _tril_ones(tm))


def _dispatch_plan(expert, rank, counts, n_blocks):
    bm = MOE_ROWS
    flat = expert.reshape(-1)
    n_assign = flat.shape[0]
    padded = (counts + bm - 1) // bm * bm
    pad_end = jnp.cumsum(padded)
    dest = (pad_end - padded)[flat] + rank.reshape(-1)
    row_token = jnp.zeros((n_blocks * bm,), jnp.int32).at[dest].set(
        jnp.arange(n_assign, dtype=jnp.int32) // MOE_TOP_K)
    block_start = jnp.arange(n_blocks, dtype=jnp.int32) * bm
    block_expert = jnp.minimum(jnp.sum((pad_end[None, :] <= block_start[:, None]).astype(jnp.int32), axis=1),
                               MOE_EXPERTS - 1)
    n_used = (pad_end[-1] // bm).reshape(1)
    return dest.astype(jnp.int32), row_token, block_expert.astype(jnp.int32), n_used.astype(jnp.int32)


def _hier_moe(x, h, lp, gt, layer, ex_gate, ex_up, ex_down):
    t, d = x.shape
    w_r = jnp.zeros((d, LANES), F32).at[:, :MOE_GROUPS].set(lp['moe_gw'])
    w_r = w_r.at[:, MOE_GROUPS:MOE_GROUPS + MOE_EXPERTS].set(lp['moe_ew'])
    b_r = jnp.zeros((LANES,), F32).at[:MOE_GROUPS].set(lp['moe_gb'])
    b_r = b_r.at[MOE_GROUPS:MOE_GROUPS + MOE_EXPERTS].set(lp['moe_eb'])
    route, count_rows = _router(h, w_r.astype(BF16), b_r)
    weights = route[:, 0:2]
    expert = route[:, 2:4].astype(jnp.int32)
    rank = route[:, 4:6].astype(jnp.int32)
    counts = count_rows[0, :MOE_EXPERTS].astype(jnp.int32)
    n_blocks = (t * MOE_TOP_K) // MOE_ROWS + MOE_EXPERTS
    dest, row_token, block_expert, n_used = _dispatch_plan(expert, rank, counts, n_blocks)
    xs = _moe_gather(h, row_token, n_used, n_blocks)
    hmid = _moe_up(xs, ex_gate, ex_up, layer, block_expert, n_used, n_blocks)
    ys = _moe_down(hmid, ex_down, layer, block_expert, n_used, n_blocks)
    return _moe_combine(x, ys, dest, weights, gt)


def kernel(x, c, ada_w, ada_b, norm_mix, norm_ffn, w_in, rw_mu, rw_w0, rw_w2, rw_a0, rw_a2, rw_g2, rw_kk, rw_ka, rw_rk, rw_lnw, rw_lnb, rw_v0, rw_v1, rw_v2, ml_conv_w, ml_conv_b, ml_ib, ml_fb, ml_norm, hg_lb, hg_norm, p_a, p_b, p_c, w_out, moe_gw, moe_gb, moe_ew, moe_eb, ex_gate, ex_up, ex_down, final_norm):
    batch, t, d = x.shape
    assert batch == 1 and d == D_MODEL and t % 512 == 0
    depth = ada_w.shape[0]
    x = x.reshape(t, d)
    lbs = jnp.cumsum(jax.nn.softmax(hg_lb.astype(F32), axis=0), axis=0)
    lbs = lbs - lbs[:1]
    mod = _adaln_mod(c, ada_w, ada_b)
    zeros_d = jnp.zeros((d,), F32)
    c_ml = RW_COLS
    c_mg = c_ml + 2 * ML_QK_WIDTH + 2 * ML_WIDTH
    c_hg = c_mg + 2 * ML_HEADS
    c_gt = c_hg + 4 * HG_WIDTH
    v_first = None
    for l in range(depth):
        sh_m, sc_m, gt_m, sh_f, sc_f, gt_f = jnp.split(mod[l], 6)
        lp = dict(rw_mu=rw_mu[l], rw_w0=rw_w0[l], rw_w2=rw_w2[l], rw_a0=rw_a0[l], rw_a2=rw_a2[l], rw_g2=rw_g2[l],
                  rw_kk=rw_kk[l], rw_ka=rw_ka[l], rw_rk=rw_rk[l], rw_lnw=rw_lnw[l], rw_lnb=rw_lnb[l],
                  ml_conv_w=ml_conv_w[l], ml_conv_b=ml_conv_b[l], ml_ib=ml_ib[l], ml_fb=ml_fb[l],
                  ml_norm=ml_norm[l], hg_norm=hg_norm[l],
                  moe_gw=moe_gw[l], moe_gb=moe_gb[l], moe_ew=moe_ew[l], moe_eb=moe_eb[l])
        if l > 0:
            lp.update(rw_v0=rw_v0[l - 1], rw_v1=rw_v1[l - 1], rw_v2=rw_v2[l - 1])
        h = _norm(x, norm_mix[l], sc_m, sh_m, modulate=True, out_dtype=BF16)
        wl = w_in[l]
        u_rw = _mm(h, wl[:, :c_ml].astype(BF16), jnp.zeros((c_ml,), F32), out_dtype=BF16, tn=c_ml)
        u_ml = _mm(h, wl[:, c_ml:c_mg].astype(BF16), jnp.zeros((c_mg - c_ml,), F32), out_dtype=BF16, tn=ML_WIDTH)
        w_mg = jnp.zeros((d, LANES), F32).at[:, :2 * ML_HEADS].set(wl[:, c_mg:c_hg])
        u_mg = _mm(h, w_mg.astype(BF16), jnp.zeros((LANES,), F32), out_dtype=F32, tn=LANES)
        u_hg = _mm(h, wl[:, c_hg:c_gt].astype(BF16), jnp.zeros((c_gt - c_hg,), F32), out_dtype=BF16, tn=ML_WIDTH)
        y_a, v_first = _rwkv_branch(u_rw, v_first, lp)
        y_b = _mlstm_branch(u_ml, u_mg, lp)
        y_c = _hgrn_branch(u_hg, lbs[l], lp)
        x = _merge_out(h, y_a, y_b, y_c, wl[:, c_gt:].astype(BF16), p_a[l].astype(BF16), p_b[l].astype(BF16),
                       p_c[l].astype(BF16), w_out[l].astype(BF16), x, gt_m)
        h = _norm(x, norm_ffn[l], sc_f, sh_f, modulate=True, out_dtype=F32)
        x = _hier_moe(x, h, lp, gt_f, l, ex_gate, ex_up, ex_down)
    out = _norm(x, final_norm, zeros_d, zeros_d, modulate=False, out_dtype=F32)
    return out.reshape(batch, t, d)
```

```python
import functools

import numpy as np
import jax
import jax.numpy as jnp
from jax import lax
from jax.experimental import pallas as pl
from jax.experimental.pallas import tpu as pltpu

F32 = jnp.float32
BF16 = jnp.bfloat16

D_MODEL = 2048
DEPTH = 2
RW_HEADS, RW_HEAD_DIM, RW_WIDTH = 8, 64, 512
RW_LORA = 256
RW_COLS = 3 * RW_WIDTH + RW_LORA
RW_GN_EPS = 64e-5
RW_GROUP = 4
RW_CHUNKS_PER_STEP = 4
ML_HEADS, ML_QK_DIM, ML_V_DIM = 4, 128, 256
ML_QK_WIDTH, ML_WIDTH = 512, 1024
ML_CONV = 4
ML_SOFTCAP = 15.0
ML_CHUNKS_PER_STEP = 2
HG_HEADS, HG_DIM, HG_WIDTH = 4, 128, 512
HG_CHUNK = 128
CHUNK = 64
MOE_GROUPS, MOE_EPG, MOE_EXPERTS, MOE_TOP_K, MOE_D_FF = 4, 8, 32, 2, 1024
NORM_EPS = 1e-6

LANES = 128
SUBLANES = 8
MOE_ROWS = 256
DMA_PRIORITIES = 2
VMEM_LIMIT = 56 * 1024 * 1024


def _params(sem, limit=VMEM_LIMIT):
    return pltpu.CompilerParams(dimension_semantics=sem, vmem_limit_bytes=limit)


def _bf(x):
    return x.astype(BF16)


def _dot(a, b):
    return jnp.dot(_bf(a), _bf(b), preferred_element_type=F32)


def _dot_nt(a, b):
    return lax.dot_general(_bf(a), _bf(b), (((1,), (1,)), ((), ())), preferred_element_type=F32)


def _dot_tn(a, b):
    return lax.dot_general(_bf(a), _bf(b), (((0,), (0,)), ((), ())), preferred_element_type=F32)


def _bf16_pieces(x, pieces):
    out = []
    rest = x
    for _ in range(pieces):
        part = rest.astype(BF16)
        out.append(part)
        rest = rest - part.astype(F32)
    return out


def _sel_dot(sel, x, pieces):
    acc = None
    for part in _bf16_pieces(x, pieces):
        term = jnp.dot(sel, part, preferred_element_type=F32)
        acc = term if acc is None else acc + term
    return acc


def _seg_sum(x, bd):
    acc = None
    for part in _bf16_pieces(x, 2):
        term = jnp.dot(part, bd, preferred_element_type=F32)
        acc = term if acc is None else acc + term
    return acc


def _sigmoid(x):
    return jax.nn.sigmoid(x)


def _silu(x):
    return x * jax.nn.sigmoid(x)


def _log_sigmoid(x):
    return jnp.minimum(x, 0.0) - jnp.log1p(jnp.exp(-jnp.abs(x)))


def _softplus(x):
    return jnp.maximum(x, 0.0) + jnp.log1p(jnp.exp(-jnp.abs(x)))


def _mod_kernel(c_ref, w_ref, b_ref, o_ref):
    cond = _silu(c_ref[...])
    o_ref[0] = _dot(cond, w_ref[0]) + b_ref[0]


def _adaln_mod(c, ada_w, ada_b):
    depth, d, n = ada_w.shape
    tn = 1024
    c8 = jnp.zeros((SUBLANES, d), F32).at[0].set(c[0])
    out = pl.pallas_call(
        _mod_kernel,
        grid=(depth, n // tn),
        in_specs=[pl.BlockSpec((SUBLANES, d), lambda l, j: (0, 0)),
                  pl.BlockSpec((1, d, tn), lambda l, j: (l, 0, j)),
                  pl.BlockSpec((1, 1, tn), lambda l, j: (l, 0, j))],
        out_specs=pl.BlockSpec((1, SUBLANES, tn), lambda l, j: (l, 0, j)),
        out_shape=jax.ShapeDtypeStruct((depth, SUBLANES, n), F32),
        compiler_params=_params(("arbitrary", "arbitrary")),
        name="adaln_mod",
    )(c8, ada_w, ada_b.reshape(depth, 1, n))
    return out[:, 0, :]


HI_HALF = -65536


def _pack_pairs(y):
    n = y.shape[1] // 2
    bits = lax.bitcast_convert_type(y.astype(BF16).astype(F32), jnp.int32)
    return lax.shift_right_logical(bits[:, :n], 16) | (bits[:, n:] & HI_HALF)


def _unpack_pairs(p):
    lo = lax.bitcast_convert_type(lax.shift_left(p, 16), F32)
    hi = lax.bitcast_convert_type(p & HI_HALF, F32)
    return jnp.concatenate([lo, hi], axis=1)


def _norm_kernel(x_ref, g_ref, sc_ref, sh_ref, o_ref, *, modulate, pack):
    x = x_ref[...]
    y = x * lax.rsqrt(jnp.mean(x * x, axis=-1, keepdims=True) + NORM_EPS) * g_ref[...]
    if modulate:
        y = y * (1.0 + sc_ref[...]) + sh_ref[...]
    o_ref[...] = _pack_pairs(y) if pack else y.astype(o_ref.dtype)


def _norm(x, g, sc, sh, *, modulate, out_dtype, pack=False):
    t, d = x.shape
    tm = min(512, t)
    row = pl.BlockSpec((1, d), lambda i: (0, 0))
    d_out = d // 2 if pack else d
    return pl.pallas_call(
        functools.partial(_norm_kernel, modulate=modulate, pack=pack),
        grid=(t // tm,),
        in_specs=[pl.BlockSpec((tm, d), lambda i: (i, 0)), row, row, row],
        out_specs=pl.BlockSpec((tm, d_out), lambda i: (i, 0)),
        out_shape=jax.ShapeDtypeStruct((t, d_out), out_dtype),
        compiler_params=_params(("arbitrary",)),
        name="rmsnorm_mod" if modulate else "rmsnorm",
    )(x, g.reshape(1, d), sc.reshape(1, d), sh.reshape(1, d))


def _mm_kernel(a_ref, w_ref, b_ref, o_ref):
    o_ref[...] = (_dot(a_ref[...], w_ref[...]) + b_ref[...]).astype(o_ref.dtype)


def _row_tile(m):
    return 1024 if m % 1024 == 0 else min(512, m)


def _mm(a, w, bias, *, out_dtype, tn):
    m, k = a.shape
    n = w.shape[1]
    tm = _row_tile(m)
    return pl.pallas_call(
        _mm_kernel,
        grid=(n // tn, m // tm),
        in_specs=[pl.BlockSpec((tm, k), lambda j, i: (i, 0)),
                  pl.BlockSpec((k, tn), lambda j, i: (0, j)),
                  pl.BlockSpec((1, tn), lambda j, i: (0, j))],
        out_specs=pl.BlockSpec((tm, tn), lambda j, i: (i, j)),
        out_shape=jax.ShapeDtypeStruct((m, n), out_dtype),
        compiler_params=_params(("arbitrary", "arbitrary")),
        name="dense_matmul",
    )(a, w, bias.reshape(1, n))


def _rwkv_kernel(*refs, has_vres):
    if has_vres:
        (u_ref, vf_ref, mu_ref, wl_ref, w0_ref, a0_ref, kk_ref, ka_ref, rk_ref, lnw_ref, lnb_ref,
         v0_ref, v1_ref, v2_ref, bd_ref, tril_ref, y_ref, xbuf, st_ref) = refs
    else:
        (u_ref, mu_ref, wl_ref, w0_ref, a0_ref, kk_ref, ka_ref, rk_ref, lnw_ref, lnb_ref,
         bd_ref, tril_ref, y_ref, vf_out_ref, xbuf, st_ref) = refs
    L = CHUNK
    W = RW_WIDTH
    TB = u_ref.shape[0]
    n_chunks = TB // L

    @pl.when(pl.program_id(0) == 0)
    def _():
        xbuf[0:SUBLANES, :] = jnp.zeros((SUBLANES, RW_COLS), F32)
        st_ref[...] = jnp.zeros(st_ref.shape, F32)

    u = u_ref[...].astype(F32)
    xbuf[SUBLANES:SUBLANES + TB, :] = u
    prev = xbuf[SUBLANES - 1:SUBLANES - 1 + TB, :]
    xbuf[0:SUBLANES, :] = u[TB - SUBLANES:TB, :]
    xs = u + mu_ref[...] * (prev - u)
    r = xs[:, 0:W]
    k = xs[:, W:2 * W]
    v = xs[:, 2 * W:3 * W]
    lr = xs[:, 3 * W:3 * W + RW_LORA]
    lane = lax.broadcasted_iota(jnp.int32, lr.shape, 1)
    act = jnp.where(lane < 64, jnp.tanh(lr), jnp.where(lane < 128, lr, _sigmoid(lr)))
    lo = _dot(act, wl_ref[...])
    z = w0_ref[...] + lo[:, 0:W]
    log_w = -_softplus(-z) - 0.5
    ld = -jnp.exp(log_w)
    iclr = _sigmoid(a0_ref[...] + lo[:, W:2 * W])
    gate = lo[:, 2 * W:3 * W]
    if has_vres:
        vv = _dot(_dot(v, v1_ref[...]), v2_ref[...])
        v = v + (vf_ref[...] - v) * _sigmoid(v0_ref[...] + vv)
    else:
        vf_out_ref[...] = v
    bd = bd_ref[...]
    kk = k * kk_ref[...]
    k = k * (1.0 + (iclr - 1.0) * ka_ref[...])
    sums = _seg_sum(jnp.concatenate([kk * kk, r * k * rk_ref[...]], axis=0), bd)
    kk = kk / jnp.maximum(jnp.sqrt(sums[0:TB, :]), 1e-12)
    a = -kk
    b = kk * iclr
    bonus = sums[TB:2 * TB, :] * v

    cum = _sel_dot(tril_ref[...], ld, 2)
    cl_rows = [cum[(c + 1) * L - 1:(c + 1) * L, :] for c in range(n_chunks)]
    cl = jnp.concatenate([jnp.broadcast_to(x, (L, W)) for x in cl_rows], axis=0)
    e_neg = jnp.exp(-cum)
    e_last = jnp.exp(cl - cum)
    a_t = a * jnp.exp(cum - ld)
    r_t = r * jnp.exp(cum)
    b_t = b * e_neg
    k_t = k * e_neg
    b_h = b * e_last
    k_h = k * e_last

    G = RW_GROUP
    S = G * L
    row = lax.broadcasted_iota(jnp.int32, (S, S), 0)
    col = lax.broadcasted_iota(jnp.int32, (S, S), 1)
    same = (row // L) == (col // L)
    strict = same & (row > col)
    incl = same & (row >= col)
    eye = (row == col).astype(F32)
    lane_head = lax.broadcasted_iota(jnp.int32, (1, S), 1) // RW_HEAD_DIM
    head_masks = [(lane_head == j).astype(F32) for j in range(G)]

    n_groups = RW_HEADS // G
    units = [(c, g) for c in range(n_chunks) for g in range(n_groups)]

    def unit_rows(x, c, g):
        return x[c * L:(c + 1) * L, g * S:(g + 1) * S]

    def stack(x, c, g):
        xs_ = unit_rows(x, c, g)
        return jnp.concatenate([xs_ * m for m in head_masks], axis=0)

    def tile(x, c, g):
        return jnp.concatenate([unit_rows(x, c, g)] * G, axis=0)

    a_s = {u: stack(a_t, *u) for u in units}
    r_s = {u: stack(r_t, *u) for u in units}
    v_s = {u: stack(v, *u) for u in units}
    bk_h = {u: jnp.concatenate([stack(b_h, *u), stack(k_h, *u)], axis=0) for u in units}
    xm = {u: _dot_nt(jnp.concatenate([a_s[u], r_s[u]], axis=0),
                     jnp.concatenate([tile(b_t, *u), tile(k_t, *u)], axis=0)) for u in units}
    n_ab = {u: jnp.where(strict, xm[u][0:S, 0:S], 0.0) for u in units}
    n_ak = {u: jnp.where(strict, xm[u][0:S, S:2 * S], 0.0) for u in units}
    r_bk = {u: jnp.concatenate([jnp.where(incl, xm[u][S:2 * S, 0:S], 0.0),
                                jnp.where(incl, xm[u][S:2 * S, S:2 * S], 0.0)], axis=1) for u in units}
    tinv = {u: eye + n_ab[u] for u in units}
    pw = dict(n_ab)
    for _ in range(5):
        pw = {u: _dot(pw[u], pw[u]) for u in units}
        tinv = {u: tinv[u] + _dot(tinv[u], pw[u]) for u in units}
    nv = {u: _dot(n_ak[u], v_s[u]) for u in units}
    pq = {u: _dot(tinv[u], jnp.concatenate([a_s[u], nv[u]], axis=1)) for u in units}
    p_s = {u: pq[u][:, 0:S] for u in units}
    qv = {u: jnp.concatenate([pq[u][:, S:2 * S], v_s[u]], axis=0) for u in units}
    y1 = {u: r_s[u] + _dot(r_bk[u][:, 0:S], p_s[u]) for u in units}
    y0 = {u: _dot(r_bk[u], qv[u]) for u in units}
    gt = {u: _dot_tn(p_s[u], bk_h[u][0:S, :]) for u in units}
    ht = {u: _dot_tn(qv[u], bk_h[u]) for u in units}

    state = [st_ref[g] for g in range(n_groups)]
    y_rows = []
    for c in range(n_chunks):
        w_last = jnp.exp(cl_rows[c])
        ys = []
        for g in range(n_groups):
            u = (c, g)
            st = state[g]
            y_s = _dot_nt(y1[u], st) + y0[u]
            state[g] = st * w_last[:, g * S:(g + 1) * S] + _dot(st, gt[u]) + ht[u]
            y_g = y_s[0:L, :]
            for j in range(1, G):
                y_g = y_g + y_s[j * L:(j + 1) * L, :]
            ys.append(y_g)
        y_rows.append(jnp.concatenate(ys, axis=1))
    for g in range(n_groups):
        st_ref[g] = state[g]
    y = jnp.concatenate(y_rows, axis=0)

    inv_n = 1.0 / RW_HEAD_DIM
    mean = _seg_sum(y, bd) * inv_n
    dlt = y - mean
    var = _seg_sum(dlt * dlt, bd) * inv_n
    yn = dlt * lax.rsqrt(var + RW_GN_EPS) * lnw_ref[...] + lnb_ref[...]
    y_ref[...] = ((yn + bonus) * gate).astype(y_ref.dtype)


def _blockdiag_ones(width, head):
    idx = np.arange(width) // head
    return jnp.asarray((idx[:, None] == idx[None, :]).astype(np.float32), BF16)


def _tril_ones(n, blocks=1):
    return jnp.asarray(np.kron(np.eye(blocks, dtype=np.float32), np.tril(np.ones((n, n), np.float32))), BF16)


def _rwkv_branch(u_rw, v_first, lp):
    t = u_rw.shape[0]
    L = CHUNK
    has_vres = v_first is not None
    W = RW_WIDTH
    vec = lambda a: a.reshape(1, -1).astype(F32)
    full = lambda shape: pl.BlockSpec(shape, lambda i: (0,) * len(shape))
    tb = RW_CHUNKS_PER_STEP * L
    rows = lambda width: pl.BlockSpec((tb, width), lambda i: (i, 0))
    wl = jnp.zeros((RW_LORA, 3 * W), F32)
    wl = wl.at[0:64, 0:W].set(lp['rw_w2']).at[64:128, W:2 * W].set(lp['rw_a2']).at[128:256, 2 * W:].set(lp['rw_g2'])
    ins = [u_rw]
    specs = [rows(RW_COLS)]
    if has_vres:
        ins.append(v_first)
        specs.append(rows(W))
    ins += [vec(lp['rw_mu']), wl.astype(BF16), vec(lp['rw_w0']), vec(lp['rw_a0']), vec(lp['rw_kk']),
            vec(lp['rw_ka']), vec(lp['rw_rk']), vec(lp['rw_lnw']), vec(lp['rw_lnb'])]
    specs += [full((1, RW_COLS)), full((RW_LORA, 3 * W))] + [full((1, W))] * 7
    if has_vres:
        v1 = jnp.zeros((W, LANES), F32).at[:, :lp['rw_v1'].shape[1]].set(lp['rw_v1'])
        v2 = jnp.zeros((LANES, W), F32).at[:lp['rw_v2'].shape[0], :].set(lp['rw_v2'])
        ins += [vec(lp['rw_v0']), v1.astype(BF16), v2.astype(BF16)]
        specs += [full((1, W)), full((W, LANES)), full((LANES, W))]
    ins += [_blockdiag_ones(W, RW_HEAD_DIM), _tril_ones(L, RW_CHUNKS_PER_STEP)]
    specs += [full((W, W)), full((tb, tb))]
    out_shape = [jax.ShapeDtypeStruct((t, W), BF16)]
    out_specs = [rows(W)]
    if not has_vres:
        out_shape.append(jax.ShapeDtypeStruct((t, W), F32))
        out_specs.append(rows(W))
    outs = pl.pallas_call(
        functools.partial(_rwkv_kernel, has_vres=has_vres),
        grid=(t // tb,),
        in_specs=specs,
        out_specs=out_specs,
        out_shape=out_shape,
        scratch_shapes=[pltpu.VMEM((tb + SUBLANES, RW_COLS), F32),
                        pltpu.VMEM((RW_HEADS // RW_GROUP, RW_GROUP * RW_HEAD_DIM, RW_GROUP * RW_HEAD_DIM), F32)],
        compiler_params=_params(("arbitrary",)),
        name="rwkv7_branch",
    )(*ins)
    if has_vres:
        return outs[0], v_first
    return outs[0], outs[1]


def _mlstm_kernel(qk_ref, v_ref, o_ref, g_ref, cw_ref, cb_ref, gb_ref, ng_ref, tril_ref,
                  y_ref, xbuf, c_ref, m_ref):
    L = CHUNK
    TB = qk_ref.shape[0]
    n_chunks = TB // L

    @pl.when(pl.program_id(0) == 0)
    def _():
        xbuf[0:SUBLANES, :] = jnp.zeros((SUBLANES, 2 * ML_QK_WIDTH), F32)
        c_ref[...] = jnp.zeros(c_ref.shape, F32)
        m_ref[...] = jnp.zeros(m_ref.shape, F32)

    x0 = qk_ref[...].astype(F32)
    xbuf[SUBLANES:SUBLANES + TB, :] = x0
    conv = cb_ref[...] + cw_ref[ML_CONV - 1:ML_CONV, :] * x0
    for dly in range(1, ML_CONV):
        conv = conv + cw_ref[ML_CONV - 1 - dly:ML_CONV - dly, :] * xbuf[SUBLANES - dly:SUBLANES - dly + TB, :]
    xbuf[0:SUBLANES, :] = x0[TB - SUBLANES:TB, :]
    qk = _silu(conv)
    q = qk[:, :ML_QK_WIDTH]
    k = qk[:, ML_QK_WIDTH:] * (ML_QK_DIM ** -0.5)

    pre = g_ref[...] + gb_ref[...]
    cap = ML_SOFTCAP * jnp.tanh(pre / ML_SOFTCAP)
    lane = lax.broadcasted_iota(jnp.int32, pre.shape, 1)
    gates = jnp.where(lane < ML_HEADS, cap, _log_sigmoid(cap))
    csum = _sel_dot(tril_ref[...], gates, 3)
    gates_t = gates.T
    csum_t = csum.T

    row = lax.broadcasted_iota(jnp.int32, (L, L), 0)
    col = lax.broadcasted_iota(jnp.int32, (L, L), 1)
    causal = row >= col
    ones_col = (lax.broadcasted_iota(jnp.int32, (L, LANES), 1) == 0).astype(F32)
    vv = v_ref[...].astype(F32)
    og = o_ref[...].astype(F32)
    units = [(c, h) for c in range(n_chunks) for h in range(ML_HEADS)]

    def rows(c):
        return slice(c * L, (c + 1) * L)

    qh = {(c, h): q[rows(c), h * ML_QK_DIM:(h + 1) * ML_QK_DIM] for c, h in units}
    kh = {(c, h): k[rows(c), h * ML_QK_DIM:(h + 1) * ML_QK_DIM] for c, h in units}
    vh = {(c, h): jnp.concatenate([vv[rows(c), h * ML_V_DIM:(h + 1) * ML_V_DIM], ones_col], axis=1)
          for c, h in units}
    b_col = {(c, h): csum[rows(c), ML_HEADS + h:ML_HEADS + h + 1] for c, h in units}
    li_col = {(c, h): gates[rows(c), h:h + 1] for c, h in units}
    dmat = {(c, h): jnp.where(causal, b_col[c, h] - csum_t[ML_HEADS + h:ML_HEADS + h + 1, rows(c)]
                              + gates_t[h:h + 1, rows(c)], -jnp.inf) for c, h in units}
    dmax = {u: jnp.max(dmat[u], axis=-1, keepdims=True) for u in units}
    s_loc = {u: _dot_nt(qh[u], kh[u]) * jnp.exp(dmat[u] - dmax[u]) for u in units}
    sv = {u: _dot(s_loc[u], vh[u]) for u in units}
    b_last = {u: b_col[u][L - 1:L, :] for u in units}
    g = {u: b_last[u] - b_col[u] + li_col[u] for u in units}
    gmax = {u: jnp.max(g[u], axis=0, keepdims=True) for u in units}
    kv = {u: _dot_tn(jnp.exp(g[u] - gmax[u]) * kh[u], vh[u]) for u in units}

    c_state = [c_ref[h] for h in range(ML_HEADS)]
    m_state = [m_ref[h][0:1, 0:1] for h in range(ML_HEADS)]
    out_rows = []
    for c in range(n_chunks):
        outs = []
        for h in range(ML_HEADS):
            u = (c, h)
            m_prev = m_state[h]
            m_inter = b_col[u] + m_prev
            m_t = jnp.maximum(m_inter, dmax[u])
            num_aug = jnp.exp(dmax[u] - m_t) * sv[u] + jnp.exp(m_inter - m_t) * _dot(qh[u], c_state[h])
            num = num_aug[:, :ML_V_DIM]
            den = num_aug[:, ML_V_DIM:ML_V_DIM + 1]
            hh = num / jnp.maximum(jnp.abs(den), jnp.exp(-m_t))
            m_new = jnp.maximum(b_last[u] + m_prev, gmax[u])
            c_state[h] = jnp.exp(b_last[u] + m_prev - m_new) * c_state[h] + jnp.exp(gmax[u] - m_new) * kv[u]
            m_state[h] = m_new
            ng = ng_ref[:, h * ML_V_DIM:(h + 1) * ML_V_DIM]
            hn = hh * lax.rsqrt(jnp.mean(hh * hh, axis=-1, keepdims=True) + NORM_EPS) * ng
            outs.append(hn * _sigmoid(og[rows(c), h * ML_V_DIM:(h + 1) * ML_V_DIM]))
        out_rows.append(jnp.concatenate(outs, axis=1))
    for h in range(ML_HEADS):
        c_ref[h] = c_state[h]
        m_ref[h] = jnp.broadcast_to(m_state[h], (SUBLANES, LANES))
    y_ref[...] = jnp.concatenate(out_rows, axis=0).astype(y_ref.dtype)


def _mlstm_branch(u_ml, u_mg, lp):
    t = u_ml.shape[0]
    L = CHUNK
    full = lambda shape: pl.BlockSpec(shape, lambda i: (0,) * len(shape))
    gb = jnp.zeros((1, LANES), F32).at[0, 0:ML_HEADS].set(lp['ml_ib']).at[0, ML_HEADS:2 * ML_HEADS].set(lp['ml_fb'])
    tb = ML_CHUNKS_PER_STEP * L
    return pl.pallas_call(
        _mlstm_kernel,
        grid=(t // tb,),
        in_specs=[pl.BlockSpec((tb, ML_WIDTH), lambda i: (i, 0)),
                  pl.BlockSpec((tb, ML_WIDTH), lambda i: (i, 1)),
                  pl.BlockSpec((tb, ML_WIDTH), lambda i: (i, 2)),
                  pl.BlockSpec((tb, LANES), lambda i: (i, 0)),
                  full((ML_CONV, 2 * ML_QK_WIDTH)), full((1, 2 * ML_QK_WIDTH)), full((1, LANES)),
                  full((1, ML_WIDTH)), full((tb, tb))],
        out_specs=pl.BlockSpec((tb, ML_WIDTH), lambda i: (i, 0)),
        out_shape=jax.ShapeDtypeStruct((t, ML_WIDTH), BF16),
        scratch_shapes=[pltpu.VMEM((tb + SUBLANES, 2 * ML_QK_WIDTH), F32),
                        pltpu.VMEM((ML_HEADS, ML_QK_DIM, ML_V_DIM + LANES), F32),
                        pltpu.VMEM((ML_HEADS, SUBLANES, LANES), F32)],
        compiler_params=_params(("arbitrary",)),
        name="mlstm_branch",
    )(u_ml, u_ml, u_ml, u_mg, lp['ml_conv_w'], lp['ml_conv_b'].reshape(1, -1), gb,
      lp['ml_norm'].reshape(1, -1), _tril_ones(L, ML_CHUNKS_PER_STEP))


def _hgrn_levels():
    L = HG_CHUNK
    t = np.arange(L)
    sel = [np.tril(np.ones((L, L), np.float32))]
    masks = []
    size = L
    while size >= 2:
        half = size // 2
        ref_row = (t // size) * size + half - 1
        sel.append((t[None, :] <= ref_row[:, None]).astype(np.float32))
        same = (t[:, None] // size) == (t[None, :] // size)
        masks.append((same & ((t[:, None] % size) >= half) & ((t[None, :] % size) < half)).astype(np.float32))
        size = half
    return np.concatenate(sel, axis=0), np.stack(masks)


def _hgrn_kernel(u_ref, la_ref, lc_ref, lb_ref, ng_ref, sel_ref, mask_ref, y_ref, s_ref, *, levels):
    L = HG_CHUNK
    W = HG_WIDTH

    @pl.when(pl.program_id(0) == 0)
    def _():
        s_ref[...] = jnp.zeros(s_ref.shape, F32)

    u = u_ref[...].astype(F32)
    q = _silu(u[:, 0:W])
    f_pre = u[:, W:2 * W]
    ii = u[:, 2 * W:3 * W]
    g_pre = u[:, 3 * W:4 * W]
    la = la_ref[...]
    lc = lc_ref[...] + _log_sigmoid(f_pre)
    log_f = jnp.maximum(la, lc) + jnp.log1p(jnp.exp(-jnp.abs(la - lc)))
    k = (1.0 - lb_ref[...]) * _sigmoid(-f_pre)

    cr = _sel_dot(sel_ref[...], log_f, 2)
    cum = cr[0:L, :]
    cl = cum[L - 1:L, :]
    q_in = q * jnp.exp(cum)
    k_out = k * jnp.exp(cl - cum)
    w_last = jnp.exp(cl)
    heads = range(HG_HEADS)
    sl = [slice(h * HG_DIM, (h + 1) * HG_DIM) for h in heads]
    att = [jnp.zeros((L, L), F32) for _ in heads]
    for lv in range(levels):
        ref = cr[(lv + 1) * L:(lv + 2) * L, :]
        qe = q * jnp.exp(jnp.minimum(cum - ref, 0.0))
        ke = k * jnp.exp(jnp.minimum(ref - cum, 0.0))
        mask = mask_ref[lv]
        att = [att[h] + mask * _dot_nt(qe[:, sl[h]], ke[:, sl[h]]) for h in heads]
    diag = [jnp.sum(q[:, sl[h]] * k[:, sl[h]], axis=-1, keepdims=True) for h in heads]
    intra = [_dot(att[h], ii[:, sl[h]]) + diag[h] * ii[:, sl[h]] for h in heads]
    kv = [_dot_tn(ii[:, sl[h]], k_out[:, sl[h]]) for h in heads]
    outs = []
    for h in heads:
        st = s_ref[h]
        o = intra[h] + _dot_nt(q_in[:, sl[h]], st)
        s_ref[h] = st * w_last[:, sl[h]] + kv[h]
        on = o * lax.rsqrt(jnp.mean(o * o, axis=-1, keepdims=True) + NORM_EPS) * ng_ref[:, sl[h]]
        outs.append(on * _silu(g_pre[:, sl[h]]))
    y_ref[...] = jnp.concatenate(outs, axis=1).astype(y_ref.dtype)


def _hgrn_branch(u_hg, lb, lp):
    t = u_hg.shape[0]
    L = HG_CHUNK
    sel, masks = _hgrn_levels()
    levels = masks.shape[0]
    full = lambda shape: pl.BlockSpec(shape, lambda i: (0,) * len(shape))
    lb = lb.reshape(1, -1).astype(F32)
    return pl.pallas_call(
        functools.partial(_hgrn_kernel, levels=levels),
        grid=(t // L,),
        in_specs=[pl.BlockSpec((L, 4 * HG_WIDTH), lambda i: (i, 0)),
                  full((1, HG_WIDTH)), full((1, HG_WIDTH)), full((1, HG_WIDTH)), full((1, HG_WIDTH)),
                  full(sel.shape), full(masks.shape)],
        out_specs=pl.BlockSpec((L, HG_WIDTH), lambda i: (i, 0)),
        out_shape=jax.ShapeDtypeStruct((t, HG_WIDTH), BF16),
        scratch_shapes=[pltpu.VMEM((HG_HEADS, HG_DIM, HG_DIM), F32)],
        compiler_params=_params(("arbitrary",)),
        name="hgrn2_branch",
    )(u_hg, jnp.log(lb), jnp.log1p(-lb), lb, lp['hg_norm'].reshape(1, -1),
      jnp.asarray(sel, BF16), jnp.asarray(masks, F32))


def _gate_merge_kernel(h_ref, ya_ref, yb_ref, yc_ref, wga_ref, wgb_ref, wgc_ref, pa_ref, pb_ref, pc_ref, o_ref):
    h = h_ref[...]
    merged = (_sigmoid(_dot(h, wga_ref[...])) * _dot(ya_ref[...], pa_ref[...])
              + _sigmoid(_dot(h, wgb_ref[...])) * _dot(yb_ref[...], pb_ref[...])
              + _sigmoid(_dot(h, wgc_ref[...])) * _dot(yc_ref[...], pc_ref[...]))
    o_ref[...] = merged.astype(o_ref.dtype)


def _residual_proj_kernel(a_ref, w_ref, x_ref, gt_ref, o_ref):
    o_ref[...] = x_ref[...] + gt_ref[...] * _dot(a_ref[...], w_ref[...])


def _merge_out(h, ya, yb, yc, w_gates, pa, pb, pc, wo, x, gt):
    t, d = x.shape
    tm = _row_tile(t)
    tn = 512
    nb = d // tn
    merged = pl.pallas_call(
        _gate_merge_kernel,
        grid=(nb, t // tm),
        in_specs=[pl.BlockSpec((tm, d), lambda n, i: (i, 0)),
                  pl.BlockSpec((tm, RW_WIDTH), lambda n, i: (i, 0)),
                  pl.BlockSpec((tm, ML_WIDTH), lambda n, i: (i, 0)),
                  pl.BlockSpec((tm, HG_WIDTH), lambda n, i: (i, 0)),
                  pl.BlockSpec((d, tn), lambda n, i: (0, n)),
                  pl.BlockSpec((d, tn), lambda n, i: (0, nb + n)),
                  pl.BlockSpec((d, tn), lambda n, i: (0, 2 * nb + n)),
                  pl.BlockSpec((RW_WIDTH, tn), lambda n, i: (0, n)),
                  pl.BlockSpec((ML_WIDTH, tn), lambda n, i: (0, n)),
                  pl.BlockSpec((HG_WIDTH, tn), lambda n, i: (0, n))],
        out_specs=pl.BlockSpec((tm, tn), lambda n, i: (i, n)),
        out_shape=jax.ShapeDtypeStruct((t, d), BF16),
        compiler_params=_params(("arbitrary", "arbitrary")),
        name="gate_merge",
    )(h, ya, yb, yc, w_gates, w_gates, w_gates, pa, pb, pc)
    tn2 = 1024
    return pl.pallas_call(
        _residual_proj_kernel,
        grid=(d // tn2, t // tm),
        in_specs=[pl.BlockSpec((tm, d), lambda n, i: (i, 0)),
                  pl.BlockSpec((d, tn2), lambda n, i: (0, n)),
                  pl.BlockSpec((tm, tn2), lambda n, i: (i, n)),
                  pl.BlockSpec((1, tn2), lambda n, i: (0, n))],
        out_specs=pl.BlockSpec((tm, tn2), lambda n, i: (i, n)),
        out_shape=jax.ShapeDtypeStruct((t, d), F32),
        compiler_params=_params(("arbitrary", "arbitrary")),
        name="residual_proj",
    )(merged, wo, x, gt.reshape(1, d))


def _moe_up_kernel(be_ref, nu_ref, tok_ref, nxt_ref, h_ref, wg_ref, wu_ref, o_ref, xbuf, wg_bf, wu_bf, sem):
    i = pl.program_id(0)
    n_used = nu_ref[0]
    rows = xbuf.shape[1]
    slot = lax.rem(i, 2)

    def row_copy(s, r, src_row):
        return pltpu.make_async_copy(h_ref.at[pl.ds(src_row, 1)], xbuf.at[s, pl.ds(r, 1)], sem.at[s])

    def start_block(s, ids_ref):
        def body(q, c):
            for prio in range(DMA_PRIORITIES):
                r = DMA_PRIORITIES * q + prio
                row_copy(s, r, ids_ref[0, 0, r]).start(priority=prio)
            return c

        lax.fori_loop(0, rows // DMA_PRIORITIES, body, 0, unroll=4)

    def wait_block(s):
        def body(r, c):
            row_copy(s, r, 0).wait()
            return c

        lax.fori_loop(0, rows, body, 0, unroll=8)

    @pl.when((i == 0) & (n_used > 0))
    def _():
        start_block(0, tok_ref)

    @pl.when(i + 1 < n_used)
    def _():
        start_block(1 - slot, nxt_ref)

    prev = be_ref[jnp.maximum(i - 1, 0)]

    @pl.when((i == 0) | (be_ref[i] != prev))
    def _():
        wg_bf[...] = wg_ref[0, 0].astype(BF16)
        wu_bf[...] = wu_ref[0, 0].astype(BF16)

    @pl.when(i < n_used)
    def _():
        wait_block(slot)
        x = _bf(_unpack_pairs(xbuf[slot]))
        g = jnp.dot(x, wg_bf[...], preferred_element_type=F32)
        u = jnp.dot(x, wu_bf[...], preferred_element_type=F32)
        o_ref[...] = (_silu(g) * u).astype(o_ref.dtype)

    @pl.when(i >= n_used)
    def _():
        o_ref[...] = jnp.zeros(o_ref.shape, o_ref.dtype)


def _moe_up(h, row_token, w_gate, w_up, layer, block_expert, n_used, n_blocks):
    d, ff = w_gate.shape[2], w_gate.shape[3]
    bm = MOE_ROWS
    tok = row_token.reshape(n_blocks, 1, bm)
    last = n_blocks - 1
    return pl.pallas_call(
        _moe_up_kernel,
        grid_spec=pltpu.PrefetchScalarGridSpec(
            num_scalar_prefetch=2,
            grid=(n_blocks,),
            in_specs=[pl.BlockSpec((1, 1, bm), lambda i, be, nu: (i, 0, 0), memory_space=pltpu.SMEM),
                      pl.BlockSpec((1, 1, bm), lambda i, be, nu: (jnp.minimum(i + 1, last), 0, 0),
                                   memory_space=pltpu.SMEM),
                      pl.BlockSpec(memory_space=pl.ANY),
                      pl.BlockSpec((1, 1, d, ff), lambda i, be, nu: (layer, be[i], 0, 0)),
                      pl.BlockSpec((1, 1, d, ff), lambda i, be, nu: (layer, be[i], 0, 0))],
            out_specs=pl.BlockSpec((bm, ff), lambda i, be, nu: (i, 0)),
            scratch_shapes=[pltpu.VMEM((2, bm, d // 2), jnp.int32),
                            pltpu.VMEM((d, ff), BF16), pltpu.VMEM((d, ff), BF16),
                            pltpu.SemaphoreType.DMA((2,))]),
        out_shape=jax.ShapeDtypeStruct((n_blocks * bm, ff), BF16),
        compiler_params=_params(("arbitrary",)),
        name="moe_up",
    )(block_expert, n_used, tok, tok, h, w_gate, w_up)


def _moe_down_kernel(be_ref, nu_ref, x_ref, wd_ref, o_ref, wd_bf):
    i = pl.program_id(0)
    prev = be_ref[jnp.maximum(i - 1, 0)]

    @pl.when((i == 0) | (be_ref[i] != prev))
    def _():
        wd_bf[...] = wd_ref[0, 0].astype(BF16)

    @pl.when(i < nu_ref[0])
    def _():
        o_ref[...] = _pack_pairs(jnp.dot(x_ref[...], wd_bf[...], preferred_element_type=F32))

    @pl.when(i >= nu_ref[0])
    def _():
        o_ref[...] = jnp.zeros(o_ref.shape, o_ref.dtype)


def _moe_down(hmid, w_down, layer, block_expert, n_used, n_blocks):
    ff = hmid.shape[1]
    d = w_down.shape[3]
    bm = MOE_ROWS
    return pl.pallas_call(
        _moe_down_kernel,
        grid_spec=pltpu.PrefetchScalarGridSpec(
            num_scalar_prefetch=2,
            grid=(n_blocks,),
            in_specs=[pl.BlockSpec((bm, ff), lambda i, be, nu: (i, 0)),
                      pl.BlockSpec((1, 1, ff, d), lambda i, be, nu: (layer, be[i], 0, 0))],
            out_specs=pl.BlockSpec((bm, d // 2), lambda i, be, nu: (i, 0)),
            scratch_shapes=[pltpu.VMEM((ff, d), BF16)]),
        out_shape=jax.ShapeDtypeStruct((n_blocks * bm, d // 2), jnp.int32),
        compiler_params=_params(("arbitrary",)),
        name="moe_down",
    )(block_expert, n_used, hmid, w_down)


def _combine_kernel(dest_ref, w_ref, x_ref, gt_ref, ys_ref, o_ref, buf, sem):
    tc = x_ref.shape[0]

    def row_copy(tok, choice, src_row):
        return pltpu.make_async_copy(ys_ref.at[pl.ds(src_row, 1)], buf.at[choice, pl.ds(tok, 1)], sem)

    def start(tok, c):
        for choice in range(MOE_TOP_K):
            row_copy(tok, choice, dest_ref[0, 0, MOE_TOP_K * tok + choice]).start(priority=choice)
        return c

    def wait(tok, c):
        for choice in range(MOE_TOP_K):
            row_copy(tok, choice, 0).wait()
        return c

    lax.fori_loop(0, tc, start, 0, unroll=4)
    lax.fori_loop(0, tc, wait, 0, unroll=4)
    w = w_ref[...]
    y = w[:, 0:1] * _unpack_pairs(buf[0]) + w[:, 1:2] * _unpack_pairs(buf[1])
    o_ref[...] = x_ref[...] + gt_ref[...] * y


def _moe_combine(x, ys, dest, weights, gt):
    t, d = x.shape
    tc = min(256, t)
    return pl.pallas_call(
        _combine_kernel,
        grid=(t // tc,),
        in_specs=[pl.BlockSpec((1, 1, MOE_TOP_K * tc), lambda i: (i, 0, 0), memory_space=pltpu.SMEM),
                  pl.BlockSpec((tc, MOE_TOP_K), lambda i: (i, 0)),
                  pl.BlockSpec((tc, d), lambda i: (i, 0)),
                  pl.BlockSpec((1, d), lambda i: (0, 0)),
                  pl.BlockSpec(memory_space=pl.ANY)],
        out_specs=pl.BlockSpec((tc, d), lambda i: (i, 0)),
        out_shape=jax.ShapeDtypeStruct((t, d), F32),
        scratch_shapes=[pltpu.VMEM((MOE_TOP_K, tc, d // 2), jnp.int32), pltpu.SemaphoreType.DMA(())],
        compiler_params=_params(("arbitrary",)),
        name="moe_combine",
    )(dest.reshape(t // tc, 1, MOE_TOP_K * tc), weights, x, gt.reshape(1, d), ys)


def _router_kernel(h_ref, w_ref, b_ref, tril_ref, route_ref, count_ref, run_ref):
    @pl.when(pl.program_id(0) == 0)
    def _():
        run_ref[...] = jnp.zeros(run_ref.shape, F32)

    logits = _dot(_unpack_pairs(h_ref[...]), w_ref[...]) + b_ref[...]
    lane = lax.broadcasted_iota(jnp.int32, logits.shape, 1)
    neg = -jnp.inf
    far = jnp.int32(LANES)

    def first_max(vals):
        top = jnp.max(vals, axis=-1, keepdims=True)
        return top, jnp.min(jnp.where(vals == top, lane, far), axis=-1, keepdims=True)

    is_group = lane < MOE_GROUPS
    g_top, g_idx = first_max(jnp.where(is_group, logits, neg))
    p_group = 1.0 / jnp.sum(jnp.where(is_group, jnp.exp(logits - g_top), 0.0), axis=-1, keepdims=True)
    first = MOE_GROUPS + MOE_EPG * g_idx
    cand = jnp.where((lane >= first) & (lane < first + MOE_EPG), logits, neg)
    e1, i1 = first_max(cand)
    e2, i2 = first_max(jnp.where(lane == i1, neg, cand))
    z = jnp.exp(e2 - e1)
    w1 = p_group / (1.0 + z)
    w2 = p_group * z / (1.0 + z)
    x1 = i1 - MOE_GROUPS
    x2 = i2 - MOE_GROUPS
    hit1 = lane == x1
    hit2 = lane == x2
    onehot = (hit1 | hit2).astype(F32)
    cum = jnp.dot(tril_ref[...], _bf(onehot), preferred_element_type=F32) + run_ref[0:1, :]
    rank1 = jnp.sum(jnp.where(hit1, cum, 0.0), axis=-1, keepdims=True) - 1.0
    rank2 = jnp.sum(jnp.where(hit2, cum, 0.0), axis=-1, keepdims=True) - 1.0
    last = cum[cum.shape[0] - 1:cum.shape[0], :]
    run_ref[...] = jnp.broadcast_to(last, run_ref.shape)
    count_ref[...] = jnp.broadcast_to(last, count_ref.shape)
    cols = (w1, w2, x1.astype(F32), x2.astype(F32), rank1, rank2)
    packed = jnp.zeros(logits.shape, F32)
    for j, cvals in enumerate(cols):
        packed = jnp.where(lane == j, cvals, packed)
    route_ref[...] = packed


def _router(h, w_r, b_r):
    t = h.shape[0]
    d = w_r.shape[0]
    tm = MOE_ROWS
    return pl.pallas_call(
        _router_kernel,
        grid=(t // tm,),
        in_specs=[pl.BlockSpec((tm, d // 2), lambda i: (i, 0)),
                  pl.BlockSpec((d, LANES), lambda i: (0, 0)),
                  pl.BlockSpec((1, LANES), lambda i: (0, 0)),
                  pl.BlockSpec((tm, tm), lambda i: (0, 0))],
        out_specs=[pl.BlockSpec((tm, LANES), lambda i: (i, 0)),
                   pl.BlockSpec((SUBLANES, LANES), lambda i: (0, 0))],
        out_shape=[jax.ShapeDtypeStruct((t, LANES), F32), jax.ShapeDtypeStruct((SUBLANES, LANES), F32)],
        scratch_shapes=[pltpu.VMEM((SUBLANES, LANES), F32)],
        compiler_params=_params(("arbitrary",)),
        name="moe_router",
    )(h, w_r, b_r.reshape(1, LANES), _tril_ones(tm))


def _dispatch_plan(expert, rank, counts, n_blocks):
    bm = MOE_ROWS
    flat = expert.reshape(-1)
    n_assign = flat.shape[0]
    padded = (counts + bm - 1) // bm * bm
    pad_end = jnp.cumsum(padded)
    dest = (pad_end - padded)[flat] + rank.reshape(-1)
    row_token = jnp.zeros((n_blocks * bm,), jnp.int32).at[dest].set(
        jnp.arange(n_assign, dtype=jnp.int32) // MOE_TOP_K)
    block_start = jnp.arange(n_blocks, dtype=jnp.int32) * bm
    block_expert = jnp.minimum(jnp.sum((pad_end[None, :] <= block_start[:, None]).astype(jnp.int32), axis=1),
                               MOE_EXPERTS - 1)
    n_used = (pad_end[-1] // bm).reshape(1)
    return dest.astype(jnp.int32), row_token, block_expert.astype(jnp.int32), n_used.astype(jnp.int32)


def _hier_moe(x, h, lp, gt, layer, ex_gate, ex_up, ex_down):
    t, d = x.shape
    w_r = jnp.zeros((d, LANES), F32).at[:, :MOE_GROUPS].set(lp['moe_gw'])
    w_r = w_r.at[:, MOE_GROUPS:MOE_GROUPS + MOE_EXPERTS].set(lp['moe_ew'])
    b_r = jnp.zeros((LANES,), F32).at[:MOE_GROUPS].set(lp['moe_gb'])
    b_r = b_r.at[MOE_GROUPS:MOE_GROUPS + MOE_EXPERTS].set(lp['moe_eb'])
    route, count_rows = _router(h, w_r.astype(BF16), b_r)
    weights = route[:, 0:2]
    expert = route[:, 2:4].astype(jnp.int32)
    rank = route[:, 4:6].astype(jnp.int32)
    counts = count_rows[0, :MOE_EXPERTS].astype(jnp.int32)
    n_blocks = (t * MOE_TOP_K) // MOE_ROWS + MOE_EXPERTS
    dest, row_token, block_expert, n_used = _dispatch_plan(expert, rank, counts, n_blocks)
    hmid = _moe_up(h, row_token, ex_gate, ex_up, layer, block_expert, n_used, n_blocks)
    ys = _moe_down(hmid, ex_down, layer, block_expert, n_used, n_blocks)
    return _moe_combine(x, ys, dest, weights, gt)


def kernel(x, c, ada_w, ada_b, norm_mix, norm_ffn, w_in, rw_mu, rw_w0, rw_w2, rw_a0, rw_a2, rw_g2, rw_kk, rw_ka, rw_rk, rw_lnw, rw_lnb, rw_v0, rw_v1, rw_v2, ml_conv_w, ml_conv_b, ml_ib, ml_fb, ml_norm, hg_lb, hg_norm, p_a, p_b, p_c, w_out, moe_gw, moe_gb, moe_ew, moe_eb, ex_gate, ex_up, ex_down, final_norm):
    batch, t, d = x.shape
    assert batch == 1 and d == D_MODEL and t % 512 == 0
    depth = ada_w.shape[0]
    x = x.reshape(t, d)
    lbs = jnp.cumsum(jax.nn.softmax(hg_lb.astype(F32), axis=0), axis=0)
    lbs = lbs - lbs[:1]
    mod = _adaln_mod(c, ada_w, ada_b)
    zeros_d = jnp.zeros((d,), F32)
    c_ml = RW_COLS
    c_mg = c_ml + 2 * ML_QK_WIDTH + 2 * ML_WIDTH
    c_hg = c_mg + 2 * ML_HEADS
    c_gt = c_hg + 4 * HG_WIDTH
    v_first = None
    for l in range(depth):
        sh_m, sc_m, gt_m, sh_f, sc_f, gt_f = jnp.split(mod[l], 6)
        lp = dict(rw_mu=rw_mu[l], rw_w0=rw_w0[l], rw_w2=rw_w2[l], rw_a0=rw_a0[l], rw_a2=rw_a2[l], rw_g2=rw_g2[l],
                  rw_kk=rw_kk[l], rw_ka=rw_ka[l], rw_rk=rw_rk[l], rw_lnw=rw_lnw[l], rw_lnb=rw_lnb[l],
                  ml_conv_w=ml_conv_w[l], ml_conv_b=ml_conv_b[l], ml_ib=ml_ib[l], ml_fb=ml_fb[l],
                  ml_norm=ml_norm[l], hg_norm=hg_norm[l],
                  moe_gw=moe_gw[l], moe_gb=moe_gb[l], moe_ew=moe_ew[l], moe_eb=moe_eb[l])
        if l > 0:
            lp.update(rw_v0=rw_v0[l - 1], rw_v1=rw_v1[l - 1], rw_v2=rw_v2[l - 1])
        h = _norm(x, norm_mix[l], sc_m, sh_m, modulate=True, out_dtype=BF16)
        wl = w_in[l]
        u_rw = _mm(h, wl[:, :c_ml].astype(BF16), jnp.zeros((c_ml,), F32), out_dtype=BF16, tn=c_ml)
        u_ml = _mm(h, wl[:, c_ml:c_mg].astype(BF16), jnp.zeros((c_mg - c_ml,), F32), out_dtype=BF16, tn=ML_WIDTH)
        w_mg = jnp.zeros((d, LANES), F32).at[:, :2 * ML_HEADS].set(wl[:, c_mg:c_hg])
        u_mg = _mm(h, w_mg.astype(BF16), jnp.zeros((LANES,), F32), out_dtype=F32, tn=LANES)
        u_hg = _mm(h, wl[:, c_hg:c_gt].astype(BF16), jnp.zeros((c_gt - c_hg,), F32), out_dtype=BF16, tn=ML_WIDTH)
        y_a, v_first = _rwkv_branch(u_rw, v_first, lp)
        y_b = _mlstm_branch(u_ml, u_mg, lp)
        y_c = _hgrn_branch(u_hg, lbs[l], lp)
        x = _merge_out(h, y_a, y_b, y_c, wl[:, c_gt:].astype(BF16), p_a[l].astype(BF16), p_b[l].astype(BF16),
                       p_c[l].astype(BF16), w_out[l].astype(BF16), x, gt_m)
        h = _norm(x, norm_ffn[l], sc_f, sh_f, modulate=True, out_dtype=jnp.int32, pack=True)
        x = _hier_moe(x, h, lp, gt_f, l, ex_gate, ex_up, ex_down)
    out = _norm(x, final_norm, zeros_d, zeros_d, modulate=False, out_dtype=F32)
    return out.reshape(batch, t, d)
```

```python
import functools

import numpy as np
import jax
import jax.numpy as jnp
from jax import lax
from jax.experimental import pallas as pl
from jax.experimental.pallas import tpu as pltpu

F32 = jnp.float32
BF16 = jnp.bfloat16

D_MODEL = 2048
DEPTH = 2
RW_HEADS, RW_HEAD_DIM, RW_WIDTH = 8, 64, 512
RW_LORA = 256
RW_COLS = 3 * RW_WIDTH + RW_LORA
RW_GN_EPS = 64e-5
RW_GROUP = 4
RW_CHUNKS_PER_STEP = 4
ML_HEADS, ML_QK_DIM, ML_V_DIM = 4, 128, 256
ML_QK_WIDTH, ML_WIDTH = 512, 1024
ML_CONV = 4
ML_SOFTCAP = 15.0
ML_CHUNKS_PER_STEP = 2
HG_HEADS, HG_DIM, HG_WIDTH = 4, 128, 512
HG_CHUNK = 128
CHUNK = 64
MOE_GROUPS, MOE_EPG, MOE_EXPERTS, MOE_TOP_K, MOE_D_FF = 4, 8, 32, 2, 1024
NORM_EPS = 1e-6

LANES = 128
SUBLANES = 8
MOE_ROWS = 256
DMA_PRIORITIES = 2
VMEM_LIMIT = 56 * 1024 * 1024


def _params(sem, limit=VMEM_LIMIT):
    return pltpu.CompilerParams(dimension_semantics=sem, vmem_limit_bytes=limit)


def _bf(x):
    return x.astype(BF16)


def _dot(a, b):
    return jnp.dot(_bf(a), _bf(b), preferred_element_type=F32)


def _dot_nt(a, b):
    return lax.dot_general(_bf(a), _bf(b), (((1,), (1,)), ((), ())), preferred_element_type=F32)


def _dot_tn(a, b):
    return lax.dot_general(_bf(a), _bf(b), (((0,), (0,)), ((), ())), preferred_element_type=F32)


def _bf16_pieces(x, pieces):
    out = []
    rest = x
    for _ in range(pieces):
        part = rest.astype(BF16)
        out.append(part)
        rest = rest - part.astype(F32)
    return out


def _sel_dot(sel, x, pieces):
    acc = None
    for part in _bf16_pieces(x, pieces):
        term = jnp.dot(sel, part, preferred_element_type=F32)
        acc = term if acc is None else acc + term
    return acc


def _seg_sum(x, bd):
    acc = None
    for part in _bf16_pieces(x, 2):
        term = jnp.dot(part, bd, preferred_element_type=F32)
        acc = term if acc is None else acc + term
    return acc


def _sigmoid(x):
    return jax.nn.sigmoid(x)


def _silu(x):
    return x * jax.nn.sigmoid(x)


def _log_sigmoid(x):
    return jnp.minimum(x, 0.0) - jnp.log1p(jnp.exp(-jnp.abs(x)))


def _softplus(x):
    return jnp.maximum(x, 0.0) + jnp.log1p(jnp.exp(-jnp.abs(x)))


def _mod_kernel(c_ref, w_ref, b_ref, o_ref):
    cond = _silu(c_ref[...])
    o_ref[0] = _dot(cond, w_ref[0]) + b_ref[0]


def _adaln_mod(c, ada_w, ada_b):
    depth, d, n = ada_w.shape
    tn = 1024
    c8 = jnp.zeros((SUBLANES, d), F32).at[0].set(c[0])
    out = pl.pallas_call(
        _mod_kernel,
        grid=(depth, n // tn),
        in_specs=[pl.BlockSpec((SUBLANES, d), lambda l, j: (0, 0)),
                  pl.BlockSpec((1, d, tn), lambda l, j: (l, 0, j)),
                  pl.BlockSpec((1, 1, tn), lambda l, j: (l, 0, j))],
        out_specs=pl.BlockSpec((1, SUBLANES, tn), lambda l, j: (l, 0, j)),
        out_shape=jax.ShapeDtypeStruct((depth, SUBLANES, n), F32),
        compiler_params=_params(("arbitrary", "arbitrary")),
        name="adaln_mod",
    )(c8, ada_w, ada_b.reshape(depth, 1, n))
    return out[:, 0, :]


HI_HALF = -65536


def _pack_pairs(y):
    n = y.shape[1] // 2
    bits = lax.bitcast_convert_type(y.astype(BF16).astype(F32), jnp.int32)
    return lax.shift_right_logical(bits[:, :n], 16) | (bits[:, n:] & HI_HALF)


def _unpack_pairs(p):
    lo = lax.bitcast_convert_type(lax.shift_left(p, 16), F32)
    hi = lax.bitcast_convert_type(p & HI_HALF, F32)
    return jnp.concatenate([lo, hi], axis=1)


def _rms_mod(x, g_ref, sc_ref, sh_ref, modulate):
    y = x * lax.rsqrt(jnp.mean(x * x, axis=-1, keepdims=True) + NORM_EPS) * g_ref[...]
    if modulate:
        y = y * (1.0 + sc_ref[...]) + sh_ref[...]
    return y


def _norm_kernel(x_ref, g_ref, sc_ref, sh_ref, o_ref):
    o_ref[...] = _rms_mod(x_ref[...], g_ref, sc_ref, sh_ref, True).astype(o_ref.dtype)


def _norm_mod(x, g, sc, sh):
    t, d = x.shape
    tm = min(512, t)
    row = pl.BlockSpec((1, d), lambda i: (0, 0))
    return pl.pallas_call(
        _norm_kernel,
        grid=(t // tm,),
        in_specs=[pl.BlockSpec((tm, d), lambda i: (i, 0)), row, row, row],
        out_specs=pl.BlockSpec((tm, d), lambda i: (i, 0)),
        out_shape=jax.ShapeDtypeStruct((t, d), BF16),
        compiler_params=_params(("arbitrary",)),
        name="rmsnorm_mod",
    )(x, g.reshape(1, d), sc.reshape(1, d), sh.reshape(1, d))


def _mm_kernel(a_ref, w_ref, b_ref, o_ref):
    o_ref[...] = (_dot(a_ref[...], w_ref[...]) + b_ref[...]).astype(o_ref.dtype)


def _row_tile(m):
    return 1024 if m % 1024 == 0 else min(512, m)


def _mm(a, w, bias, *, out_dtype, tn):
    m, k = a.shape
    n = w.shape[1]
    tm = _row_tile(m)
    return pl.pallas_call(
        _mm_kernel,
        grid=(n // tn, m // tm),
        in_specs=[pl.BlockSpec((tm, k), lambda j, i: (i, 0)),
                  pl.BlockSpec((k, tn), lambda j, i: (0, j)),
                  pl.BlockSpec((1, tn), lambda j, i: (0, j))],
        out_specs=pl.BlockSpec((tm, tn), lambda j, i: (i, j)),
        out_shape=jax.ShapeDtypeStruct((m, n), out_dtype),
        compiler_params=_params(("arbitrary", "arbitrary")),
        name="dense_matmul",
    )(a, w, bias.reshape(1, n))


def _rwkv_kernel(*refs, has_vres):
    if has_vres:
        (u_ref, vf_ref, mu_ref, wl_ref, w0_ref, a0_ref, kk_ref, ka_ref, rk_ref, lnw_ref, lnb_ref,
         v0_ref, v1_ref, v2_ref, bd_ref, tril_ref, y_ref, xbuf, st_ref) = refs
    else:
        (u_ref, mu_ref, wl_ref, w0_ref, a0_ref, kk_ref, ka_ref, rk_ref, lnw_ref, lnb_ref,
         bd_ref, tril_ref, y_ref, vf_out_ref, xbuf, st_ref) = refs
    L = CHUNK
    W = RW_WIDTH
    TB = u_ref.shape[0]
    n_chunks = TB // L

    @pl.when(pl.program_id(0) == 0)
    def _():
        xbuf[0:SUBLANES, :] = jnp.zeros((SUBLANES, RW_COLS), F32)
        st_ref[...] = jnp.zeros(st_ref.shape, F32)

    u = u_ref[...].astype(F32)
    xbuf[SUBLANES:SUBLANES + TB, :] = u
    prev = xbuf[SUBLANES - 1:SUBLANES - 1 + TB, :]
    xbuf[0:SUBLANES, :] = u[TB - SUBLANES:TB, :]
    xs = u + mu_ref[...] * (prev - u)
    r = xs[:, 0:W]
    k = xs[:, W:2 * W]
    v = xs[:, 2 * W:3 * W]
    lr = xs[:, 3 * W:3 * W + RW_LORA]
    lane = lax.broadcasted_iota(jnp.int32, lr.shape, 1)
    act = jnp.where(lane < 64, jnp.tanh(lr), jnp.where(lane < 128, lr, _sigmoid(lr)))
    lo = _dot(act, wl_ref[...])
    z = w0_ref[...] + lo[:, 0:W]
    log_w = -_softplus(-z) - 0.5
    ld = -jnp.exp(log_w)
    iclr = _sigmoid(a0_ref[...] + lo[:, W:2 * W])
    gate = lo[:, 2 * W:3 * W]
    if has_vres:
        vv = _dot(_dot(v, v1_ref[...]), v2_ref[...])
        v = v + (vf_ref[...] - v) * _sigmoid(v0_ref[...] + vv)
    else:
        vf_out_ref[...] = v
    bd = bd_ref[...]
    kk = k * kk_ref[...]
    k = k * (1.0 + (iclr - 1.0) * ka_ref[...])
    sums = _seg_sum(jnp.concatenate([kk * kk, r * k * rk_ref[...]], axis=0), bd)
    kk = kk / jnp.maximum(jnp.sqrt(sums[0:TB, :]), 1e-12)
    a = -kk
    b = kk * iclr
    bonus = sums[TB:2 * TB, :] * v

    cum = _sel_dot(tril_ref[...], ld, 2)
    cl_rows = [cum[(c + 1) * L - 1:(c + 1) * L, :] for c in range(n_chunks)]
    cl = jnp.concatenate([jnp.broadcast_to(x, (L, W)) for x in cl_rows], axis=0)
    e_neg = jnp.exp(-cum)
    e_last = jnp.exp(cl - cum)
    a_t = a * jnp.exp(cum - ld)
    r_t = r * jnp.exp(cum)
    b_t = b * e_neg
    k_t = k * e_neg
    b_h = b * e_last
    k_h = k * e_last

    G = RW_GROUP
    S = G * L
    row = lax.broadcasted_iota(jnp.int32, (S, S), 0)
    col = lax.broadcasted_iota(jnp.int32, (S, S), 1)
    same = (row // L) == (col // L)
    strict = same & (row > col)
    incl = same & (row >= col)
    eye = (row == col).astype(F32)
    lane_head = lax.broadcasted_iota(jnp.int32, (1, S), 1) // RW_HEAD_DIM
    head_masks = [(lane_head == j).astype(F32) for j in range(G)]

    n_groups = RW_HEADS // G
    units = [(c, g) for c in range(n_chunks) for g in range(n_groups)]

    def unit_rows(x, c, g):
        return x[c * L:(c + 1) * L, g * S:(g + 1) * S]

    def stack(x, c, g):
        xs_ = unit_rows(x, c, g)
        return jnp.concatenate([xs_ * m for m in head_masks], axis=0)

    def tile(x, c, g):
        return jnp.concatenate([unit_rows(x, c, g)] * G, axis=0)

    a_s = {u: stack(a_t, *u) for u in units}
    r_s = {u: stack(r_t, *u) for u in units}
    v_s = {u: stack(v, *u) for u in units}
    bk_h = {u: jnp.concatenate([stack(b_h, *u), stack(k_h, *u)], axis=0) for u in units}
    xm = {u: _dot_nt(jnp.concatenate([a_s[u], r_s[u]], axis=0),
                     jnp.concatenate([tile(b_t, *u), tile(k_t, *u)], axis=0)) for u in units}
    n_ab = {u: jnp.where(strict, xm[u][0:S, 0:S], 0.0) for u in units}
    n_ak = {u: jnp.where(strict, xm[u][0:S, S:2 * S], 0.0) for u in units}
    r_bk = {u: jnp.concatenate([jnp.where(incl, xm[u][S:2 * S, 0:S], 0.0),
                                jnp.where(incl, xm[u][S:2 * S, S:2 * S], 0.0)], axis=1) for u in units}
    tinv = {u: eye + n_ab[u] for u in units}
    pw = dict(n_ab)
    for _ in range(5):
        pw = {u: _dot(pw[u], pw[u]) for u in units}
        tinv = {u: tinv[u] + _dot(tinv[u], pw[u]) for u in units}
    nv = {u: _dot(n_ak[u], v_s[u]) for u in units}
    pq = {u: _dot(tinv[u], jnp.concatenate([a_s[u], nv[u]], axis=1)) for u in units}
    p_s = {u: pq[u][:, 0:S] for u in units}
    qv = {u: jnp.concatenate([pq[u][:, S:2 * S], v_s[u]], axis=0) for u in units}
    y1 = {u: r_s[u] + _dot(r_bk[u][:, 0:S], p_s[u]) for u in units}
    y0 = {u: _dot(r_bk[u], qv[u]) for u in units}
    gt = {u: _dot_tn(p_s[u], bk_h[u][0:S, :]) for u in units}
    ht = {u: _dot_tn(qv[u], bk_h[u]) for u in units}

    state = [st_ref[g] for g in range(n_groups)]
    y_rows = []
    for c in range(n_chunks):
        w_last = jnp.exp(cl_rows[c])
        ys = []
        for g in range(n_groups):
            u = (c, g)
            st = state[g]
            y_s = _dot_nt(y1[u], st) + y0[u]
            state[g] = st * w_last[:, g * S:(g + 1) * S] + _dot(st, gt[u]) + ht[u]
            y_g = y_s[0:L, :]
            for j in range(1, G):
                y_g = y_g + y_s[j * L:(j + 1) * L, :]
            ys.append(y_g)
        y_rows.append(jnp.concatenate(ys, axis=1))
    for g in range(n_groups):
        st_ref[g] = state[g]
    y = jnp.concatenate(y_rows, axis=0)

    inv_n = 1.0 / RW_HEAD_DIM
    mean = _seg_sum(y, bd) * inv_n
    dlt = y - mean
    var = _seg_sum(dlt * dlt, bd) * inv_n
    yn = dlt * lax.rsqrt(var + RW_GN_EPS) * lnw_ref[...] + lnb_ref[...]
    y_ref[...] = ((yn + bonus) * gate).astype(y_ref.dtype)


def _blockdiag_ones(width, head):
    idx = np.arange(width) // head
    return jnp.asarray((idx[:, None] == idx[None, :]).astype(np.float32), BF16)


def _tril_ones(n, blocks=1):
    return jnp.asarray(np.kron(np.eye(blocks, dtype=np.float32), np.tril(np.ones((n, n), np.float32))), BF16)


def _rwkv_branch(u_rw, v_first, lp):
    t = u_rw.shape[0]
    L = CHUNK
    has_vres = v_first is not None
    W = RW_WIDTH
    vec = lambda a: a.reshape(1, -1).astype(F32)
    full = lambda shape: pl.BlockSpec(shape, lambda i: (0,) * len(shape))
    tb = RW_CHUNKS_PER_STEP * L
    rows = lambda width: pl.BlockSpec((tb, width), lambda i: (i, 0))
    wl = jnp.zeros((RW_LORA, 3 * W), F32)
    wl = wl.at[0:64, 0:W].set(lp['rw_w2']).at[64:128, W:2 * W].set(lp['rw_a2']).at[128:256, 2 * W:].set(lp['rw_g2'])
    ins = [u_rw]
    specs = [rows(RW_COLS)]
    if has_vres:
        ins.append(v_first)
        specs.append(rows(W))
    ins += [vec(lp['rw_mu']), wl.astype(BF16), vec(lp['rw_w0']), vec(lp['rw_a0']), vec(lp['rw_kk']),
            vec(lp['rw_ka']), vec(lp['rw_rk']), vec(lp['rw_lnw']), vec(lp['rw_lnb'])]
    specs += [full((1, RW_COLS)), full((RW_LORA, 3 * W))] + [full((1, W))] * 7
    if has_vres:
        v1 = jnp.zeros((W, LANES), F32).at[:, :lp['rw_v1'].shape[1]].set(lp['rw_v1'])
        v2 = jnp.zeros((LANES, W), F32).at[:lp['rw_v2'].shape[0], :].set(lp['rw_v2'])
        ins += [vec(lp['rw_v0']), v1.astype(BF16), v2.astype(BF16)]
        specs += [full((1, W)), full((W, LANES)), full((LANES, W))]
    ins += [_blockdiag_ones(W, RW_HEAD_DIM), _tril_ones(L, RW_CHUNKS_PER_STEP)]
    specs += [full((W, W)), full((tb, tb))]
    out_shape = [jax.ShapeDtypeStruct((t, W), BF16)]
    out_specs = [rows(W)]
    if not has_vres:
        out_shape.append(jax.ShapeDtypeStruct((t, W), F32))
        out_specs.append(rows(W))
    outs = pl.pallas_call(
        functools.partial(_rwkv_kernel, has_vres=has_vres),
        grid=(t // tb,),
        in_specs=specs,
        out_specs=out_specs,
        out_shape=out_shape,
        scratch_shapes=[pltpu.VMEM((tb + SUBLANES, RW_COLS), F32),
                        pltpu.VMEM((RW_HEADS // RW_GROUP, RW_GROUP * RW_HEAD_DIM, RW_GROUP * RW_HEAD_DIM), F32)],
        compiler_params=_params(("arbitrary",)),
        name="rwkv7_branch",
    )(*ins)
    if has_vres:
        return outs[0], v_first
    return outs[0], outs[1]


def _mlstm_kernel(qk_ref, v_ref, o_ref, g_ref, cw_ref, cb_ref, gb_ref, ng_ref, tril_ref,
                  y_ref, xbuf, c_ref, m_ref):
    L = CHUNK
    TB = qk_ref.shape[0]
    n_chunks = TB // L

    @pl.when(pl.program_id(0) == 0)
    def _():
        xbuf[0:SUBLANES, :] = jnp.zeros((SUBLANES, 2 * ML_QK_WIDTH), F32)
        c_ref[...] = jnp.zeros(c_ref.shape, F32)
        m_ref[...] = jnp.zeros(m_ref.shape, F32)

    x0 = qk_ref[...].astype(F32)
    xbuf[SUBLANES:SUBLANES + TB, :] = x0
    conv = cb_ref[...] + cw_ref[ML_CONV - 1:ML_CONV, :] * x0
    for dly in range(1, ML_CONV):
        conv = conv + cw_ref[ML_CONV - 1 - dly:ML_CONV - dly, :] * xbuf[SUBLANES - dly:SUBLANES - dly + TB, :]
    xbuf[0:SUBLANES, :] = x0[TB - SUBLANES:TB, :]
    qk = _silu(conv)
    q = qk[:, :ML_QK_WIDTH]
    k = qk[:, ML_QK_WIDTH:] * (ML_QK_DIM ** -0.5)

    pre = g_ref[...] + gb_ref[...]
    cap = ML_SOFTCAP * jnp.tanh(pre / ML_SOFTCAP)
    lane = lax.broadcasted_iota(jnp.int32, pre.shape, 1)
    gates = jnp.where(lane < ML_HEADS, cap, _log_sigmoid(cap))
    csum = _sel_dot(tril_ref[...], gates, 3)
    gates_t = gates.T
    csum_t = csum.T

    row = lax.broadcasted_iota(jnp.int32, (L, L), 0)
    col = lax.broadcasted_iota(jnp.int32, (L, L), 1)
    causal = row >= col
    ones_col = (lax.broadcasted_iota(jnp.int32, (L, LANES), 1) == 0).astype(F32)
    vv = v_ref[...].astype(F32)
    og = o_ref[...].astype(F32)
    units = [(c, h) for c in range(n_chunks) for h in range(ML_HEADS)]

    def rows(c):
        return slice(c * L, (c + 1) * L)

    qh = {(c, h): q[rows(c), h * ML_QK_DIM:(h + 1) * ML_QK_DIM] for c, h in units}
    kh = {(c, h): k[rows(c), h * ML_QK_DIM:(h + 1) * ML_QK_DIM] for c, h in units}
    vh = {(c, h): jnp.concatenate([vv[rows(c), h * ML_V_DIM:(h + 1) * ML_V_DIM], ones_col], axis=1)
          for c, h in units}
    b_col = {(c, h): csum[rows(c), ML_HEADS + h:ML_HEADS + h + 1] for c, h in units}
    li_col = {(c, h): gates[rows(c), h:h + 1] for c, h in units}
    dmat = {(c, h): jnp.where(causal, b_col[c, h] - csum_t[ML_HEADS + h:ML_HEADS + h + 1, rows(c)]
                              + gates_t[h:h + 1, rows(c)], -jnp.inf) for c, h in units}
    dmax = {u: jnp.max(dmat[u], axis=-1, keepdims=True) for u in units}
    s_loc = {u: _dot_nt(qh[u], kh[u]) * jnp.exp(dmat[u] - dmax[u]) for u in units}
    sv = {u: _dot(s_loc[u], vh[u]) for u in units}
    b_last = {u: b_col[u][L - 1:L, :] for u in units}
    g = {u: b_last[u] - b_col[u] + li_col[u] for u in units}
    gmax = {u: jnp.max(g[u], axis=0, keepdims=True) for u in units}
    kv = {u: _dot_tn(jnp.exp(g[u] - gmax[u]) * kh[u], vh[u]) for u in units}

    c_state = [c_ref[h] for h in range(ML_HEADS)]
    m_state = [m_ref[h][0:1, 0:1] for h in range(ML_HEADS)]
    out_rows = []
    for c in range(n_chunks):
        outs = []
        for h in range(ML_HEADS):
            u = (c, h)
            m_prev = m_state[h]
            m_inter = b_col[u] + m_prev
            m_t = jnp.maximum(m_inter, dmax[u])
            num_aug = jnp.exp(dmax[u] - m_t) * sv[u] + jnp.exp(m_inter - m_t) * _dot(qh[u], c_state[h])
            num = num_aug[:, :ML_V_DIM]
            den = num_aug[:, ML_V_DIM:ML_V_DIM + 1]
            hh = num / jnp.maximum(jnp.abs(den), jnp.exp(-m_t))
            m_new = jnp.maximum(b_last[u] + m_prev, gmax[u])
            c_state[h] = jnp.exp(b_last[u] + m_prev - m_new) * c_state[h] + jnp.exp(gmax[u] - m_new) * kv[u]
            m_state[h] = m_new
            ng = ng_ref[:, h * ML_V_DIM:(h + 1) * ML_V_DIM]
            hn = hh * lax.rsqrt(jnp.mean(hh * hh, axis=-1, keepdims=True) + NORM_EPS) * ng
            outs.append(hn * _sigmoid(og[rows(c), h * ML_V_DIM:(h + 1) * ML_V_DIM]))
        out_rows.append(jnp.concatenate(outs, axis=1))
    for h in range(ML_HEADS):
        c_ref[h] = c_state[h]
        m_ref[h] = jnp.broadcast_to(m_state[h], (SUBLANES, LANES))
    y_ref[...] = jnp.concatenate(out_rows, axis=0).astype(y_ref.dtype)


def _mlstm_branch(u_ml, u_mg, lp):
    t = u_ml.shape[0]
    L = CHUNK
    full = lambda shape: pl.BlockSpec(shape, lambda i: (0,) * len(shape))
    gb = jnp.zeros((1, LANES), F32).at[0, 0:ML_HEADS].set(lp['ml_ib']).at[0, ML_HEADS:2 * ML_HEADS].set(lp['ml_fb'])
    tb = ML_CHUNKS_PER_STEP * L
    return pl.pallas_call(
        _mlstm_kernel,
        grid=(t // tb,),
        in_specs=[pl.BlockSpec((tb, ML_WIDTH), lambda i: (i, 0)),
                  pl.BlockSpec((tb, ML_WIDTH), lambda i: (i, 1)),
                  pl.BlockSpec((tb, ML_WIDTH), lambda i: (i, 2)),
                  pl.BlockSpec((tb, LANES), lambda i: (i, 0)),
                  full((ML_CONV, 2 * ML_QK_WIDTH)), full((1, 2 * ML_QK_WIDTH)), full((1, LANES)),
                  full((1, ML_WIDTH)), full((tb, tb))],
        out_specs=pl.BlockSpec((tb, ML_WIDTH), lambda i: (i, 0)),
        out_shape=jax.ShapeDtypeStruct((t, ML_WIDTH), BF16),
        scratch_shapes=[pltpu.VMEM((tb + SUBLANES, 2 * ML_QK_WIDTH), F32),
                        pltpu.VMEM((ML_HEADS, ML_QK_DIM, ML_V_DIM + LANES), F32),
                        pltpu.VMEM((ML_HEADS, SUBLANES, LANES), F32)],
        compiler_params=_params(("arbitrary",)),
        name="mlstm_branch",
    )(u_ml, u_ml, u_ml, u_mg, lp['ml_conv_w'], lp['ml_conv_b'].reshape(1, -1), gb,
      lp['ml_norm'].reshape(1, -1), _tril_ones(L, ML_CHUNKS_PER_STEP))


def _hgrn_levels():
    L = HG_CHUNK
    t = np.arange(L)
    sel = [np.tril(np.ones((L, L), np.float32))]
    masks = []
    size = L
    while size >= 2:
        half = size // 2
        ref_row = (t // size) * size + half - 1
        sel.append((t[None, :] <= ref_row[:, None]).astype(np.float32))
        same = (t[:, None] // size) == (t[None, :] // size)
        masks.append((same & ((t[:, None] % size) >= half) & ((t[None, :] % size) < half)).astype(np.float32))
        size = half
    return np.concatenate(sel, axis=0), np.stack(masks)


def _hgrn_kernel(u_ref, la_ref, lc_ref, lb_ref, ng_ref, sel_ref, mask_ref, y_ref, s_ref, *, levels):
    L = HG_CHUNK
    W = HG_WIDTH

    @pl.when(pl.program_id(0) == 0)
    def _():
        s_ref[...] = jnp.zeros(s_ref.shape, F32)

    u = u_ref[...].astype(F32)
    q = _silu(u[:, 0:W])
    f_pre = u[:, W:2 * W]
    ii = u[:, 2 * W:3 * W]
    g_pre = u[:, 3 * W:4 * W]
    la = la_ref[...]
    lc = lc_ref[...] + _log_sigmoid(f_pre)
    log_f = jnp.maximum(la, lc) + jnp.log1p(jnp.exp(-jnp.abs(la - lc)))
    k = (1.0 - lb_ref[...]) * _sigmoid(-f_pre)

    cr = _sel_dot(sel_ref[...], log_f, 2)
    cum = cr[0:L, :]
    cl = cum[L - 1:L, :]
    q_in = q * jnp.exp(cum)
    k_out = k * jnp.exp(cl - cum)
    w_last = jnp.exp(cl)
    heads = range(HG_HEADS)
    sl = [slice(h * HG_DIM, (h + 1) * HG_DIM) for h in heads]
    att = [jnp.zeros((L, L), F32) for _ in heads]
    for lv in range(levels):
        ref = cr[(lv + 1) * L:(lv + 2) * L, :]
        qe = q * jnp.exp(jnp.minimum(cum - ref, 0.0))
        ke = k * jnp.exp(jnp.minimum(ref - cum, 0.0))
        mask = mask_ref[lv]
        att = [att[h] + mask * _dot_nt(qe[:, sl[h]], ke[:, sl[h]]) for h in heads]
    diag = [jnp.sum(q[:, sl[h]] * k[:, sl[h]], axis=-1, keepdims=True) for h in heads]
    intra = [_dot(att[h], ii[:, sl[h]]) + diag[h] * ii[:, sl[h]] for h in heads]
    kv = [_dot_tn(ii[:, sl[h]], k_out[:, sl[h]]) for h in heads]
    outs = []
    for h in heads:
        st = s_ref[h]
        o = intra[h] + _dot_nt(q_in[:, sl[h]], st)
        s_ref[h] = st * w_last[:, sl[h]] + kv[h]
        on = o * lax.rsqrt(jnp.mean(o * o, axis=-1, keepdims=True) + NORM_EPS) * ng_ref[:, sl[h]]
        outs.append(on * _silu(g_pre[:, sl[h]]))
    y_ref[...] = jnp.concatenate(outs, axis=1).astype(y_ref.dtype)


def _hgrn_branch(u_hg, lb, lp):
    t = u_hg.shape[0]
    L = HG_CHUNK
    sel, masks = _hgrn_levels()
    levels = masks.shape[0]
    full = lambda shape: pl.BlockSpec(shape, lambda i: (0,) * len(shape))
    lb = lb.reshape(1, -1).astype(F32)
    return pl.pallas_call(
        functools.partial(_hgrn_kernel, levels=levels),
        grid=(t // L,),
        in_specs=[pl.BlockSpec((L, 4 * HG_WIDTH), lambda i: (i, 0)),
                  full((1, HG_WIDTH)), full((1, HG_WIDTH)), full((1, HG_WIDTH)), full((1, HG_WIDTH)),
                  full(sel.shape), full(masks.shape)],
        out_specs=pl.BlockSpec((L, HG_WIDTH), lambda i: (i, 0)),
        out_shape=jax.ShapeDtypeStruct((t, HG_WIDTH), BF16),
        scratch_shapes=[pltpu.VMEM((HG_HEADS, HG_DIM, HG_DIM), F32)],
        compiler_params=_params(("arbitrary",)),
        name="hgrn2_branch",
    )(u_hg, jnp.log(lb), jnp.log1p(-lb), lb, lp['hg_norm'].reshape(1, -1),
      jnp.asarray(sel, BF16), jnp.asarray(masks, F32))


def _gate_merge_kernel(h_ref, ya_ref, yb_ref, yc_ref, wga_ref, wgb_ref, wgc_ref, pa_ref, pb_ref, pc_ref, o_ref):
    h = h_ref[...]
    merged = (_sigmoid(_dot(h, wga_ref[...])) * _dot(ya_ref[...], pa_ref[...])
              + _sigmoid(_dot(h, wgb_ref[...])) * _dot(yb_ref[...], pb_ref[...])
              + _sigmoid(_dot(h, wgc_ref[...])) * _dot(yc_ref[...], pc_ref[...]))
    o_ref[...] = merged.astype(o_ref.dtype)


def _residual_proj_kernel(a_ref, w_ref, x_ref, gt_ref, g_ref, sc_ref, sh_ref, xo_ref, h_ref):
    x_new = x_ref[...] + gt_ref[...] * _dot(a_ref[...], w_ref[...])
    xo_ref[...] = x_new
    h_ref[...] = _pack_pairs(_rms_mod(x_new, g_ref, sc_ref, sh_ref, True))


def _merge_out(h, ya, yb, yc, w_gates, pa, pb, pc, wo, x, gt, g_ffn, sc_ffn, sh_ffn):
    t, d = x.shape
    tm = _row_tile(t)
    tn = 512
    nb = d // tn
    merged = pl.pallas_call(
        _gate_merge_kernel,
        grid=(nb, t // tm),
        in_specs=[pl.BlockSpec((tm, d), lambda n, i: (i, 0)),
                  pl.BlockSpec((tm, RW_WIDTH), lambda n, i: (i, 0)),
                  pl.BlockSpec((tm, ML_WIDTH), lambda n, i: (i, 0)),
                  pl.BlockSpec((tm, HG_WIDTH), lambda n, i: (i, 0)),
                  pl.BlockSpec((d, tn), lambda n, i: (0, n)),
                  pl.BlockSpec((d, tn), lambda n, i: (0, nb + n)),
                  pl.BlockSpec((d, tn), lambda n, i: (0, 2 * nb + n)),
                  pl.BlockSpec((RW_WIDTH, tn), lambda n, i: (0, n)),
                  pl.BlockSpec((ML_WIDTH, tn), lambda n, i: (0, n)),
                  pl.BlockSpec((HG_WIDTH, tn), lambda n, i: (0, n))],
        out_specs=pl.BlockSpec((tm, tn), lambda n, i: (i, n)),
        out_shape=jax.ShapeDtypeStruct((t, d), BF16),
        compiler_params=_params(("arbitrary", "arbitrary")),
        name="gate_merge",
    )(h, ya, yb, yc, w_gates, w_gates, w_gates, pa, pb, pc)
    tm2 = min(512, t)
    row = pl.BlockSpec((1, d), lambda i: (0, 0))
    return pl.pallas_call(
        _residual_proj_kernel,
        grid=(t // tm2,),
        in_specs=[pl.BlockSpec((tm2, d), lambda i: (i, 0)),
                  pl.BlockSpec((d, d), lambda i: (0, 0)),
                  pl.BlockSpec((tm2, d), lambda i: (i, 0)),
                  row, row, row, row],
        out_specs=[pl.BlockSpec((tm2, d), lambda i: (i, 0)),
                   pl.BlockSpec((tm2, d // 2), lambda i: (i, 0))],
        out_shape=[jax.ShapeDtypeStruct((t, d), F32), jax.ShapeDtypeStruct((t, d // 2), jnp.int32)],
        compiler_params=_params(("arbitrary",)),
        name="residual_proj",
    )(merged, wo, x, gt.reshape(1, d), g_ffn.reshape(1, d), sc_ffn.reshape(1, d), sh_ffn.reshape(1, d))


def _moe_up_kernel(be_ref, nu_ref, tok_ref, nxt_ref, h_ref, wg_ref, wu_ref, o_ref, xbuf, wg_bf, wu_bf, sem):
    i = pl.program_id(0)
    n_used = nu_ref[0]
    rows = xbuf.shape[1]
    slot = lax.rem(i, 2)

    def row_copy(s, r, src_row):
        return pltpu.make_async_copy(h_ref.at[pl.ds(src_row, 1)], xbuf.at[s, pl.ds(r, 1)], sem.at[s])

    def start_block(s, ids_ref):
        for r in range(rows):
            row_copy(s, r, ids_ref[0, 0, r]).start(priority=r % DMA_PRIORITIES)

    def wait_block(s):
        def body(r, c):
            row_copy(s, r, 0).wait()
            return c

        lax.fori_loop(0, rows, body, 0, unroll=8)

    @pl.when((i == 0) & (n_used > 0))
    def _():
        start_block(0, tok_ref)

    @pl.when(i + 1 < n_used)
    def _():
        start_block(1 - slot, nxt_ref)

    prev = be_ref[jnp.maximum(i - 1, 0)]

    @pl.when((i == 0) | (be_ref[i] != prev))
    def _():
        wg_bf[...] = wg_ref[0, 0].astype(BF16)
        wu_bf[...] = wu_ref[0, 0].astype(BF16)

    @pl.when(i < n_used)
    def _():
        wait_block(slot)
        x = _bf(_unpack_pairs(xbuf[slot]))
        g = jnp.dot(x, wg_bf[...], preferred_element_type=F32)
        u = jnp.dot(x, wu_bf[...], preferred_element_type=F32)
        o_ref[...] = (_silu(g) * u).astype(o_ref.dtype)

    @pl.when(i >= n_used)
    def _():
        o_ref[...] = jnp.zeros(o_ref.shape, o_ref.dtype)


def _moe_up(h, row_token, w_gate, w_up, layer, block_expert, n_used, n_blocks):
    d, ff = w_gate.shape[2], w_gate.shape[3]
    bm = MOE_ROWS
    tok = row_token.reshape(n_blocks, 1, bm)
    last = n_blocks - 1
    return pl.pallas_call(
        _moe_up_kernel,
        grid_spec=pltpu.PrefetchScalarGridSpec(
            num_scalar_prefetch=2,
            grid=(n_blocks,),
            in_specs=[pl.BlockSpec((1, 1, bm), lambda i, be, nu: (i, 0, 0), memory_space=pltpu.SMEM),
                      pl.BlockSpec((1, 1, bm), lambda i, be, nu: (jnp.minimum(i + 1, last), 0, 0),
                                   memory_space=pltpu.SMEM),
                      pl.BlockSpec(memory_space=pl.ANY),
                      pl.BlockSpec((1, 1, d, ff), lambda i, be, nu: (layer, be[i], 0, 0)),
                      pl.BlockSpec((1, 1, d, ff), lambda i, be, nu: (layer, be[i], 0, 0))],
            out_specs=pl.BlockSpec((bm, ff), lambda i, be, nu: (i, 0)),
            scratch_shapes=[pltpu.VMEM((2, bm, d // 2), jnp.int32),
                            pltpu.VMEM((d, ff), BF16), pltpu.VMEM((d, ff), BF16),
                            pltpu.SemaphoreType.DMA((2,))]),
        out_shape=jax.ShapeDtypeStruct((n_blocks * bm, ff), BF16),
        compiler_params=_params(("arbitrary",)),
        name="moe_up",
    )(block_expert, n_used, tok, tok, h, w_gate, w_up)


def _moe_down_kernel(be_ref, nu_ref, x_ref, wd_ref, o_ref, wd_bf):
    i = pl.program_id(0)
    prev = be_ref[jnp.maximum(i - 1, 0)]

    @pl.when((i == 0) | (be_ref[i] != prev))
    def _():
        wd_bf[...] = wd_ref[0, 0].astype(BF16)

    @pl.when(i < nu_ref[0])
    def _():
        o_ref[...] = _pack_pairs(jnp.dot(x_ref[...], wd_bf[...], preferred_element_type=F32))

    @pl.when(i >= nu_ref[0])
    def _():
        o_ref[...] = jnp.zeros(o_ref.shape, o_ref.dtype)


def _moe_down(hmid, w_down, layer, block_expert, n_used, n_blocks):
    ff = hmid.shape[1]
    d = w_down.shape[3]
    bm = MOE_ROWS
    return pl.pallas_call(
        _moe_down_kernel,
        grid_spec=pltpu.PrefetchScalarGridSpec(
            num_scalar_prefetch=2,
            grid=(n_blocks,),
            in_specs=[pl.BlockSpec((bm, ff), lambda i, be, nu: (i, 0)),
                      pl.BlockSpec((1, 1, ff, d), lambda i, be, nu: (layer, be[i], 0, 0))],
            out_specs=pl.BlockSpec((bm, d // 2), lambda i, be, nu: (i, 0)),
            scratch_shapes=[pltpu.VMEM((ff, d), BF16)]),
        out_shape=jax.ShapeDtypeStruct((n_blocks * bm, d // 2), jnp.int32),
        compiler_params=_params(("arbitrary",)),
        name="moe_down",
    )(block_expert, n_used, hmid, w_down)


def _combine_kernel(dest_ref, nxt_ref, w_ref, x_ref, gt_ref, g_ref, sc_ref, sh_ref, ys_ref, *rest, emit_x, modulate):
    if emit_x:
        xo_ref, h_ref, buf, sem = rest
    else:
        h_ref, buf, sem = rest
    tc = x_ref.shape[0]
    i = pl.program_id(0)
    slot = lax.rem(i, 2)

    def row_copy(s, tok, choice, src_row):
        return pltpu.make_async_copy(ys_ref.at[pl.ds(src_row, 1)], buf.at[s, choice, pl.ds(tok, 1)], sem.at[s])

    def start_tile(s, ids_ref):
        for tok in range(tc):
            for choice in range(MOE_TOP_K):
                row_copy(s, tok, choice, ids_ref[0, 0, MOE_TOP_K * tok + choice]).start(priority=choice)

    def wait_tile(s):
        def body(j, c):
            row_copy(s, 0, 0, 0).wait()
            return c

        lax.fori_loop(0, MOE_TOP_K * tc, body, 0, unroll=8)

    @pl.when(i == 0)
    def _():
        start_tile(0, dest_ref)

    @pl.when(i + 1 < pl.num_programs(0))
    def _():
        start_tile(1 - slot, nxt_ref)

    wait_tile(slot)
    w = w_ref[...]
    y = w[:, 0:1] * _unpack_pairs(buf[slot, 0]) + w[:, 1:2] * _unpack_pairs(buf[slot, 1])
    x_new = x_ref[...] + gt_ref[...] * y
    if emit_x:
        xo_ref[...] = x_new
    h_ref[...] = _rms_mod(x_new, g_ref, sc_ref, sh_ref, modulate).astype(h_ref.dtype)


def _moe_combine(x, ys, dest, weights, gt, g, sc, sh, *, emit_x, modulate, h_dtype):
    t, d = x.shape
    tc = min(256, t)
    n_tiles = t // tc
    ids = dest.reshape(n_tiles, 1, MOE_TOP_K * tc)
    row = pl.BlockSpec((1, d), lambda i: (0, 0))
    tile = pl.BlockSpec((tc, d), lambda i: (i, 0))
    out_specs = [tile]
    out_shape = [jax.ShapeDtypeStruct((t, d), h_dtype)]
    if emit_x:
        out_specs = [tile, tile]
        out_shape = [jax.ShapeDtypeStruct((t, d), F32)] + out_shape
    outs = pl.pallas_call(
        functools.partial(_combine_kernel, emit_x=emit_x, modulate=modulate),
        grid=(n_tiles,),
        in_specs=[pl.BlockSpec((1, 1, MOE_TOP_K * tc), lambda i: (i, 0, 0), memory_space=pltpu.SMEM),
                  pl.BlockSpec((1, 1, MOE_TOP_K * tc), lambda i: (jnp.minimum(i + 1, n_tiles - 1), 0, 0),
                               memory_space=pltpu.SMEM),
                  pl.BlockSpec((tc, MOE_TOP_K), lambda i: (i, 0)),
                  tile, row, row, row, row,
                  pl.BlockSpec(memory_space=pl.ANY)],
        out_specs=out_specs,
        out_shape=out_shape,
        scratch_shapes=[pltpu.VMEM((2, MOE_TOP_K, tc, d // 2), jnp.int32), pltpu.SemaphoreType.DMA((2,))],
        compiler_params=_params(("arbitrary",)),
        name="moe_combine",
    )(ids, ids, weights, x, gt.reshape(1, d), g.reshape(1, d), sc.reshape(1, d), sh.reshape(1, d), ys)
    if emit_x:
        return outs[0], outs[1]
    return None, outs[0]


def _router_kernel(h_ref, w_ref, b_ref, tril_ref, route_ref, count_ref, run_ref):
    @pl.when(pl.program_id(0) == 0)
    def _():
        run_ref[...] = jnp.zeros(run_ref.shape, F32)

    logits = _dot(_unpack_pairs(h_ref[...]), w_ref[...]) + b_ref[...]
    lane = lax.broadcasted_iota(jnp.int32, logits.shape, 1)
    neg = -jnp.inf
    far = jnp.int32(LANES)

    def first_max(vals):
        top = jnp.max(vals, axis=-1, keepdims=True)
        return top, jnp.min(jnp.where(vals == top, lane, far), axis=-1, keepdims=True)

    is_group = lane < MOE_GROUPS
    g_top, g_idx = first_max(jnp.where(is_group, logits, neg))
    p_group = 1.0 / jnp.sum(jnp.where(is_group, jnp.exp(logits - g_top), 0.0), axis=-1, keepdims=True)
    first = MOE_GROUPS + MOE_EPG * g_idx
    cand = jnp.where((lane >= first) & (lane < first + MOE_EPG), logits, neg)
    e1, i1 = first_max(cand)
    e2, i2 = first_max(jnp.where(lane == i1, neg, cand))
    z = jnp.exp(e2 - e1)
    w1 = p_group / (1.0 + z)
    w2 = p_group * z / (1.0 + z)
    x1 = i1 - MOE_GROUPS
    x2 = i2 - MOE_GROUPS
    hit1 = lane == x1
    hit2 = lane == x2
    onehot = (hit1 | hit2).astype(F32)
    cum = jnp.dot(tril_ref[...], _bf(onehot), preferred_element_type=F32) + run_ref[0:1, :]
    rank1 = jnp.sum(jnp.where(hit1, cum, 0.0), axis=-1, keepdims=True) - 1.0
    rank2 = jnp.sum(jnp.where(hit2, cum, 0.0), axis=-1, keepdims=True) - 1.0
    last = cum[cum.shape[0] - 1:cum.shape[0], :]
    run_ref[...] = jnp.broadcast_to(last, run_ref.shape)
    count_ref[...] = jnp.broadcast_to(last, count_ref.shape)
    cols = (w1, w2, x1.astype(F32), x2.astype(F32), rank1, rank2)
    packed = jnp.zeros(logits.shape, F32)
    for j, cvals in enumerate(cols):
        packed = jnp.where(lane == j, cvals, packed)
    route_ref[...] = packed


def _router(h, w_r, b_r):
    t = h.shape[0]
    d = w_r.shape[0]
    tm = MOE_ROWS
    return pl.pallas_call(
        _router_kernel,
        grid=(t // tm,),
        in_specs=[pl.BlockSpec((tm, d // 2), lambda i: (i, 0)),
                  pl.BlockSpec((d, LANES), lambda i: (0, 0)),
                  pl.BlockSpec((1, LANES), lambda i: (0, 0)),
                  pl.BlockSpec((tm, tm), lambda i: (0, 0))],
        out_specs=[pl.BlockSpec((tm, LANES), lambda i: (i, 0)),
                   pl.BlockSpec((SUBLANES, LANES), lambda i: (0, 0))],
        out_shape=[jax.ShapeDtypeStruct((t, LANES), F32), jax.ShapeDtypeStruct((SUBLANES, LANES), F32)],
        scratch_shapes=[pltpu.VMEM((SUBLANES, LANES), F32)],
        compiler_params=_params(("arbitrary",)),
        name="moe_router",
    )(h, w_r, b_r.reshape(1, LANES), _tril_ones(tm))


def _dispatch_plan(expert, rank, counts, n_blocks):
    bm = MOE_ROWS
    flat = expert.reshape(-1)
    n_assign = flat.shape[0]
    padded = (counts + bm - 1) // bm * bm
    pad_end = jnp.cumsum(padded)
    dest = (pad_end - padded)[flat] + rank.reshape(-1)
    row_token = jnp.zeros((n_blocks * bm,), jnp.int32).at[dest].set(
        jnp.arange(n_assign, dtype=jnp.int32) // MOE_TOP_K)
    block_start = jnp.arange(n_blocks, dtype=jnp.int32) * bm
    block_expert = jnp.minimum(jnp.sum((pad_end[None, :] <= block_start[:, None]).astype(jnp.int32), axis=1),
                               MOE_EXPERTS - 1)
    n_used = (pad_end[-1] // bm).reshape(1)
    return dest.astype(jnp.int32), row_token, block_expert.astype(jnp.int32), n_used.astype(jnp.int32)


def _hier_moe(x, h, lp, gt, layer, ex_gate, ex_up, ex_down, next_norm):
    t, d = x.shape
    w_r = jnp.zeros((d, LANES), F32).at[:, :MOE_GROUPS].set(lp['moe_gw'])
    w_r = w_r.at[:, MOE_GROUPS:MOE_GROUPS + MOE_EXPERTS].set(lp['moe_ew'])
    b_r = jnp.zeros((LANES,), F32).at[:MOE_GROUPS].set(lp['moe_gb'])
    b_r = b_r.at[MOE_GROUPS:MOE_GROUPS + MOE_EXPERTS].set(lp['moe_eb'])
    route, count_rows = _router(h, w_r.astype(BF16), b_r)
    weights = route[:, 0:2]
    expert = route[:, 2:4].astype(jnp.int32)
    rank = route[:, 4:6].astype(jnp.int32)
    counts = count_rows[0, :MOE_EXPERTS].astype(jnp.int32)
    n_blocks = (t * MOE_TOP_K) // MOE_ROWS + MOE_EXPERTS
    dest, row_token, block_expert, n_used = _dispatch_plan(expert, rank, counts, n_blocks)
    hmid = _moe_up(h, row_token, ex_gate, ex_up, layer, block_expert, n_used, n_blocks)
    ys = _moe_down(hmid, ex_down, layer, block_expert, n_used, n_blocks)
    return _moe_combine(x, ys, dest, weights, gt, *next_norm[:3], emit_x=next_norm[3], modulate=next_norm[3],
                        h_dtype=BF16 if next_norm[3] else F32)


def kernel(x, c, ada_w, ada_b, norm_mix, norm_ffn, w_in, rw_mu, rw_w0, rw_w2, rw_a0, rw_a2, rw_g2, rw_kk, rw_ka, rw_rk, rw_lnw, rw_lnb, rw_v0, rw_v1, rw_v2, ml_conv_w, ml_conv_b, ml_ib, ml_fb, ml_norm, hg_lb, hg_norm, p_a, p_b, p_c, w_out, moe_gw, moe_gb, moe_ew, moe_eb, ex_gate, ex_up, ex_down, final_norm):
    batch, t, d = x.shape
    assert batch == 1 and d == D_MODEL and t % 512 == 0
    depth = ada_w.shape[0]
    x = x.reshape(t, d)
    lbs = jnp.cumsum(jax.nn.softmax(hg_lb.astype(F32), axis=0), axis=0)
    lbs = lbs - lbs[:1]
    mod = _adaln_mod(c, ada_w, ada_b)
    zeros_d = jnp.zeros((d,), F32)
    c_ml = RW_COLS
    c_mg = c_ml + 2 * ML_QK_WIDTH + 2 * ML_WIDTH
    c_hg = c_mg + 2 * ML_HEADS
    c_gt = c_hg + 4 * HG_WIDTH
    v_first = None
    for l in range(depth):
        sh_m, sc_m, gt_m, sh_f, sc_f, gt_f = jnp.split(mod[l], 6)
        lp = dict(rw_mu=rw_mu[l], rw_w0=rw_w0[l], rw_w2=rw_w2[l], rw_a0=rw_a0[l], rw_a2=rw_a2[l], rw_g2=rw_g2[l],
                  rw_kk=rw_kk[l], rw_ka=rw_ka[l], rw_rk=rw_rk[l], rw_lnw=rw_lnw[l], rw_lnb=rw_lnb[l],
                  ml_conv_w=ml_conv_w[l], ml_conv_b=ml_conv_b[l], ml_ib=ml_ib[l], ml_fb=ml_fb[l],
                  ml_norm=ml_norm[l], hg_norm=hg_norm[l],
                  moe_gw=moe_gw[l], moe_gb=moe_gb[l], moe_ew=moe_ew[l], moe_eb=moe_eb[l])
        if l > 0:
            lp.update(rw_v0=rw_v0[l - 1], rw_v1=rw_v1[l - 1], rw_v2=rw_v2[l - 1])
        if l == 0:
            h = _norm_mod(x, norm_mix[l], sc_m, sh_m)
        wl = w_in[l]
        u_rw = _mm(h, wl[:, :c_ml].astype(BF16), jnp.zeros((c_ml,), F32), out_dtype=BF16, tn=c_ml)
        u_ml = _mm(h, wl[:, c_ml:c_mg].astype(BF16), jnp.zeros((c_mg - c_ml,), F32), out_dtype=BF16, tn=ML_WIDTH)
        w_mg = jnp.zeros((d, LANES), F32).at[:, :2 * ML_HEADS].set(wl[:, c_mg:c_hg])
        u_mg = _mm(h, w_mg.astype(BF16), jnp.zeros((LANES,), F32), out_dtype=F32, tn=LANES)
        u_hg = _mm(h, wl[:, c_hg:c_gt].astype(BF16), jnp.zeros((c_gt - c_hg,), F32), out_dtype=BF16, tn=ML_WIDTH)
        y_a, v_first = _rwkv_branch(u_rw, v_first, lp)
        y_b = _mlstm_branch(u_ml, u_mg, lp)
        y_c = _hgrn_branch(u_hg, lbs[l], lp)
        x, h = _merge_out(h, y_a, y_b, y_c, wl[:, c_gt:].astype(BF16), p_a[l].astype(BF16), p_b[l].astype(BF16),
                          p_c[l].astype(BF16), w_out[l].astype(BF16), x, gt_m, norm_ffn[l], sc_f, sh_f)
        if l + 1 < depth:
            nxt = jnp.split(mod[l + 1], 6)
            next_norm = (norm_mix[l + 1], nxt[1], nxt[0], True)
        else:
            next_norm = (final_norm, zeros_d, zeros_d, False)
        x, h = _hier_moe(x, h, lp, gt_f, l, ex_gate, ex_up, ex_down, next_norm)
    return h.reshape(batch, t, d)
```

```python
import functools

import numpy as np
import jax
import jax.numpy as jnp
from jax import lax
from jax.experimental import pallas as pl
from jax.experimental.pallas import tpu as pltpu

F32 = jnp.float32
BF16 = jnp.bfloat16

D_MODEL = 2048
DEPTH = 2
RW_HEADS, RW_HEAD_DIM, RW_WIDTH = 8, 64, 512
RW_LORA = 256
RW_COLS = 3 * RW_WIDTH + RW_LORA
RW_GN_EPS = 64e-5
RW_GROUP = 4
RW_CHUNKS_PER_STEP = 4
ML_HEADS, ML_QK_DIM, ML_V_DIM = 4, 128, 256
ML_QK_WIDTH, ML_WIDTH = 512, 1024
ML_CONV = 4
ML_SOFTCAP = 15.0
ML_CHUNKS_PER_STEP = 2
HG_HEADS, HG_DIM, HG_WIDTH = 4, 128, 512
HG_CHUNK = 128
CHUNK = 64
MOE_GROUPS, MOE_EPG, MOE_EXPERTS, MOE_TOP_K, MOE_D_FF = 4, 8, 32, 2, 1024
NORM_EPS = 1e-6

LANES = 128
SUBLANES = 8
MOE_ROWS = 256
DMA_PRIORITIES = 2
VMEM_LIMIT = 56 * 1024 * 1024


def _params(sem, limit=VMEM_LIMIT):
    return pltpu.CompilerParams(dimension_semantics=sem, vmem_limit_bytes=limit)


def _bf(x):
    return x.astype(BF16)


def _dot(a, b):
    return jnp.dot(_bf(a), _bf(b), preferred_element_type=F32)


def _dot_nt(a, b):
    return lax.dot_general(_bf(a), _bf(b), (((1,), (1,)), ((), ())), preferred_element_type=F32)


def _dot_tn(a, b):
    return lax.dot_general(_bf(a), _bf(b), (((0,), (0,)), ((), ())), preferred_element_type=F32)


def _bf16_pieces(x, pieces):
    out = []
    rest = x
    for _ in range(pieces):
        part = rest.astype(BF16)
        out.append(part)
        rest = rest - part.astype(F32)
    return out


def _sel_dot(sel, x, pieces):
    acc = None
    for part in _bf16_pieces(x, pieces):
        term = jnp.dot(sel, part, preferred_element_type=F32)
        acc = term if acc is None else acc + term
    return acc


def _seg_sum(x, bd):
    acc = None
    for part in _bf16_pieces(x, 2):
        term = jnp.dot(part, bd, preferred_element_type=F32)
        acc = term if acc is None else acc + term
    return acc


def _sigmoid(x):
    return jax.nn.sigmoid(x)


def _silu(x):
    return x * jax.nn.sigmoid(x)


def _log_sigmoid(x):
    return jnp.minimum(x, 0.0) - jnp.log1p(jnp.exp(-jnp.abs(x)))


def _softplus(x):
    return jnp.maximum(x, 0.0) + jnp.log1p(jnp.exp(-jnp.abs(x)))


def _mod_kernel(c_ref, w_ref, b_ref, o_ref):
    cond = _silu(c_ref[...])
    o_ref[0] = jnp.sum(w_ref[0] * cond, axis=0, keepdims=True) + b_ref[0]


def _adaln_mod(c, ada_w, ada_b):
    depth, d, n = ada_w.shape
    tn = 1024
    out = pl.pallas_call(
        _mod_kernel,
        grid=(depth, n // tn),
        in_specs=[pl.BlockSpec((d, 1), lambda l, j: (0, 0)),
                  pl.BlockSpec((1, d, tn), lambda l, j: (l, 0, j)),
                  pl.BlockSpec((1, 1, tn), lambda l, j: (l, 0, j))],
        out_specs=pl.BlockSpec((1, 1, tn), lambda l, j: (l, 0, j)),
        out_shape=jax.ShapeDtypeStruct((depth, 1, n), F32),
        compiler_params=_params(("arbitrary", "arbitrary")),
        name="adaln_mod",
    )(c.reshape(d, 1), ada_w, ada_b.reshape(depth, 1, n))
    return out[:, 0, :]


HI_HALF = -65536


def _pack_pairs(y):
    n = y.shape[1] // 2
    bits = lax.bitcast_convert_type(y.astype(BF16).astype(F32), jnp.int32)
    return lax.shift_right_logical(bits[:, :n], 16) | (bits[:, n:] & HI_HALF)


def _unpack_pairs(p):
    lo = lax.bitcast_convert_type(lax.shift_left(p, 16), F32)
    hi = lax.bitcast_convert_type(p & HI_HALF, F32)
    return jnp.concatenate([lo, hi], axis=1)


def _rms_mod(x, g_ref, sc_ref, sh_ref, modulate):
    y = x * lax.rsqrt(jnp.mean(x * x, axis=-1, keepdims=True) + NORM_EPS) * g_ref[...]
    if modulate:
        y = y * (1.0 + sc_ref[...]) + sh_ref[...]
    return y


def _norm_kernel(x_ref, g_ref, sc_ref, sh_ref, o_ref):
    o_ref[...] = _rms_mod(x_ref[...], g_ref, sc_ref, sh_ref, True).astype(o_ref.dtype)


def _norm_mod(x, g, sc, sh):
    t, d = x.shape
    tm = min(512, t)
    row = pl.BlockSpec((1, d), lambda i: (0, 0))
    return pl.pallas_call(
        _norm_kernel,
        grid=(t // tm,),
        in_specs=[pl.BlockSpec((tm, d), lambda i: (i, 0)), row, row, row],
        out_specs=pl.BlockSpec((tm, d), lambda i: (i, 0)),
        out_shape=jax.ShapeDtypeStruct((t, d), BF16),
        compiler_params=_params(("arbitrary",)),
        name="rmsnorm_mod",
    )(x, g.reshape(1, d), sc.reshape(1, d), sh.reshape(1, d))


def _mm_kernel(a_ref, w_ref, b_ref, o_ref):
    o_ref[...] = (_dot(a_ref[...], w_ref[...]) + b_ref[...]).astype(o_ref.dtype)


def _row_tile(m):
    return 1024 if m % 1024 == 0 else min(512, m)


def _mm(a, w, bias, *, out_dtype, tn):
    m, k = a.shape
    n = w.shape[1]
    tm = _row_tile(m)
    return pl.pallas_call(
        _mm_kernel,
        grid=(n // tn, m // tm),
        in_specs=[pl.BlockSpec((tm, k), lambda j, i: (i, 0)),
                  pl.BlockSpec((k, tn), lambda j, i: (0, j)),
                  pl.BlockSpec((1, tn), lambda j, i: (0, j))],
        out_specs=pl.BlockSpec((tm, tn), lambda j, i: (i, j)),
        out_shape=jax.ShapeDtypeStruct((m, n), out_dtype),
        compiler_params=_params(("arbitrary", "arbitrary")),
        name="dense_matmul",
    )(a, w, bias.reshape(1, n))


def _rwkv_kernel(*refs, has_vres):
    if has_vres:
        (u_ref, vf_ref, mu_ref, wl_ref, w0_ref, a0_ref, kk_ref, ka_ref, rk_ref, lnw_ref, lnb_ref,
         v0_ref, v1_ref, v2_ref, bd_ref, tril_ref, y_ref, xbuf, st_ref) = refs
    else:
        (u_ref, mu_ref, wl_ref, w0_ref, a0_ref, kk_ref, ka_ref, rk_ref, lnw_ref, lnb_ref,
         bd_ref, tril_ref, y_ref, vf_out_ref, xbuf, st_ref) = refs
    L = CHUNK
    W = RW_WIDTH
    TB = u_ref.shape[0]
    n_chunks = TB // L

    @pl.when(pl.program_id(0) == 0)
    def _():
        xbuf[0:SUBLANES, :] = jnp.zeros((SUBLANES, RW_COLS), F32)
        st_ref[...] = jnp.zeros(st_ref.shape, F32)

    u = u_ref[...].astype(F32)
    xbuf[SUBLANES:SUBLANES + TB, :] = u
    prev = xbuf[SUBLANES - 1:SUBLANES - 1 + TB, :]
    xbuf[0:SUBLANES, :] = u[TB - SUBLANES:TB, :]
    xs = u + mu_ref[...] * (prev - u)
    r = xs[:, 0:W]
    k = xs[:, W:2 * W]
    v = xs[:, 2 * W:3 * W]
    lr = xs[:, 3 * W:3 * W + RW_LORA]
    lane = lax.broadcasted_iota(jnp.int32, lr.shape, 1)
    act = jnp.where(lane < 64, jnp.tanh(lr), jnp.where(lane < 128, lr, _sigmoid(lr)))
    lo = _dot(act, wl_ref[...])
    z = w0_ref[...] + lo[:, 0:W]
    log_w = -_softplus(-z) - 0.5
    ld = -jnp.exp(log_w)
    iclr = _sigmoid(a0_ref[...] + lo[:, W:2 * W])
    gate = lo[:, 2 * W:3 * W]
    if has_vres:
        vv = _dot(_dot(v, v1_ref[...]), v2_ref[...])
        v = v + (vf_ref[...] - v) * _sigmoid(v0_ref[...] + vv)
    else:
        vf_out_ref[...] = v
    bd = bd_ref[...]
    kk = k * kk_ref[...]
    k = k * (1.0 + (iclr - 1.0) * ka_ref[...])
    sums = _seg_sum(jnp.concatenate([kk * kk, r * k * rk_ref[...]], axis=0), bd)
    kk = kk / jnp.maximum(jnp.sqrt(sums[0:TB, :]), 1e-12)
    a = -kk
    b = kk * iclr
    bonus = sums[TB:2 * TB, :] * v

    cum = _sel_dot(tril_ref[...], ld, 2)
    cl_rows = [cum[(c + 1) * L - 1:(c + 1) * L, :] for c in range(n_chunks)]
    cl = jnp.concatenate([jnp.broadcast_to(x, (L, W)) for x in cl_rows], axis=0)
    e_neg = jnp.exp(-cum)
    e_last = jnp.exp(cl - cum)
    a_t = a * jnp.exp(cum - ld)
    r_t = r * jnp.exp(cum)
    b_t = b * e_neg
    k_t = k * e_neg
    b_h = b * e_last
    k_h = k * e_last

    G = RW_GROUP
    S = G * L
    row = lax.broadcasted_iota(jnp.int32, (S, S), 0)
    col = lax.broadcasted_iota(jnp.int32, (S, S), 1)
    same = (row // L) == (col // L)
    strict = same & (row > col)
    incl = same & (row >= col)
    eye = (row == col).astype(F32)
    lane_head = lax.broadcasted_iota(jnp.int32, (1, S), 1) // RW_HEAD_DIM
    head_masks = [(lane_head == j).astype(F32) for j in range(G)]

    n_groups = RW_HEADS // G
    units = [(c, g) for c in range(n_chunks) for g in range(n_groups)]

    def unit_rows(x, c, g):
        return x[c * L:(c + 1) * L, g * S:(g + 1) * S]

    def stack(x, c, g):
        xs_ = unit_rows(x, c, g)
        return jnp.concatenate([xs_ * m for m in head_masks], axis=0)

    def tile(x, c, g):
        return jnp.concatenate([unit_rows(x, c, g)] * G, axis=0)

    a_s = {u: stack(a_t, *u) for u in units}
    r_s = {u: stack(r_t, *u) for u in units}
    v_s = {u: stack(v, *u) for u in units}
    bk_h = {u: jnp.concatenate([stack(b_h, *u), stack(k_h, *u)], axis=0) for u in units}
    xm = {u: _dot_nt(jnp.concatenate([a_s[u], r_s[u]], axis=0),
                     jnp.concatenate([tile(b_t, *u), tile(k_t, *u)], axis=0)) for u in units}
    n_ab = {u: jnp.where(strict, xm[u][0:S, 0:S], 0.0) for u in units}
    n_ak = {u: jnp.where(strict, xm[u][0:S, S:2 * S], 0.0) for u in units}
    r_bk = {u: jnp.concatenate([jnp.where(incl, xm[u][S:2 * S, 0:S], 0.0),
                                jnp.where(incl, xm[u][S:2 * S, S:2 * S], 0.0)], axis=1) for u in units}
    tinv = {u: eye + n_ab[u] for u in units}
    pw = dict(n_ab)
    for _ in range(5):
        pw = {u: _dot(pw[u], pw[u]) for u in units}
        tinv = {u: tinv[u] + _dot(tinv[u], pw[u]) for u in units}
    nv = {u: _dot(n_ak[u], v_s[u]) for u in units}
    pq = {u: _dot(tinv[u], jnp.concatenate([a_s[u], nv[u]], axis=1)) for u in units}
    p_s = {u: pq[u][:, 0:S] for u in units}
    qv = {u: jnp.concatenate([pq[u][:, S:2 * S], v_s[u]], axis=0) for u in units}
    y1 = {u: r_s[u] + _dot(r_bk[u][:, 0:S], p_s[u]) for u in units}
    y0 = {u: _dot(r_bk[u], qv[u]) for u in units}
    gt = {u: _dot_tn(p_s[u], bk_h[u][0:S, :]) for u in units}
    ht = {u: _dot_tn(qv[u], bk_h[u]) for u in units}

    state = [st_ref[g] for g in range(n_groups)]
    y_rows = []
    for c in range(n_chunks):
        w_last = jnp.exp(cl_rows[c])
        ys = []
        for g in range(n_groups):
            u = (c, g)
            st = state[g]
            y_s = _dot_nt(y1[u], st) + y0[u]
            state[g] = st * w_last[:, g * S:(g + 1) * S] + _dot(st, gt[u]) + ht[u]
            y_g = y_s[0:L, :]
            for j in range(1, G):
                y_g = y_g + y_s[j * L:(j + 1) * L, :]
            ys.append(y_g)
        y_rows.append(jnp.concatenate(ys, axis=1))
    for g in range(n_groups):
        st_ref[g] = state[g]
    y = jnp.concatenate(y_rows, axis=0)

    inv_n = 1.0 / RW_HEAD_DIM
    mean = _seg_sum(y, bd) * inv_n
    dlt = y - mean
    var = _seg_sum(dlt * dlt, bd) * inv_n
    yn = dlt * lax.rsqrt(var + RW_GN_EPS) * lnw_ref[...] + lnb_ref[...]
    y_ref[...] = ((yn + bonus) * gate).astype(y_ref.dtype)


def _blockdiag_ones(width, head):
    idx = np.arange(width) // head
    return jnp.asarray((idx[:, None] == idx[None, :]).astype(np.float32), BF16)


def _tril_ones(n, blocks=1):
    return jnp.asarray(np.kron(np.eye(blocks, dtype=np.float32), np.tril(np.ones((n, n), np.float32))), BF16)


def _rwkv_branch(u_rw, v_first, lp):
    t = u_rw.shape[0]
    L = CHUNK
    has_vres = v_first is not None
    W = RW_WIDTH
    vec = lambda a: a.reshape(1, -1).astype(F32)
    full = lambda shape: pl.BlockSpec(shape, lambda i: (0,) * len(shape))
    tb = RW_CHUNKS_PER_STEP * L
    rows = lambda width: pl.BlockSpec((tb, width), lambda i: (i, 0))
    wl = jnp.zeros((RW_LORA, 3 * W), F32)
    wl = wl.at[0:64, 0:W].set(lp['rw_w2']).at[64:128, W:2 * W].set(lp['rw_a2']).at[128:256, 2 * W:].set(lp['rw_g2'])
    ins = [u_rw]
    specs = [rows(RW_COLS)]
    if has_vres:
        ins.append(v_first)
        specs.append(rows(W))
    ins += [vec(lp['rw_mu']), wl.astype(BF16), vec(lp['rw_w0']), vec(lp['rw_a0']), vec(lp['rw_kk']),
            vec(lp['rw_ka']), vec(lp['rw_rk']), vec(lp['rw_lnw']), vec(lp['rw_lnb'])]
    specs += [full((1, RW_COLS)), full((RW_LORA, 3 * W))] + [full((1, W))] * 7
    if has_vres:
        v1 = jnp.zeros((W, LANES), F32).at[:, :lp['rw_v1'].shape[1]].set(lp['rw_v1'])
        v2 = jnp.zeros((LANES, W), F32).at[:lp['rw_v2'].shape[0], :].set(lp['rw_v2'])
        ins += [vec(lp['rw_v0']), v1.astype(BF16), v2.astype(BF16)]
        specs += [full((1, W)), full((W, LANES)), full((LANES, W))]
    ins += [_blockdiag_ones(W, RW_HEAD_DIM), _tril_ones(L, RW_CHUNKS_PER_STEP)]
    specs += [full((W, W)), full((tb, tb))]
    out_shape = [jax.ShapeDtypeStruct((t, W), BF16)]
    out_specs = [rows(W)]
    if not has_vres:
        out_shape.append(jax.ShapeDtypeStruct((t, W), F32))
        out_specs.append(rows(W))
    outs = pl.pallas_call(
        functools.partial(_rwkv_kernel, has_vres=has_vres),
        grid=(t // tb,),
        in_specs=specs,
        out_specs=out_specs,
        out_shape=out_shape,
        scratch_shapes=[pltpu.VMEM((tb + SUBLANES, RW_COLS), F32),
                        pltpu.VMEM((RW_HEADS // RW_GROUP, RW_GROUP * RW_HEAD_DIM, RW_GROUP * RW_HEAD_DIM), F32)],
        compiler_params=_params(("arbitrary",)),
        name="rwkv7_branch",
    )(*ins)
    if has_vres:
        return outs[0], v_first
    return outs[0], outs[1]


def _mlstm_kernel(qk_ref, v_ref, o_ref, g_ref, cw_ref, cb_ref, gb_ref, ng_ref, tril_ref,
                  y_ref, xbuf, c_ref, m_ref):
    L = CHUNK
    TB = qk_ref.shape[0]
    n_chunks = TB // L

    @pl.when(pl.program_id(0) == 0)
    def _():
        xbuf[0:SUBLANES, :] = jnp.zeros((SUBLANES, 2 * ML_QK_WIDTH), F32)
        c_ref[...] = jnp.zeros(c_ref.shape, F32)
        m_ref[...] = jnp.zeros(m_ref.shape, F32)

    x0 = qk_ref[...].astype(F32)
    xbuf[SUBLANES:SUBLANES + TB, :] = x0
    conv = cb_ref[...] + cw_ref[ML_CONV - 1:ML_CONV, :] * x0
    for dly in range(1, ML_CONV):
        conv = conv + cw_ref[ML_CONV - 1 - dly:ML_CONV - dly, :] * xbuf[SUBLANES - dly:SUBLANES - dly + TB, :]
    xbuf[0:SUBLANES, :] = x0[TB - SUBLANES:TB, :]
    qk = _silu(conv)
    q = qk[:, :ML_QK_WIDTH]
    k = qk[:, ML_QK_WIDTH:] * (ML_QK_DIM ** -0.5)

    pre = g_ref[...] + gb_ref[...]
    cap = ML_SOFTCAP * jnp.tanh(pre / ML_SOFTCAP)
    lane = lax.broadcasted_iota(jnp.int32, pre.shape, 1)
    gates = jnp.where(lane < ML_HEADS, cap, _log_sigmoid(cap))
    csum = _sel_dot(tril_ref[...], gates, 3)
    gates_t = gates.T
    csum_t = csum.T

    row = lax.broadcasted_iota(jnp.int32, (L, L), 0)
    col = lax.broadcasted_iota(jnp.int32, (L, L), 1)
    causal = row >= col
    ones_col = (lax.broadcasted_iota(jnp.int32, (L, LANES), 1) == 0).astype(F32)
    vv = v_ref[...].astype(F32)
    og = o_ref[...].astype(F32)
    units = [(c, h) for c in range(n_chunks) for h in range(ML_HEADS)]

    def rows(c):
        return slice(c * L, (c + 1) * L)

    qh = {(c, h): q[rows(c), h * ML_QK_DIM:(h + 1) * ML_QK_DIM] for c, h in units}
    kh = {(c, h): k[rows(c), h * ML_QK_DIM:(h + 1) * ML_QK_DIM] for c, h in units}
    vh = {(c, h): jnp.concatenate([vv[rows(c), h * ML_V_DIM:(h + 1) * ML_V_DIM], ones_col], axis=1)
          for c, h in units}
    b_col = {(c, h): csum[rows(c), ML_HEADS + h:ML_HEADS + h + 1] for c, h in units}
    li_col = {(c, h): gates[rows(c), h:h + 1] for c, h in units}
    dmat = {(c, h): jnp.where(causal, b_col[c, h] - csum_t[ML_HEADS + h:ML_HEADS + h + 1, rows(c)]
                              + gates_t[h:h + 1, rows(c)], -jnp.inf) for c, h in units}
    dmax = {u: jnp.max(dmat[u], axis=-1, keepdims=True) for u in units}
    s_loc = {u: _dot_nt(qh[u], kh[u]) * jnp.exp(dmat[u] - dmax[u]) for u in units}
    sv = {u: _dot(s_loc[u], vh[u]) for u in units}
    b_last = {u: b_col[u][L - 1:L, :] for u in units}
    g = {u: b_last[u] - b_col[u] + li_col[u] for u in units}
    gmax = {u: jnp.max(g[u], axis=0, keepdims=True) for u in units}
    kv = {u: _dot_tn(jnp.exp(g[u] - gmax[u]) * kh[u], vh[u]) for u in units}

    c_state = [c_ref[h] for h in range(ML_HEADS)]
    m_state = [m_ref[h][0:1, 0:1] for h in range(ML_HEADS)]
    out_rows = []
    for c in range(n_chunks):
        outs = []
        for h in range(ML_HEADS):
            u = (c, h)
            m_prev = m_state[h]
            m_inter = b_col[u] + m_prev
            m_t = jnp.maximum(m_inter, dmax[u])
            num_aug = jnp.exp(dmax[u] - m_t) * sv[u] + jnp.exp(m_inter - m_t) * _dot(qh[u], c_state[h])
            num = num_aug[:, :ML_V_DIM]
            den = num_aug[:, ML_V_DIM:ML_V_DIM + 1]
            hh = num / jnp.maximum(jnp.abs(den), jnp.exp(-m_t))
            m_new = jnp.maximum(b_last[u] + m_prev, gmax[u])
            c_state[h] = jnp.exp(b_last[u] + m_prev - m_new) * c_state[h] + jnp.exp(gmax[u] - m_new) * kv[u]
            m_state[h] = m_new
            ng = ng_ref[:, h * ML_V_DIM:(h + 1) * ML_V_DIM]
            hn = hh * lax.rsqrt(jnp.mean(hh * hh, axis=-1, keepdims=True) + NORM_EPS) * ng
            outs.append(hn * _sigmoid(og[rows(c), h * ML_V_DIM:(h + 1) * ML_V_DIM]))
        out_rows.append(jnp.concatenate(outs, axis=1))
    for h in range(ML_HEADS):
        c_ref[h] = c_state[h]
        m_ref[h] = jnp.broadcast_to(m_state[h], (SUBLANES, LANES))
    y_ref[...] = jnp.concatenate(out_rows, axis=0).astype(y_ref.dtype)


def _mlstm_branch(u_ml, u_mg, lp):
    t = u_ml.shape[0]
    L = CHUNK
    full = lambda shape: pl.BlockSpec(shape, lambda i: (0,) * len(shape))
    gb = jnp.zeros((1, LANES), F32).at[0, 0:ML_HEADS].set(lp['ml_ib']).at[0, ML_HEADS:2 * ML_HEADS].set(lp['ml_fb'])
    tb = ML_CHUNKS_PER_STEP * L
    return pl.pallas_call(
        _mlstm_kernel,
        grid=(t // tb,),
        in_specs=[pl.BlockSpec((tb, ML_WIDTH), lambda i: (i, 0)),
                  pl.BlockSpec((tb, ML_WIDTH), lambda i: (i, 1)),
                  pl.BlockSpec((tb, ML_WIDTH), lambda i: (i, 2)),
                  pl.BlockSpec((tb, LANES), lambda i: (i, 0)),
                  full((ML_CONV, 2 * ML_QK_WIDTH)), full((1, 2 * ML_QK_WIDTH)), full((1, LANES)),
                  full((1, ML_WIDTH)), full((tb, tb))],
        out_specs=pl.BlockSpec((tb, ML_WIDTH), lambda i: (i, 0)),
        out_shape=jax.ShapeDtypeStruct((t, ML_WIDTH), BF16),
        scratch_shapes=[pltpu.VMEM((tb + SUBLANES, 2 * ML_QK_WIDTH), F32),
                        pltpu.VMEM((ML_HEADS, ML_QK_DIM, ML_V_DIM + LANES), F32),
                        pltpu.VMEM((ML_HEADS, SUBLANES, LANES), F32)],
        compiler_params=_params(("arbitrary",)),
        name="mlstm_branch",
    )(u_ml, u_ml, u_ml, u_mg, lp['ml_conv_w'], lp['ml_conv_b'].reshape(1, -1), gb,
      lp['ml_norm'].reshape(1, -1), _tril_ones(L, ML_CHUNKS_PER_STEP))


def _hgrn_levels():
    L = HG_CHUNK
    t = np.arange(L)
    sel = [np.tril(np.ones((L, L), np.float32))]
    masks = []
    size = L
    while size >= 2:
        half = size // 2
        ref_row = (t // size) * size + half - 1
        sel.append((t[None, :] <= ref_row[:, None]).astype(np.float32))
        same = (t[:, None] // size) == (t[None, :] // size)
        masks.append((same & ((t[:, None] % size) >= half) & ((t[None, :] % size) < half)).astype(np.float32))
        size = half
    return np.concatenate(sel, axis=0), np.stack(masks)


def _hgrn_kernel(u_ref, la_ref, lc_ref, lb_ref, ng_ref, sel_ref, mask_ref, y_ref, s_ref, *, levels):
    L = HG_CHUNK
    W = HG_WIDTH

    @pl.when(pl.program_id(0) == 0)
    def _():
        s_ref[...] = jnp.zeros(s_ref.shape, F32)

    u = u_ref[...].astype(F32)
    q = _silu(u[:, 0:W])
    f_pre = u[:, W:2 * W]
    ii = u[:, 2 * W:3 * W]
    g_pre = u[:, 3 * W:4 * W]
    la = la_ref[...]
    lc = lc_ref[...] + _log_sigmoid(f_pre)
    log_f = jnp.maximum(la, lc) + jnp.log1p(jnp.exp(-jnp.abs(la - lc)))
    k = (1.0 - lb_ref[...]) * _sigmoid(-f_pre)

    cr = _sel_dot(sel_ref[...], log_f, 2)
    cum = cr[0:L, :]
    cl = cum[L - 1:L, :]
    q_in = q * jnp.exp(cum)
    k_out = k * jnp.exp(cl - cum)
    w_last = jnp.exp(cl)
    heads = range(HG_HEADS)
    sl = [slice(h * HG_DIM, (h + 1) * HG_DIM) for h in heads]
    att = [jnp.zeros((L, L), F32) for _ in heads]
    for lv in range(levels):
        ref = cr[(lv + 1) * L:(lv + 2) * L, :]
        qe = q * jnp.exp(jnp.minimum(cum - ref, 0.0))
        ke = k * jnp.exp(jnp.minimum(ref - cum, 0.0))
        mask = mask_ref[lv]
        att = [att[h] + mask * _dot_nt(qe[:, sl[h]], ke[:, sl[h]]) for h in heads]
    diag = [jnp.sum(q[:, sl[h]] * k[:, sl[h]], axis=-1, keepdims=True) for h in heads]
    intra = [_dot(att[h], ii[:, sl[h]]) + diag[h] * ii[:, sl[h]] for h in heads]
    kv = [_dot_tn(ii[:, sl[h]], k_out[:, sl[h]]) for h in heads]
    outs = []
    for h in heads:
        st = s_ref[h]
        o = intra[h] + _dot_nt(q_in[:, sl[h]], st)
        s_ref[h] = st * w_last[:, sl[h]] + kv[h]
        on = o * lax.rsqrt(jnp.mean(o * o, axis=-1, keepdims=True) + NORM_EPS) * ng_ref[:, sl[h]]
        outs.append(on * _silu(g_pre[:, sl[h]]))
    y_ref[...] = jnp.concatenate(outs, axis=1).astype(y_ref.dtype)


def _hgrn_branch(u_hg, lb, lp):
    t = u_hg.shape[0]
    L = HG_CHUNK
    sel, masks = _hgrn_levels()
    levels = masks.shape[0]
    full = lambda shape: pl.BlockSpec(shape, lambda i: (0,) * len(shape))
    lb = lb.reshape(1, -1).astype(F32)
    return pl.pallas_call(
        functools.partial(_hgrn_kernel, levels=levels),
        grid=(t // L,),
        in_specs=[pl.BlockSpec((L, 4 * HG_WIDTH), lambda i: (i, 0)),
                  full((1, HG_WIDTH)), full((1, HG_WIDTH)), full((1, HG_WIDTH)), full((1, HG_WIDTH)),
                  full(sel.shape), full(masks.shape)],
        out_specs=pl.BlockSpec((L, HG_WIDTH), lambda i: (i, 0)),
        out_shape=jax.ShapeDtypeStruct((t, HG_WIDTH), BF16),
        scratch_shapes=[pltpu.VMEM((HG_HEADS, HG_DIM, HG_DIM), F32)],
        compiler_params=_params(("arbitrary",)),
        name="hgrn2_branch",
    )(u_hg, jnp.log(lb), jnp.log1p(-lb), lb, lp['hg_norm'].reshape(1, -1),
      jnp.asarray(sel, BF16), jnp.asarray(masks, F32))


def _gate_merge_kernel(h_ref, ya_ref, yb_ref, yc_ref, wga_ref, wgb_ref, wgc_ref, pa_ref, pb_ref, pc_ref, o_ref):
    h = h_ref[...]
    merged = (_sigmoid(_dot(h, wga_ref[...])) * _dot(ya_ref[...], pa_ref[...])
              + _sigmoid(_dot(h, wgb_ref[...])) * _dot(yb_ref[...], pb_ref[...])
              + _sigmoid(_dot(h, wgc_ref[...])) * _dot(yc_ref[...], pc_ref[...]))
    o_ref[...] = merged.astype(o_ref.dtype)


def _residual_proj_kernel(a_ref, w_ref, x_ref, gt_ref, g_ref, sc_ref, sh_ref, wr_ref, br_ref, tril_ref,
                          xo_ref, h_ref, route_ref, count_ref, run_ref):
    @pl.when(pl.program_id(0) == 0)
    def _():
        run_ref[...] = jnp.zeros(run_ref.shape, F32)

    x_new = x_ref[...] + gt_ref[...] * _dot(a_ref[...], w_ref[...])
    xo_ref[...] = x_new
    h = _rms_mod(x_new, g_ref, sc_ref, sh_ref, True)
    h_ref[...] = _pack_pairs(h)
    route, counts = _route_rows(_dot(h, wr_ref[...]) + br_ref[...], tril_ref, run_ref)
    route_ref[...] = route
    count_ref[...] = jnp.broadcast_to(counts, count_ref.shape)


def _merge_out(h, ya, yb, yc, w_gates, pa, pb, pc, wo, x, gt, g_ffn, sc_ffn, sh_ffn, w_r, b_r):
    t, d = x.shape
    tm = _row_tile(t)
    tn = 512
    nb = d // tn
    merged = pl.pallas_call(
        _gate_merge_kernel,
        grid=(nb, t // tm),
        in_specs=[pl.BlockSpec((tm, d), lambda n, i: (i, 0)),
                  pl.BlockSpec((tm, RW_WIDTH), lambda n, i: (i, 0)),
                  pl.BlockSpec((tm, ML_WIDTH), lambda n, i: (i, 0)),
                  pl.BlockSpec((tm, HG_WIDTH), lambda n, i: (i, 0)),
                  pl.BlockSpec((d, tn), lambda n, i: (0, n)),
                  pl.BlockSpec((d, tn), lambda n, i: (0, nb + n)),
                  pl.BlockSpec((d, tn), lambda n, i: (0, 2 * nb + n)),
                  pl.BlockSpec((RW_WIDTH, tn), lambda n, i: (0, n)),
                  pl.BlockSpec((ML_WIDTH, tn), lambda n, i: (0, n)),
                  pl.BlockSpec((HG_WIDTH, tn), lambda n, i: (0, n))],
        out_specs=pl.BlockSpec((tm, tn), lambda n, i: (i, n)),
        out_shape=jax.ShapeDtypeStruct((t, d), BF16),
        compiler_params=_params(("arbitrary", "arbitrary")),
        name="gate_merge",
    )(h, ya, yb, yc, w_gates, w_gates, w_gates, pa, pb, pc)
    tm2 = min(512, t)
    row = pl.BlockSpec((1, d), lambda i: (0, 0))
    return pl.pallas_call(
        _residual_proj_kernel,
        grid=(t // tm2,),
        in_specs=[pl.BlockSpec((tm2, d), lambda i: (i, 0)),
                  pl.BlockSpec((d, d), lambda i: (0, 0)),
                  pl.BlockSpec((tm2, d), lambda i: (i, 0)),
                  row, row, row, row,
                  pl.BlockSpec((d, LANES), lambda i: (0, 0)),
                  pl.BlockSpec((1, LANES), lambda i: (0, 0)),
                  pl.BlockSpec((tm2, tm2), lambda i: (0, 0))],
        out_specs=[pl.BlockSpec((tm2, d), lambda i: (i, 0)),
                   pl.BlockSpec((tm2, d // 2), lambda i: (i, 0)),
                   pl.BlockSpec((tm2, LANES), lambda i: (i, 0)),
                   pl.BlockSpec((SUBLANES, LANES), lambda i: (0, 0))],
        out_shape=[jax.ShapeDtypeStruct((t, d), F32), jax.ShapeDtypeStruct((t, d // 2), jnp.int32),
                   jax.ShapeDtypeStruct((t, LANES), F32), jax.ShapeDtypeStruct((SUBLANES, LANES), F32)],
        scratch_shapes=[pltpu.VMEM((SUBLANES, LANES), F32)],
        compiler_params=_params(("arbitrary",)),
        name="residual_proj",
    )(merged, wo, x, gt.reshape(1, d), g_ffn.reshape(1, d), sc_ffn.reshape(1, d), sh_ffn.reshape(1, d),
      w_r, b_r.reshape(1, LANES), _tril_ones(tm2))


def _moe_up_kernel(be_ref, nu_ref, tok_ref, nxt_ref, h_ref, wg_ref, wu_ref, o_ref, xbuf, wg_bf, wu_bf, sem):
    i = pl.program_id(0)
    n_used = nu_ref[0]
    rows = xbuf.shape[1]
    slot = lax.rem(i, 2)

    def row_copy(s, r, src_row):
        return pltpu.make_async_copy(h_ref.at[pl.ds(src_row, 1)], xbuf.at[s, pl.ds(r, 1)], sem.at[s])

    def start_block(s, ids_ref):
        for r in range(rows):
            row_copy(s, r, ids_ref[0, 0, r]).start(priority=r % DMA_PRIORITIES)

    def wait_block(s):
        def body(r, c):
            row_copy(s, r, 0).wait()
            return c

        lax.fori_loop(0, rows, body, 0, unroll=8)

    @pl.when((i == 0) & (n_used > 0))
    def _():
        start_block(0, tok_ref)

    @pl.when(i + 1 < n_used)
    def _():
        start_block(1 - slot, nxt_ref)

    prev = be_ref[jnp.maximum(i - 1, 0)]

    @pl.when((i == 0) | (be_ref[i] != prev))
    def _():
        wg_bf[...] = wg_ref[0, 0].astype(BF16)
        wu_bf[...] = wu_ref[0, 0].astype(BF16)

    @pl.when(i < n_used)
    def _():
        wait_block(slot)
        x = _bf(_unpack_pairs(xbuf[slot]))
        g = jnp.dot(x, wg_bf[...], preferred_element_type=F32)
        u = jnp.dot(x, wu_bf[...], preferred_element_type=F32)
        o_ref[...] = (_silu(g) * u).astype(o_ref.dtype)

    @pl.when(i >= n_used)
    def _():
        o_ref[...] = jnp.zeros(o_ref.shape, o_ref.dtype)


def _moe_up(h, row_token, w_gate, w_up, layer, block_expert, n_used, n_blocks):
    d, ff = w_gate.shape[2], w_gate.shape[3]
    bm = MOE_ROWS
    tok = row_token.reshape(n_blocks, 1, bm)
    last = n_blocks - 1
    return pl.pallas_call(
        _moe_up_kernel,
        grid_spec=pltpu.PrefetchScalarGridSpec(
            num_scalar_prefetch=2,
            grid=(n_blocks,),
            in_specs=[pl.BlockSpec((1, 1, bm), lambda i, be, nu: (i, 0, 0), memory_space=pltpu.SMEM),
                      pl.BlockSpec((1, 1, bm), lambda i, be, nu: (jnp.minimum(i + 1, last), 0, 0),
                                   memory_space=pltpu.SMEM),
                      pl.BlockSpec(memory_space=pl.ANY),
                      pl.BlockSpec((1, 1, d, ff), lambda i, be, nu: (layer, be[i], 0, 0)),
                      pl.BlockSpec((1, 1, d, ff), lambda i, be, nu: (layer, be[i], 0, 0))],
            out_specs=pl.BlockSpec((bm, ff), lambda i, be, nu: (i, 0)),
            scratch_shapes=[pltpu.VMEM((2, bm, d // 2), jnp.int32),
                            pltpu.VMEM((d, ff), BF16), pltpu.VMEM((d, ff), BF16),
                            pltpu.SemaphoreType.DMA((2,))]),
        out_shape=jax.ShapeDtypeStruct((n_blocks * bm, ff), BF16),
        compiler_params=_params(("arbitrary",)),
        name="moe_up",
    )(block_expert, n_used, tok, tok, h, w_gate, w_up)


def _moe_down_kernel(be_ref, nu_ref, x_ref, wd_ref, o_ref, wd_bf):
    i = pl.program_id(0)
    prev = be_ref[jnp.maximum(i - 1, 0)]

    @pl.when((i == 0) | (be_ref[i] != prev))
    def _():
        wd_bf[...] = wd_ref[0, 0].astype(BF16)

    @pl.when(i < nu_ref[0])
    def _():
        o_ref[...] = _pack_pairs(jnp.dot(x_ref[...], wd_bf[...], preferred_element_type=F32))

    @pl.when(i >= nu_ref[0])
    def _():
        o_ref[...] = jnp.zeros(o_ref.shape, o_ref.dtype)


def _moe_down(hmid, w_down, layer, block_expert, n_used, n_blocks):
    ff = hmid.shape[1]
    d = w_down.shape[3]
    bm = MOE_ROWS
    return pl.pallas_call(
        _moe_down_kernel,
        grid_spec=pltpu.PrefetchScalarGridSpec(
            num_scalar_prefetch=2,
            grid=(n_blocks,),
            in_specs=[pl.BlockSpec((bm, ff), lambda i, be, nu: (i, 0)),
                      pl.BlockSpec((1, 1, ff, d), lambda i, be, nu: (layer, be[i], 0, 0))],
            out_specs=pl.BlockSpec((bm, d // 2), lambda i, be, nu: (i, 0)),
            scratch_shapes=[pltpu.VMEM((ff, d), BF16)]),
        out_shape=jax.ShapeDtypeStruct((n_blocks * bm, d // 2), jnp.int32),
        compiler_params=_params(("arbitrary",)),
        name="moe_down",
    )(block_expert, n_used, hmid, w_down)


def _combine_kernel(dest_ref, nxt_ref, w_ref, x_ref, gt_ref, g_ref, sc_ref, sh_ref, ys_ref, *rest, emit_x, modulate):
    if emit_x:
        xo_ref, h_ref, buf, sem = rest
    else:
        h_ref, buf, sem = rest
    tc = x_ref.shape[0]
    i = pl.program_id(0)
    slot = lax.rem(i, 2)

    def row_copy(s, tok, choice, src_row):
        return pltpu.make_async_copy(ys_ref.at[pl.ds(src_row, 1)], buf.at[s, choice, pl.ds(tok, 1)], sem.at[s])

    def start_tile(s, ids_ref):
        for tok in range(tc):
            for choice in range(MOE_TOP_K):
                row_copy(s, tok, choice, ids_ref[0, 0, MOE_TOP_K * tok + choice]).start(priority=choice)

    def wait_tile(s):
        def body(j, c):
            row_copy(s, 0, 0, 0).wait()
            return c

        lax.fori_loop(0, MOE_TOP_K * tc, body, 0, unroll=8)

    @pl.when(i == 0)
    def _():
        start_tile(0, dest_ref)

    @pl.when(i + 1 < pl.num_programs(0))
    def _():
        start_tile(1 - slot, nxt_ref)

    wait_tile(slot)
    w = w_ref[...]
    y = w[:, 0:1] * _unpack_pairs(buf[slot, 0]) + w[:, 1:2] * _unpack_pairs(buf[slot, 1])
    x_new = x_ref[...] + gt_ref[...] * y
    if emit_x:
        xo_ref[...] = x_new
    h_ref[...] = _rms_mod(x_new, g_ref, sc_ref, sh_ref, modulate).astype(h_ref.dtype)


def _moe_combine(x, ys, dest, weights, gt, g, sc, sh, *, emit_x, modulate, h_dtype):
    t, d = x.shape
    tc = min(256, t)
    n_tiles = t // tc
    ids = dest.reshape(n_tiles, 1, MOE_TOP_K * tc)
    row = pl.BlockSpec((1, d), lambda i: (0, 0))
    tile = pl.BlockSpec((tc, d), lambda i: (i, 0))
    out_specs = [tile]
    out_shape = [jax.ShapeDtypeStruct((t, d), h_dtype)]
    if emit_x:
        out_specs = [tile, tile]
        out_shape = [jax.ShapeDtypeStruct((t, d), F32)] + out_shape
    outs = pl.pallas_call(
        functools.partial(_combine_kernel, emit_x=emit_x, modulate=modulate),
        grid=(n_tiles,),
        in_specs=[pl.BlockSpec((1, 1, MOE_TOP_K * tc), lambda i: (i, 0, 0), memory_space=pltpu.SMEM),
                  pl.BlockSpec((1, 1, MOE_TOP_K * tc), lambda i: (jnp.minimum(i + 1, n_tiles - 1), 0, 0),
                               memory_space=pltpu.SMEM),
                  pl.BlockSpec((tc, MOE_TOP_K), lambda i: (i, 0)),
                  tile, row, row, row, row,
                  pl.BlockSpec(memory_space=pl.ANY)],
        out_specs=out_specs,
        out_shape=out_shape,
        scratch_shapes=[pltpu.VMEM((2, MOE_TOP_K, tc, d // 2), jnp.int32), pltpu.SemaphoreType.DMA((2,))],
        compiler_params=_params(("arbitrary",)),
        name="moe_combine",
    )(ids, ids, weights, x, gt.reshape(1, d), g.reshape(1, d), sc.reshape(1, d), sh.reshape(1, d), ys)
    if emit_x:
        return outs[0], outs[1]
    return None, outs[0]


def _route_rows(logits, tril_ref, run_ref):
    lane = lax.broadcasted_iota(jnp.int32, logits.shape, 1)
    neg = -jnp.inf
    far = jnp.int32(LANES)

    def first_max(vals):
        top = jnp.max(vals, axis=-1, keepdims=True)
        return top, jnp.min(jnp.where(vals == top, lane, far), axis=-1, keepdims=True)

    is_group = lane < MOE_GROUPS
    g_top, g_idx = first_max(jnp.where(is_group, logits, neg))
    p_group = 1.0 / jnp.sum(jnp.where(is_group, jnp.exp(logits - g_top), 0.0), axis=-1, keepdims=True)
    first = MOE_GROUPS + MOE_EPG * g_idx
    cand = jnp.where((lane >= first) & (lane < first + MOE_EPG), logits, neg)
    e1, i1 = first_max(cand)
    e2, i2 = first_max(jnp.where(lane == i1, neg, cand))
    z = jnp.exp(e2 - e1)
    w1 = p_group / (1.0 + z)
    w2 = p_group * z / (1.0 + z)
    x1 = i1 - MOE_GROUPS
    x2 = i2 - MOE_GROUPS
    hit1 = lane == x1
    hit2 = lane == x2
    onehot = (hit1 | hit2).astype(F32)
    cum = jnp.dot(tril_ref[...], _bf(onehot), preferred_element_type=F32) + run_ref[0:1, :]
    rank1 = jnp.sum(jnp.where(hit1, cum, 0.0), axis=-1, keepdims=True) - 1.0
    rank2 = jnp.sum(jnp.where(hit2, cum, 0.0), axis=-1, keepdims=True) - 1.0
    last = cum[cum.shape[0] - 1:cum.shape[0], :]
    run_ref[...] = jnp.broadcast_to(last, run_ref.shape)
    cols = (w1, w2, x1.astype(F32), x2.astype(F32), rank1, rank2)
    packed = jnp.zeros(logits.shape, F32)
    for j, cvals in enumerate(cols):
        packed = jnp.where(lane == j, cvals, packed)
    return packed, last


def _dispatch_plan(expert, rank, counts, n_blocks):
    bm = MOE_ROWS
    flat = expert.reshape(-1)
    n_assign = flat.shape[0]
    padded = (counts + bm - 1) // bm * bm
    pad_end = jnp.cumsum(padded)
    dest = (pad_end - padded)[flat] + rank.reshape(-1)
    row_token = jnp.zeros((n_blocks * bm,), jnp.int32).at[dest].set(
        jnp.arange(n_assign, dtype=jnp.int32) // MOE_TOP_K)
    block_start = jnp.arange(n_blocks, dtype=jnp.int32) * bm
    block_expert = jnp.minimum(jnp.sum((pad_end[None, :] <= block_start[:, None]).astype(jnp.int32), axis=1),
                               MOE_EXPERTS - 1)
    n_used = (pad_end[-1] // bm).reshape(1)
    return dest.astype(jnp.int32), row_token, block_expert.astype(jnp.int32), n_used.astype(jnp.int32)


def _router_params(lp, d):
    w_r = jnp.zeros((d, LANES), F32).at[:, :MOE_GROUPS].set(lp['moe_gw'])
    w_r = w_r.at[:, MOE_GROUPS:MOE_GROUPS + MOE_EXPERTS].set(lp['moe_ew'])
    b_r = jnp.zeros((LANES,), F32).at[:MOE_GROUPS].set(lp['moe_gb'])
    b_r = b_r.at[MOE_GROUPS:MOE_GROUPS + MOE_EXPERTS].set(lp['moe_eb'])
    return w_r.astype(BF16), b_r


def _hier_moe(x, h, route, count_rows, gt, layer, ex_gate, ex_up, ex_down, next_norm):
    t, d = x.shape
    weights = route[:, 0:2]
    expert = route[:, 2:4].astype(jnp.int32)
    rank = route[:, 4:6].astype(jnp.int32)
    counts = count_rows[0, :MOE_EXPERTS].astype(jnp.int32)
    n_blocks = (t * MOE_TOP_K) // MOE_ROWS + MOE_EXPERTS
    dest, row_token, block_expert, n_used = _dispatch_plan(expert, rank, counts, n_blocks)
    hmid = _moe_up(h, row_token, ex_gate, ex_up, layer, block_expert, n_used, n_blocks)
    ys = _moe_down(hmid, ex_down, layer, block_expert, n_used, n_blocks)
    return _moe_combine(x, ys, dest, weights, gt, *next_norm[:3], emit_x=next_norm[3], modulate=next_norm[3],
                        h_dtype=BF16 if next_norm[3] else F32)


def kernel(x, c, ada_w, ada_b, norm_mix, norm_ffn, w_in, rw_mu, rw_w0, rw_w2, rw_a0, rw_a2, rw_g2, rw_kk, rw_ka, rw_rk, rw_lnw, rw_lnb, rw_v0, rw_v1, rw_v2, ml_conv_w, ml_conv_b, ml_ib, ml_fb, ml_norm, hg_lb, hg_norm, p_a, p_b, p_c, w_out, moe_gw, moe_gb, moe_ew, moe_eb, ex_gate, ex_up, ex_down, final_norm):
    batch, t, d = x.shape
    assert batch == 1 and d == D_MODEL and t % 512 == 0
    depth = ada_w.shape[0]
    x = x.reshape(t, d)
    lbs = jnp.cumsum(jax.nn.softmax(hg_lb.astype(F32), axis=0), axis=0)
    lbs = lbs - lbs[:1]
    mod = _adaln_mod(c, ada_w, ada_b)
    zeros_d = jnp.zeros((d,), F32)
    c_ml = RW_COLS
    c_mg = c_ml + 2 * ML_QK_WIDTH + 2 * ML_WIDTH
    c_hg = c_mg + 2 * ML_HEADS
    c_gt = c_hg + 4 * HG_WIDTH
    v_first = None
    for l in range(depth):
        sh_m, sc_m, gt_m, sh_f, sc_f, gt_f = jnp.split(mod[l], 6)
        lp = dict(rw_mu=rw_mu[l], rw_w0=rw_w0[l], rw_w2=rw_w2[l], rw_a0=rw_a0[l], rw_a2=rw_a2[l], rw_g2=rw_g2[l],
                  rw_kk=rw_kk[l], rw_ka=rw_ka[l], rw_rk=rw_rk[l], rw_lnw=rw_lnw[l], rw_lnb=rw_lnb[l],
                  ml_conv_w=ml_conv_w[l], ml_conv_b=ml_conv_b[l], ml_ib=ml_ib[l], ml_fb=ml_fb[l],
                  ml_norm=ml_norm[l], hg_norm=hg_norm[l],
                  moe_gw=moe_gw[l], moe_gb=moe_gb[l], moe_ew=moe_ew[l], moe_eb=moe_eb[l])
        if l > 0:
            lp.update(rw_v0=rw_v0[l - 1], rw_v1=rw_v1[l - 1], rw_v2=rw_v2[l - 1])
        if l == 0:
            h = _norm_mod(x, norm_mix[l], sc_m, sh_m)
        wl = w_in[l]
        u_rw = _mm(h, wl[:, :c_ml].astype(BF16), jnp.zeros((c_ml,), F32), out_dtype=BF16, tn=c_ml)
        u_ml = _mm(h, wl[:, c_ml:c_mg].astype(BF16), jnp.zeros((c_mg - c_ml,), F32), out_dtype=BF16, tn=ML_WIDTH)
        w_mg = jnp.zeros((d, LANES), F32).at[:, :2 * ML_HEADS].set(wl[:, c_mg:c_hg])
        u_mg = _mm(h, w_mg.astype(BF16), jnp.zeros((LANES,), F32), out_dtype=F32, tn=LANES)
        u_hg = _mm(h, wl[:, c_hg:c_gt].astype(BF16), jnp.zeros((c_gt - c_hg,), F32), out_dtype=BF16, tn=ML_WIDTH)
        y_a, v_first = _rwkv_branch(u_rw, v_first, lp)
        y_b = _mlstm_branch(u_ml, u_mg, lp)
        y_c = _hgrn_branch(u_hg, lbs[l], lp)
        x, h, route, count_rows = _merge_out(
            h, y_a, y_b, y_c, wl[:, c_gt:].astype(BF16), p_a[l].astype(BF16), p_b[l].astype(BF16),
            p_c[l].astype(BF16), w_out[l].astype(BF16), x, gt_m, norm_ffn[l], sc_f, sh_f, *_router_params(lp, d))
        if l + 1 < depth:
            nxt = jnp.split(mod[l + 1], 6)
            next_norm = (norm_mix[l + 1], nxt[1], nxt[0], True)
        else:
            next_norm = (final_norm, zeros_d, zeros_d, False)
        x, h = _hier_moe(x, h, route, count_rows, gt_f, l, ex_gate, ex_up, ex_down, next_norm)
    return h.reshape(batch, t, d)
```

```python
import functools

import numpy as np
import jax
import jax.numpy as jnp
from jax import lax
from jax.experimental import pallas as pl
from jax.experimental.pallas import tpu as pltpu

F32 = jnp.float32
BF16 = jnp.bfloat16

D_MODEL = 2048
RW_HEADS, RW_HEAD_DIM, RW_WIDTH = 8, 64, 512
RW_LORA_DECAY, RW_LORA_ICLR, RW_LORA_GATE = 64, 64, 128
RW_LORA = RW_LORA_DECAY + RW_LORA_ICLR + RW_LORA_GATE
RW_COLS = 3 * RW_WIDTH + RW_LORA
RW_GN_EPS = 64e-5
RW_GROUP = 4
RW_CHUNKS_PER_STEP = 4
ML_HEADS, ML_QK_DIM, ML_V_DIM = 4, 128, 256
ML_QK_WIDTH, ML_WIDTH = 512, 1024
ML_CONV = 4
ML_SOFTCAP = 15.0
ML_CHUNKS_PER_STEP = 2
HG_HEADS, HG_DIM, HG_WIDTH = 4, 128, 512
HG_CHUNK = 128
CHUNK = 64
MOE_GROUPS, MOE_EPG, MOE_EXPERTS, MOE_TOP_K = 4, 8, 32, 2
NORM_EPS = 1e-6

LANES = 128
SUBLANES = 8
MOE_ROWS = 256
DMA_PRIORITIES = 2
VMEM_LIMIT = 56 * 1024 * 1024


def _params(sem, limit=VMEM_LIMIT):
    return pltpu.CompilerParams(dimension_semantics=sem, vmem_limit_bytes=limit)


def _bf(x):
    return x.astype(BF16)


def _dot(a, b):
    return jnp.dot(_bf(a), _bf(b), preferred_element_type=F32)


def _dot_nt(a, b):
    return lax.dot_general(_bf(a), _bf(b), (((1,), (1,)), ((), ())), preferred_element_type=F32)


def _dot_tn(a, b):
    return lax.dot_general(_bf(a), _bf(b), (((0,), (0,)), ((), ())), preferred_element_type=F32)


def _bf16_pieces(x, pieces):
    out = []
    rest = x
    for _ in range(pieces):
        part = rest.astype(BF16)
        out.append(part)
        rest = rest - part.astype(F32)
    return out


def _sel_dot(sel, x, pieces):
    acc = None
    for part in _bf16_pieces(x, pieces):
        term = jnp.dot(sel, part, preferred_element_type=F32)
        acc = term if acc is None else acc + term
    return acc


def _seg_sum(x, bd):
    acc = None
    for part in _bf16_pieces(x, 2):
        term = jnp.dot(part, bd, preferred_element_type=F32)
        acc = term if acc is None else acc + term
    return acc


def _sigmoid(x):
    return jax.nn.sigmoid(x)


def _silu(x):
    return x * jax.nn.sigmoid(x)


def _log_sigmoid(x):
    return jnp.minimum(x, 0.0) - jnp.log1p(jnp.exp(-jnp.abs(x)))


def _softplus(x):
    return jnp.maximum(x, 0.0) + jnp.log1p(jnp.exp(-jnp.abs(x)))


def _mod_kernel(c_ref, w_ref, b_ref, o_ref):
    cond = _silu(c_ref[...])
    o_ref[0] = jnp.sum(w_ref[0] * cond, axis=0, keepdims=True) + b_ref[0]


def _adaln_mod(c, ada_w, ada_b):
    depth, d, n = ada_w.shape
    tn = 2048
    out = pl.pallas_call(
        _mod_kernel,
        grid=(depth, n // tn),
        in_specs=[pl.BlockSpec((d, 1), lambda l, j: (0, 0)),
                  pl.BlockSpec((1, d, tn), lambda l, j: (l, 0, j)),
                  pl.BlockSpec((1, 1, tn), lambda l, j: (l, 0, j))],
        out_specs=pl.BlockSpec((1, 1, tn), lambda l, j: (l, 0, j)),
        out_shape=jax.ShapeDtypeStruct((depth, 1, n), F32),
        compiler_params=_params(("arbitrary", "arbitrary")),
        name="adaln_mod",
    )(c.reshape(d, 1), ada_w, ada_b.reshape(depth, 1, n))
    return out[:, 0, :]


HI_HALF = -65536


def _pack_pairs(y):
    n = y.shape[1] // 2
    bits = lax.bitcast_convert_type(y.astype(BF16).astype(F32), jnp.int32)
    return lax.shift_right_logical(bits[:, :n], 16) | (bits[:, n:] & HI_HALF)


def _unpack_pairs(p):
    lo = lax.bitcast_convert_type(lax.shift_left(p, 16), F32)
    hi = lax.bitcast_convert_type(p & HI_HALF, F32)
    return jnp.concatenate([lo, hi], axis=1)


def _rms_mod(x, g_ref, sc_ref, sh_ref, modulate):
    y = x * lax.rsqrt(jnp.mean(x * x, axis=-1, keepdims=True) + NORM_EPS) * g_ref[...]
    if modulate:
        y = y * (1.0 + sc_ref[...]) + sh_ref[...]
    return y


def _norm_kernel(x_ref, g_ref, sc_ref, sh_ref, o_ref):
    o_ref[...] = _rms_mod(x_ref[...], g_ref, sc_ref, sh_ref, True).astype(o_ref.dtype)


def _norm_mod(x, g, sc, sh):
    t, d = x.shape
    tm = min(512, t)
    row = pl.BlockSpec((1, d), lambda i: (0, 0))
    return pl.pallas_call(
        _norm_kernel,
        grid=(t // tm,),
        in_specs=[pl.BlockSpec((tm, d), lambda i: (i, 0)), row, row, row],
        out_specs=pl.BlockSpec((tm, d), lambda i: (i, 0)),
        out_shape=jax.ShapeDtypeStruct((t, d), BF16),
        compiler_params=_params(("arbitrary",)),
        name="rmsnorm_mod",
    )(x, g.reshape(1, d), sc.reshape(1, d), sh.reshape(1, d))


def _mm_kernel(a_ref, w_ref, o_ref):
    o_ref[...] = _dot(a_ref[...], w_ref[...]).astype(o_ref.dtype)


def _row_tile(m):
    return 1024 if m % 1024 == 0 else min(512, m)


def _mm(a, w, *, out_dtype, tn):
    m, k = a.shape
    n = w.shape[1]
    tm = _row_tile(m)
    return pl.pallas_call(
        _mm_kernel,
        grid=(n // tn, m // tm),
        in_specs=[pl.BlockSpec((tm, k), lambda j, i: (i, 0)),
                  pl.BlockSpec((k, tn), lambda j, i: (0, j))],
        out_specs=pl.BlockSpec((tm, tn), lambda j, i: (i, j)),
        out_shape=jax.ShapeDtypeStruct((m, n), out_dtype),
        compiler_params=_params(("arbitrary", "arbitrary")),
        name="dense_matmul",
    )(a, w)


def _rwkv_kernel(*refs, has_vres):
    if has_vres:
        (u_ref, vf_ref, mu_ref, wl_ref, w0_ref, a0_ref, kk_ref, ka_ref, rk_ref, lnw_ref, lnb_ref,
         v0_ref, v1_ref, v2_ref, bd_ref, tril_ref, y_ref, xbuf, st_ref) = refs
    else:
        (u_ref, mu_ref, wl_ref, w0_ref, a0_ref, kk_ref, ka_ref, rk_ref, lnw_ref, lnb_ref,
         bd_ref, tril_ref, y_ref, vf_out_ref, xbuf, st_ref) = refs
    L = CHUNK
    W = RW_WIDTH
    TB = u_ref.shape[0]
    n_chunks = TB // L

    @pl.when(pl.program_id(0) == 0)
    def _():
        xbuf[0:SUBLANES, :] = jnp.zeros((SUBLANES, RW_COLS), F32)
        st_ref[...] = jnp.zeros(st_ref.shape, F32)

    u = u_ref[...].astype(F32)
    xbuf[SUBLANES:SUBLANES + TB, :] = u
    prev = xbuf[SUBLANES - 1:SUBLANES - 1 + TB, :]
    xbuf[0:SUBLANES, :] = u[TB - SUBLANES:TB, :]
    xs = u + mu_ref[...] * (prev - u)
    r = xs[:, 0:W]
    k = xs[:, W:2 * W]
    v = xs[:, 2 * W:3 * W]
    lr = xs[:, 3 * W:3 * W + RW_LORA]
    lane = lax.broadcasted_iota(jnp.int32, lr.shape, 1)
    act = jnp.where(lane < RW_LORA_DECAY, jnp.tanh(lr),
                    jnp.where(lane < RW_LORA_DECAY + RW_LORA_ICLR, lr, _sigmoid(lr)))
    lo = _dot(act, wl_ref[...])
    z = w0_ref[...] + lo[:, 0:W]
    log_w = -_softplus(-z) - 0.5
    ld = -jnp.exp(log_w)
    iclr = _sigmoid(a0_ref[...] + lo[:, W:2 * W])
    gate = lo[:, 2 * W:3 * W]
    if has_vres:
        vv = _dot(_dot(v, v1_ref[...]), v2_ref[...])
        v = v + (vf_ref[...] - v) * _sigmoid(v0_ref[...] + vv)
    else:
        vf_out_ref[...] = v
    bd = bd_ref[...]
    kk = k * kk_ref[...]
    k = k * (1.0 + (iclr - 1.0) * ka_ref[...])
    sums = _seg_sum(jnp.concatenate([kk * kk, r * k * rk_ref[...]], axis=0), bd)
    kk = kk / jnp.maximum(jnp.sqrt(sums[0:TB, :]), 1e-12)
    a = -kk
    b = kk * iclr
    bonus = sums[TB:2 * TB, :] * v

    cum = _sel_dot(tril_ref[...], ld, 2)
    cl_rows = [cum[(c + 1) * L - 1:(c + 1) * L, :] for c in range(n_chunks)]
    cl = jnp.concatenate([jnp.broadcast_to(x, (L, W)) for x in cl_rows], axis=0)
    e_neg = jnp.exp(-cum)
    e_last = jnp.exp(cl - cum)
    a_t = a * jnp.exp(cum - ld)
    r_t = r * jnp.exp(cum)
    b_t = b * e_neg
    k_t = k * e_neg
    b_h = b * e_last
    k_h = k * e_last

    G = RW_GROUP
    S = G * L
    row = lax.broadcasted_iota(jnp.int32, (S, S), 0)
    col = lax.broadcasted_iota(jnp.int32, (S, S), 1)
    same = (row // L) == (col // L)
    strict = same & (row > col)
    incl = same & (row >= col)
    eye = (row == col).astype(F32)
    lane_head = lax.broadcasted_iota(jnp.int32, (1, S), 1) // RW_HEAD_DIM
    head_masks = [(lane_head == j).astype(F32) for j in range(G)]

    n_groups = RW_HEADS // G
    units = [(c, g) for c in range(n_chunks) for g in range(n_groups)]

    def unit_rows(x, c, g):
        return x[c * L:(c + 1) * L, g * S:(g + 1) * S]

    def stack(x, c, g):
        xs_ = unit_rows(x, c, g)
        return jnp.concatenate([xs_ * m for m in head_masks], axis=0)

    def tile(x, c, g):
        return jnp.concatenate([unit_rows(x, c, g)] * G, axis=0)

    a_s = {u: stack(a_t, *u) for u in units}
    r_s = {u: stack(r_t, *u) for u in units}
    v_s = {u: stack(v, *u) for u in units}
    bk_h = {u: jnp.concatenate([stack(b_h, *u), stack(k_h, *u)], axis=0) for u in units}
    xm = {u: _dot_nt(jnp.concatenate([a_s[u], r_s[u]], axis=0),
                     jnp.concatenate([tile(b_t, *u), tile(k_t, *u)], axis=0)) for u in units}
    n_ab = {u: jnp.where(strict, xm[u][0:S, 0:S], 0.0) for u in units}
    n_ak = {u: jnp.where(strict, xm[u][0:S, S:2 * S], 0.0) for u in units}
    r_bk = {u: jnp.concatenate([jnp.where(incl, xm[u][S:2 * S, 0:S], 0.0),
                                jnp.where(incl, xm[u][S:2 * S, S:2 * S], 0.0)], axis=1) for u in units}
    tinv = {u: eye + n_ab[u] for u in units}
    pw = dict(n_ab)
    for _ in range(5):
        pw = {u: _dot(pw[u], pw[u]) for u in units}
        tinv = {u: tinv[u] + _dot(tinv[u], pw[u]) for u in units}
    nv = {u: _dot(n_ak[u], v_s[u]) for u in units}
    pq = {u: _dot(tinv[u], jnp.concatenate([a_s[u], nv[u]], axis=1)) for u in units}
    p_s = {u: pq[u][:, 0:S] for u in units}
    qv = {u: jnp.concatenate([pq[u][:, S:2 * S], v_s[u]], axis=0) for u in units}
    y1 = {u: r_s[u] + _dot(r_bk[u][:, 0:S], p_s[u]) for u in units}
    y0 = {u: _dot(r_bk[u], qv[u]) for u in units}
    gt = {u: _dot_tn(p_s[u], bk_h[u][0:S, :]) for u in units}
    ht = {u: _dot_tn(qv[u], bk_h[u]) for u in units}

    state = [st_ref[g] for g in range(n_groups)]
    y_rows = []
    for c in range(n_chunks):
        w_last = jnp.exp(cl_rows[c])
        ys = []
        for g in range(n_groups):
            u = (c, g)
            st = state[g]
            y_s = _dot_nt(y1[u], st) + y0[u]
            state[g] = st * w_last[:, g * S:(g + 1) * S] + _dot(st, gt[u]) + ht[u]
            y_g = y_s[0:L, :]
            for j in range(1, G):
                y_g = y_g + y_s[j * L:(j + 1) * L, :]
            ys.append(y_g)
        y_rows.append(jnp.concatenate(ys, axis=1))
    for g in range(n_groups):
        st_ref[g] = state[g]
    y = jnp.concatenate(y_rows, axis=0)

    inv_n = 1.0 / RW_HEAD_DIM
    mean = _seg_sum(y, bd) * inv_n
    dlt = y - mean
    var = _seg_sum(dlt * dlt, bd) * inv_n
    yn = dlt * lax.rsqrt(var + RW_GN_EPS) * lnw_ref[...] + lnb_ref[...]
    y_ref[...] = ((yn + bonus) * gate).astype(y_ref.dtype)


def _blockdiag_ones(width, head):
    idx = np.arange(width) // head
    return jnp.asarray((idx[:, None] == idx[None, :]).astype(np.float32), BF16)


def _tril_ones(n, blocks=1):
    return jnp.asarray(np.kron(np.eye(blocks, dtype=np.float32), np.tril(np.ones((n, n), np.float32))), BF16)


def _rwkv_branch(u_rw, v_first, lp):
    t = u_rw.shape[0]
    L = CHUNK
    has_vres = v_first is not None
    W = RW_WIDTH
    vec = lambda a: a.reshape(1, -1).astype(F32)
    full = lambda shape: pl.BlockSpec(shape, lambda i: (0,) * len(shape))
    tb = RW_CHUNKS_PER_STEP * L
    rows = lambda width: pl.BlockSpec((tb, width), lambda i: (i, 0))
    wl = jnp.zeros((RW_LORA, 3 * W), F32)
    r0, r1 = RW_LORA_DECAY, RW_LORA_DECAY + RW_LORA_ICLR
    wl = wl.at[0:r0, 0:W].set(lp['rw_w2']).at[r0:r1, W:2 * W].set(lp['rw_a2']).at[r1:, 2 * W:].set(lp['rw_g2'])
    ins = [u_rw]
    specs = [rows(RW_COLS)]
    if has_vres:
        ins.append(v_first)
        specs.append(rows(W))
    ins += [vec(lp['rw_mu']), wl.astype(BF16), vec(lp['rw_w0']), vec(lp['rw_a0']), vec(lp['rw_kk']),
            vec(lp['rw_ka']), vec(lp['rw_rk']), vec(lp['rw_lnw']), vec(lp['rw_lnb'])]
    specs += [full((1, RW_COLS)), full((RW_LORA, 3 * W))] + [full((1, W))] * 7
    if has_vres:
        v1 = jnp.zeros((W, LANES), F32).at[:, :lp['rw_v1'].shape[1]].set(lp['rw_v1'])
        v2 = jnp.zeros((LANES, W), F32).at[:lp['rw_v2'].shape[0], :].set(lp['rw_v2'])
        ins += [vec(lp['rw_v0']), v1.astype(BF16), v2.astype(BF16)]
        specs += [full((1, W)), full((W, LANES)), full((LANES, W))]
    ins += [_blockdiag_ones(W, RW_HEAD_DIM), _tril_ones(L, RW_CHUNKS_PER_STEP)]
    specs += [full((W, W)), full((tb, tb))]
    out_shape = [jax.ShapeDtypeStruct((t, W), BF16)]
    out_specs = [rows(W)]
    if not has_vres:
        out_shape.append(jax.ShapeDtypeStruct((t, W), F32))
        out_specs.append(rows(W))
    outs = pl.pallas_call(
        functools.partial(_rwkv_kernel, has_vres=has_vres),
        grid=(t // tb,),
        in_specs=specs,
        out_specs=out_specs,
        out_shape=out_shape,
        scratch_shapes=[pltpu.VMEM((tb + SUBLANES, RW_COLS), F32),
                        pltpu.VMEM((RW_HEADS // RW_GROUP, RW_GROUP * RW_HEAD_DIM, RW_GROUP * RW_HEAD_DIM), F32)],
        compiler_params=_params(("arbitrary",)),
        name="rwkv7_branch",
    )(*ins)
    if has_vres:
        return outs[0], v_first
    return outs[0], outs[1]


def _mlstm_kernel(qk_ref, v_ref, o_ref, g_ref, cw_ref, cb_ref, gb_ref, ng_ref, tril_ref,
                  y_ref, xbuf, c_ref, m_ref):
    L = CHUNK
    TB = qk_ref.shape[0]
    n_chunks = TB // L

    @pl.when(pl.program_id(0) == 0)
    def _():
        xbuf[0:SUBLANES, :] = jnp.zeros((SUBLANES, 2 * ML_QK_WIDTH), F32)
        c_ref[...] = jnp.zeros(c_ref.shape, F32)
        m_ref[...] = jnp.zeros(m_ref.shape, F32)

    x0 = qk_ref[...].astype(F32)
    xbuf[SUBLANES:SUBLANES + TB, :] = x0
    conv = cb_ref[...] + cw_ref[ML_CONV - 1:ML_CONV, :] * x0
    for dly in range(1, ML_CONV):
        conv = conv + cw_ref[ML_CONV - 1 - dly:ML_CONV - dly, :] * xbuf[SUBLANES - dly:SUBLANES - dly + TB, :]
    xbuf[0:SUBLANES, :] = x0[TB - SUBLANES:TB, :]
    qk = _silu(conv)
    q = qk[:, :ML_QK_WIDTH]
    k = qk[:, ML_QK_WIDTH:] * (ML_QK_DIM ** -0.5)

    pre = g_ref[...] + gb_ref[...]
    cap = ML_SOFTCAP * jnp.tanh(pre / ML_SOFTCAP)
    lane = lax.broadcasted_iota(jnp.int32, pre.shape, 1)
    gates = jnp.where(lane < ML_HEADS, cap, _log_sigmoid(cap))
    csum = _sel_dot(tril_ref[...], gates, 3)
    gates_t = gates.T
    csum_t = csum.T

    row = lax.broadcasted_iota(jnp.int32, (L, L), 0)
    col = lax.broadcasted_iota(jnp.int32, (L, L), 1)
    causal = row >= col
    ones_col = (lax.broadcasted_iota(jnp.int32, (L, LANES), 1) == 0).astype(F32)
    vv = v_ref[...].astype(F32)
    og = o_ref[...].astype(F32)
    units = [(c, h) for c in range(n_chunks) for h in range(ML_HEADS)]

    def rows(c):
        return slice(c * L, (c + 1) * L)

    qh = {(c, h): q[rows(c), h * ML_QK_DIM:(h + 1) * ML_QK_DIM] for c, h in units}
    kh = {(c, h): k[rows(c), h * ML_QK_DIM:(h + 1) * ML_QK_DIM] for c, h in units}
    vh = {(c, h): jnp.concatenate([vv[rows(c), h * ML_V_DIM:(h + 1) * ML_V_DIM], ones_col], axis=1)
          for c, h in units}
    b_col = {(c, h): csum[rows(c), ML_HEADS + h:ML_HEADS + h + 1] for c, h in units}
    li_col = {(c, h): gates[rows(c), h:h + 1] for c, h in units}
    dmat = {(c, h): jnp.where(causal, b_col[c, h] - csum_t[ML_HEADS + h:ML_HEADS + h + 1, rows(c)]
                              + gates_t[h:h + 1, rows(c)], -jnp.inf) for c, h in units}
    dmax = {u: jnp.max(dmat[u], axis=-1, keepdims=True) for u in units}
    s_loc = {u: _dot_nt(qh[u], kh[u]) * jnp.exp(dmat[u] - dmax[u]) for u in units}
    sv = {u: _dot(s_loc[u], vh[u]) for u in units}
    b_last = {u: b_col[u][L - 1:L, :] for u in units}
    g = {u: b_last[u] - b_col[u] + li_col[u] for u in units}
    gmax = {u: jnp.max(g[u], axis=0, keepdims=True) for u in units}
    kv = {u: _dot_tn(jnp.exp(g[u] - gmax[u]) * kh[u], vh[u]) for u in units}

    c_state = [c_ref[h] for h in range(ML_HEADS)]
    m_state = [m_ref[h][0:1, 0:1] for h in range(ML_HEADS)]
    out_rows = []
    for c in range(n_chunks):
        outs = []
        for h in range(ML_HEADS):
            u = (c, h)
            m_prev = m_state[h]
            m_inter = b_col[u] + m_prev
            m_t = jnp.maximum(m_inter, dmax[u])
            num_aug = jnp.exp(dmax[u] - m_t) * sv[u] + jnp.exp(m_inter - m_t) * _dot(qh[u], c_state[h])
            num = num_aug[:, :ML_V_DIM]
            den = num_aug[:, ML_V_DIM:ML_V_DIM + 1]
            hh = num / jnp.maximum(jnp.abs(den), jnp.exp(-m_t))
            m_new = jnp.maximum(b_last[u] + m_prev, gmax[u])
            c_state[h] = jnp.exp(b_last[u] + m_prev - m_new) * c_state[h] + jnp.exp(gmax[u] - m_new) * kv[u]
            m_state[h] = m_new
            ng = ng_ref[:, h * ML_V_DIM:(h + 1) * ML_V_DIM]
            hn = hh * lax.rsqrt(jnp.mean(hh * hh, axis=-1, keepdims=True) + NORM_EPS) * ng
            outs.append(hn * _sigmoid(og[rows(c), h * ML_V_DIM:(h + 1) * ML_V_DIM]))
        out_rows.append(jnp.concatenate(outs, axis=1))
    for h in range(ML_HEADS):
        c_ref[h] = c_state[h]
        m_ref[h] = jnp.broadcast_to(m_state[h], (SUBLANES, LANES))
    y_ref[...] = jnp.concatenate(out_rows, axis=0).astype(y_ref.dtype)


def _mlstm_branch(u_ml, u_mg, lp):
    t = u_ml.shape[0]
    L = CHUNK
    full = lambda shape: pl.BlockSpec(shape, lambda i: (0,) * len(shape))
    gb = jnp.zeros((1, LANES), F32).at[0, 0:ML_HEADS].set(lp['ml_ib']).at[0, ML_HEADS:2 * ML_HEADS].set(lp['ml_fb'])
    tb = ML_CHUNKS_PER_STEP * L
    return pl.pallas_call(
        _mlstm_kernel,
        grid=(t // tb,),
        in_specs=[pl.BlockSpec((tb, ML_WIDTH), lambda i: (i, 0)),
                  pl.BlockSpec((tb, ML_WIDTH), lambda i: (i, 1)),
                  pl.BlockSpec((tb, ML_WIDTH), lambda i: (i, 2)),
                  pl.BlockSpec((tb, LANES), lambda i: (i, 0)),
                  full((ML_CONV, 2 * ML_QK_WIDTH)), full((1, 2 * ML_QK_WIDTH)), full((1, LANES)),
                  full((1, ML_WIDTH)), full((tb, tb))],
        out_specs=pl.BlockSpec((tb, ML_WIDTH), lambda i: (i, 0)),
        out_shape=jax.ShapeDtypeStruct((t, ML_WIDTH), BF16),
        scratch_shapes=[pltpu.VMEM((tb + SUBLANES, 2 * ML_QK_WIDTH), F32),
                        pltpu.VMEM((ML_HEADS, ML_QK_DIM, ML_V_DIM + LANES), F32),
                        pltpu.VMEM((ML_HEADS, SUBLANES, LANES), F32)],
        compiler_params=_params(("arbitrary",)),
        name="mlstm_branch",
    )(u_ml, u_ml, u_ml, u_mg, lp['ml_conv_w'], lp['ml_conv_b'].reshape(1, -1), gb,
      lp['ml_norm'].reshape(1, -1), _tril_ones(L, ML_CHUNKS_PER_STEP))


def _hgrn_levels():
    L = HG_CHUNK
    t = np.arange(L)
    sel = [np.tril(np.ones((L, L), np.float32))]
    masks = []
    size = L
    while size >= 2:
        half = size // 2
        ref_row = (t // size) * size + half - 1
        sel.append((t[None, :] <= ref_row[:, None]).astype(np.float32))
        same = (t[:, None] // size) == (t[None, :] // size)
        masks.append((same & ((t[:, None] % size) >= half) & ((t[None, :] % size) < half)).astype(np.float32))
        size = half
    return np.concatenate(sel, axis=0), np.stack(masks)


def _hgrn_kernel(u_ref, la_ref, lc_ref, lb_ref, ng_ref, sel_ref, mask_ref, y_ref, s_ref, *, levels):
    L = HG_CHUNK
    W = HG_WIDTH

    @pl.when(pl.program_id(0) == 0)
    def _():
        s_ref[...] = jnp.zeros(s_ref.shape, F32)

    u = u_ref[...].astype(F32)
    q = _silu(u[:, 0:W])
    f_pre = u[:, W:2 * W]
    ii = u[:, 2 * W:3 * W]
    g_pre = u[:, 3 * W:4 * W]
    la = la_ref[...]
    lc = lc_ref[...] + _log_sigmoid(f_pre)
    log_f = jnp.maximum(la, lc) + jnp.log1p(jnp.exp(-jnp.abs(la - lc)))
    k = (1.0 - lb_ref[...]) * _sigmoid(-f_pre)

    cr = _sel_dot(sel_ref[...], log_f, 2)
    cum = cr[0:L, :]
    cl = cum[L - 1:L, :]
    q_in = q * jnp.exp(cum)
    k_out = k * jnp.exp(cl - cum)
    w_last = jnp.exp(cl)
    heads = range(HG_HEADS)
    sl = [slice(h * HG_DIM, (h + 1) * HG_DIM) for h in heads]
    att = [jnp.zeros((L, L), F32) for _ in heads]
    for lv in range(levels):
        ref = cr[(lv + 1) * L:(lv + 2) * L, :]
        qe = q * jnp.exp(jnp.minimum(cum - ref, 0.0))
        ke = k * jnp.exp(jnp.minimum(ref - cum, 0.0))
        mask = mask_ref[lv]
        att = [att[h] + mask * _dot_nt(qe[:, sl[h]], ke[:, sl[h]]) for h in heads]
    diag = [jnp.sum(q[:, sl[h]] * k[:, sl[h]], axis=-1, keepdims=True) for h in heads]
    intra = [_dot(att[h], ii[:, sl[h]]) + diag[h] * ii[:, sl[h]] for h in heads]
    kv = [_dot_tn(ii[:, sl[h]], k_out[:, sl[h]]) for h in heads]
    outs = []
    for h in heads:
        st = s_ref[h]
        o = intra[h] + _dot_nt(q_in[:, sl[h]], st)
        s_ref[h] = st * w_last[:, sl[h]] + kv[h]
        on = o * lax.rsqrt(jnp.mean(o * o, axis=-1, keepdims=True) + NORM_EPS) * ng_ref[:, sl[h]]
        outs.append(on * _silu(g_pre[:, sl[h]]))
    y_ref[...] = jnp.concatenate(outs, axis=1).astype(y_ref.dtype)


def _hgrn_branch(u_hg, lb, lp):
    t = u_hg.shape[0]
    L = HG_CHUNK
    sel, masks = _hgrn_levels()
    levels = masks.shape[0]
    full = lambda shape: pl.BlockSpec(shape, lambda i: (0,) * len(shape))
    lb = lb.reshape(1, -1).astype(F32)
    return pl.pallas_call(
        functools.partial(_hgrn_kernel, levels=levels),
        grid=(t // L,),
        in_specs=[pl.BlockSpec((L, 4 * HG_WIDTH), lambda i: (i, 0)),
                  full((1, HG_WIDTH)), full((1, HG_WIDTH)), full((1, HG_WIDTH)), full((1, HG_WIDTH)),
                  full(sel.shape), full(masks.shape)],
        out_specs=pl.BlockSpec((L, HG_WIDTH), lambda i: (i, 0)),
        out_shape=jax.ShapeDtypeStruct((t, HG_WIDTH), BF16),
        scratch_shapes=[pltpu.VMEM((HG_HEADS, HG_DIM, HG_DIM), F32)],
        compiler_params=_params(("arbitrary",)),
        name="hgrn2_branch",
    )(u_hg, jnp.log(lb), jnp.log1p(-lb), lb, lp['hg_norm'].reshape(1, -1),
      jnp.asarray(sel, BF16), jnp.asarray(masks, F32))


def _gate_merge_kernel(h_ref, ya_ref, yb_ref, yc_ref, wga_ref, wgb_ref, wgc_ref, pa_ref, pb_ref, pc_ref, o_ref):
    h = h_ref[...]
    merged = (_sigmoid(_dot(h, wga_ref[...])) * _dot(ya_ref[...], pa_ref[...])
              + _sigmoid(_dot(h, wgb_ref[...])) * _dot(yb_ref[...], pb_ref[...])
              + _sigmoid(_dot(h, wgc_ref[...])) * _dot(yc_ref[...], pc_ref[...]))
    o_ref[...] = merged.astype(o_ref.dtype)


def _residual_proj_kernel(a_ref, w_ref, x_ref, gt_ref, g_ref, sc_ref, sh_ref, wr_ref, br_ref, tril_ref,
                          xo_ref, h_ref, route_ref, count_ref, run_ref):
    @pl.when(pl.program_id(0) == 0)
    def _():
        run_ref[...] = jnp.zeros(run_ref.shape, F32)

    x_new = x_ref[...] + gt_ref[...] * _dot(a_ref[...], w_ref[...])
    xo_ref[...] = x_new
    h = _rms_mod(x_new, g_ref, sc_ref, sh_ref, True)
    h_ref[...] = _pack_pairs(h)
    route, counts = _route_rows(_dot(h, wr_ref[...]) + br_ref[...], tril_ref, run_ref)
    route_ref[...] = route
    count_ref[...] = jnp.broadcast_to(counts, count_ref.shape)


def _merge_out(h, ya, yb, yc, w_gates, pa, pb, pc, wo, x, gt, g_ffn, sc_ffn, sh_ffn, w_r, b_r):
    t, d = x.shape
    tm = _row_tile(t)
    tn = 512
    nb = d // tn
    merged = pl.pallas_call(
        _gate_merge_kernel,
        grid=(nb, t // tm),
        in_specs=[pl.BlockSpec((tm, d), lambda n, i: (i, 0)),
                  pl.BlockSpec((tm, RW_WIDTH), lambda n, i: (i, 0)),
                  pl.BlockSpec((tm, ML_WIDTH), lambda n, i: (i, 0)),
                  pl.BlockSpec((tm, HG_WIDTH), lambda n, i: (i, 0)),
                  pl.BlockSpec((d, tn), lambda n, i: (0, n)),
                  pl.BlockSpec((d, tn), lambda n, i: (0, nb + n)),
                  pl.BlockSpec((d, tn), lambda n, i: (0, 2 * nb + n)),
                  pl.BlockSpec((RW_WIDTH, tn), lambda n, i: (0, n)),
                  pl.BlockSpec((ML_WIDTH, tn), lambda n, i: (0, n)),
                  pl.BlockSpec((HG_WIDTH, tn), lambda n, i: (0, n))],
        out_specs=pl.BlockSpec((tm, tn), lambda n, i: (i, n)),
        out_shape=jax.ShapeDtypeStruct((t, d), BF16),
        compiler_params=_params(("arbitrary", "arbitrary")),
        name="gate_merge",
    )(h, ya, yb, yc, w_gates, w_gates, w_gates, pa, pb, pc)
    tm2 = min(512, t)
    row = pl.BlockSpec((1, d), lambda i: (0, 0))
    return pl.pallas_call(
        _residual_proj_kernel,
        grid=(t // tm2,),
        in_specs=[pl.BlockSpec((tm2, d), lambda i: (i, 0)),
                  pl.BlockSpec((d, d), lambda i: (0, 0)),
                  pl.BlockSpec((tm2, d), lambda i: (i, 0)),
                  row, row, row, row,
                  pl.BlockSpec((d, LANES), lambda i: (0, 0)),
                  pl.BlockSpec((1, LANES), lambda i: (0, 0)),
                  pl.BlockSpec((tm2, tm2), lambda i: (0, 0))],
        out_specs=[pl.BlockSpec((tm2, d), lambda i: (i, 0)),
                   pl.BlockSpec((tm2, d // 2), lambda i: (i, 0)),
                   pl.BlockSpec((tm2, LANES), lambda i: (i, 0)),
                   pl.BlockSpec((SUBLANES, LANES), lambda i: (0, 0))],
        out_shape=[jax.ShapeDtypeStruct((t, d), F32), jax.ShapeDtypeStruct((t, d // 2), jnp.int32),
                   jax.ShapeDtypeStruct((t, LANES), F32), jax.ShapeDtypeStruct((SUBLANES, LANES), F32)],
        scratch_shapes=[pltpu.VMEM((SUBLANES, LANES), F32)],
        compiler_params=_params(("arbitrary",)),
        name="residual_proj",
    )(merged, wo, x, gt.reshape(1, d), g_ffn.reshape(1, d), sc_ffn.reshape(1, d), sh_ffn.reshape(1, d),
      w_r, b_r.reshape(1, LANES), _tril_ones(tm2))


def _moe_up_kernel(be_ref, nu_ref, tok_ref, nxt_ref, h_ref, wg_ref, wu_ref, o_ref, xbuf, wg_bf, wu_bf, sem):
    i = pl.program_id(0)
    n_used = nu_ref[0]
    rows = xbuf.shape[1]
    slot = lax.rem(i, 2)

    def row_copy(s, r, src_row):
        return pltpu.make_async_copy(h_ref.at[pl.ds(src_row, 1)], xbuf.at[s, pl.ds(r, 1)], sem.at[s])

    def start_block(s, ids_ref):
        for r in range(rows):
            row_copy(s, r, ids_ref[0, 0, r]).start(priority=r % DMA_PRIORITIES)

    def wait_block(s):
        def body(r, c):
            row_copy(s, r, 0).wait()
            return c

        lax.fori_loop(0, rows, body, 0, unroll=8)

    @pl.when((i == 0) & (n_used > 0))
    def _():
        start_block(0, tok_ref)

    @pl.when(i + 1 < n_used)
    def _():
        start_block(1 - slot, nxt_ref)

    prev = be_ref[jnp.maximum(i - 1, 0)]

    @pl.when((i == 0) | (be_ref[i] != prev))
    def _():
        wg_bf[...] = wg_ref[0, 0].astype(BF16)
        wu_bf[...] = wu_ref[0, 0].astype(BF16)

    @pl.when(i < n_used)
    def _():
        wait_block(slot)
        x = _bf(_unpack_pairs(xbuf[slot]))
        g = jnp.dot(x, wg_bf[...], preferred_element_type=F32)
        u = jnp.dot(x, wu_bf[...], preferred_element_type=F32)
        o_ref[...] = (_silu(g) * u).astype(o_ref.dtype)

    @pl.when(i >= n_used)
    def _():
        o_ref[...] = jnp.zeros(o_ref.shape, o_ref.dtype)


def _moe_up(h, row_token, w_gate, w_up, layer, block_expert, n_used, n_blocks):
    d, ff = w_gate.shape[2], w_gate.shape[3]
    bm = MOE_ROWS
    tok = row_token.reshape(n_blocks, 1, bm)
    last = n_blocks - 1
    return pl.pallas_call(
        _moe_up_kernel,
        grid_spec=pltpu.PrefetchScalarGridSpec(
            num_scalar_prefetch=2,
            grid=(n_blocks,),
            in_specs=[pl.BlockSpec((1, 1, bm), lambda i, be, nu: (i, 0, 0), memory_space=pltpu.SMEM),
                      pl.BlockSpec((1, 1, bm), lambda i, be, nu: (jnp.minimum(i + 1, last), 0, 0),
                                   memory_space=pltpu.SMEM),
                      pl.BlockSpec(memory_space=pl.ANY),
                      pl.BlockSpec((1, 1, d, ff), lambda i, be, nu: (layer, be[i], 0, 0)),
                      pl.BlockSpec((1, 1, d, ff), lambda i, be, nu: (layer, be[i], 0, 0))],
            out_specs=pl.BlockSpec((bm, ff), lambda i, be, nu: (i, 0)),
            scratch_shapes=[pltpu.VMEM((2, bm, d // 2), jnp.int32),
                            pltpu.VMEM((d, ff), BF16), pltpu.VMEM((d, ff), BF16),
                            pltpu.SemaphoreType.DMA((2,))]),
        out_shape=jax.ShapeDtypeStruct((n_blocks * bm, ff), BF16),
        compiler_params=_params(("arbitrary",)),
        name="moe_up",
    )(block_expert, n_used, tok, tok, h, w_gate, w_up)


def _moe_down_kernel(be_ref, nu_ref, x_ref, wd_ref, o_ref, wd_bf):
    i = pl.program_id(0)
    prev = be_ref[jnp.maximum(i - 1, 0)]

    @pl.when((i == 0) | (be_ref[i] != prev))
    def _():
        wd_bf[...] = wd_ref[0, 0].astype(BF16)

    @pl.when(i < nu_ref[0])
    def _():
        o_ref[...] = _pack_pairs(jnp.dot(x_ref[...], wd_bf[...], preferred_element_type=F32))

    @pl.when(i >= nu_ref[0])
    def _():
        o_ref[...] = jnp.zeros(o_ref.shape, o_ref.dtype)


def _moe_down(hmid, w_down, layer, block_expert, n_used, n_blocks):
    ff = hmid.shape[1]
    d = w_down.shape[3]
    bm = MOE_ROWS
    return pl.pallas_call(
        _moe_down_kernel,
        grid_spec=pltpu.PrefetchScalarGridSpec(
            num_scalar_prefetch=2,
            grid=(n_blocks,),
            in_specs=[pl.BlockSpec((bm, ff), lambda i, be, nu: (i, 0)),
                      pl.BlockSpec((1, 1, ff, d), lambda i, be, nu: (layer, be[i], 0, 0))],
            out_specs=pl.BlockSpec((bm, d // 2), lambda i, be, nu: (i, 0)),
            scratch_shapes=[pltpu.VMEM((ff, d), BF16)]),
        out_shape=jax.ShapeDtypeStruct((n_blocks * bm, d // 2), jnp.int32),
        compiler_params=_params(("arbitrary",)),
        name="moe_down",
    )(block_expert, n_used, hmid, w_down)


def _combine_kernel(dest_ref, nxt_ref, w_ref, x_ref, gt_ref, g_ref, sc_ref, sh_ref, ys_ref, *rest, emit_x, modulate):
    if emit_x:
        xo_ref, h_ref, buf, sem = rest
    else:
        h_ref, buf, sem = rest
    tc = x_ref.shape[0]
    i = pl.program_id(0)
    slot = lax.rem(i, 2)

    def row_copy(s, tok, choice, src_row):
        return pltpu.make_async_copy(ys_ref.at[pl.ds(src_row, 1)], buf.at[s, choice, pl.ds(tok, 1)], sem.at[s])

    def start_tile(s, ids_ref):
        for tok in range(tc):
            for choice in range(MOE_TOP_K):
                row_copy(s, tok, choice, ids_ref[0, 0, MOE_TOP_K * tok + choice]).start(priority=choice)

    def wait_tile(s):
        def body(j, c):
            row_copy(s, 0, 0, 0).wait()
            return c

        lax.fori_loop(0, MOE_TOP_K * tc, body, 0, unroll=8)

    @pl.when(i == 0)
    def _():
        start_tile(0, dest_ref)

    @pl.when(i + 1 < pl.num_programs(0))
    def _():
        start_tile(1 - slot, nxt_ref)

    wait_tile(slot)
    w = w_ref[...]
    y = w[:, 0:1] * _unpack_pairs(buf[slot, 0]) + w[:, 1:2] * _unpack_pairs(buf[slot, 1])
    x_new = x_ref[...] + gt_ref[...] * y
    if emit_x:
        xo_ref[...] = x_new
    h_ref[...] = _rms_mod(x_new, g_ref, sc_ref, sh_ref, modulate).astype(h_ref.dtype)


def _moe_combine(x, ys, dest, weights, gt, g, sc, sh, *, emit_x, modulate, h_dtype):
    t, d = x.shape
    tc = min(256, t)
    n_tiles = t // tc
    ids = dest.reshape(n_tiles, 1, MOE_TOP_K * tc)
    row = pl.BlockSpec((1, d), lambda i: (0, 0))
    tile = pl.BlockSpec((tc, d), lambda i: (i, 0))
    out_specs = [tile]
    out_shape = [jax.ShapeDtypeStruct((t, d), h_dtype)]
    if emit_x:
        out_specs = [tile, tile]
        out_shape = [jax.ShapeDtypeStruct((t, d), F32)] + out_shape
    outs = pl.pallas_call(
        functools.partial(_combine_kernel, emit_x=emit_x, modulate=modulate),
        grid=(n_tiles,),
        in_specs=[pl.BlockSpec((1, 1, MOE_TOP_K * tc), lambda i: (i, 0, 0), memory_space=pltpu.SMEM),
                  pl.BlockSpec((1, 1, MOE_TOP_K * tc), lambda i: (jnp.minimum(i + 1, n_tiles - 1), 0, 0),
                               memory_space=pltpu.SMEM),
                  pl.BlockSpec((tc, MOE_TOP_K), lambda i: (i, 0)),
                  tile, row, row, row, row,
                  pl.BlockSpec(memory_space=pl.ANY)],
        out_specs=out_specs,
        out_shape=out_shape,
        scratch_shapes=[pltpu.VMEM((2, MOE_TOP_K, tc, d // 2), jnp.int32), pltpu.SemaphoreType.DMA((2,))],
        compiler_params=_params(("arbitrary",)),
        name="moe_combine",
    )(ids, ids, weights, x, gt.reshape(1, d), g.reshape(1, d), sc.reshape(1, d), sh.reshape(1, d), ys)
    if emit_x:
        return outs[0], outs[1]
    return None, outs[0]


def _route_rows(logits, tril_ref, run_ref):
    lane = lax.broadcasted_iota(jnp.int32, logits.shape, 1)
    neg = -jnp.inf
    far = jnp.int32(LANES)

    def first_max(vals):
        top = jnp.max(vals, axis=-1, keepdims=True)
        return top, jnp.min(jnp.where(vals == top, lane, far), axis=-1, keepdims=True)

    is_group = lane < MOE_GROUPS
    g_top, g_idx = first_max(jnp.where(is_group, logits, neg))
    p_group = 1.0 / jnp.sum(jnp.where(is_group, jnp.exp(logits - g_top), 0.0), axis=-1, keepdims=True)
    first = MOE_GROUPS + MOE_EPG * g_idx
    cand = jnp.where((lane >= first) & (lane < first + MOE_EPG), logits, neg)
    e1, i1 = first_max(cand)
    e2, i2 = first_max(jnp.where(lane == i1, neg, cand))
    z = jnp.exp(e2 - e1)
    w1 = p_group / (1.0 + z)
    w2 = p_group * z / (1.0 + z)
    x1 = i1 - MOE_GROUPS
    x2 = i2 - MOE_GROUPS
    hit1 = lane == x1
    hit2 = lane == x2
    onehot = (hit1 | hit2).astype(F32)
    cum = jnp.dot(tril_ref[...], _bf(onehot), preferred_element_type=F32) + run_ref[0:1, :]
    rank1 = jnp.sum(jnp.where(hit1, cum, 0.0), axis=-1, keepdims=True) - 1.0
    rank2 = jnp.sum(jnp.where(hit2, cum, 0.0), axis=-1, keepdims=True) - 1.0
    last = cum[cum.shape[0] - 1:cum.shape[0], :]
    run_ref[...] = jnp.broadcast_to(last, run_ref.shape)
    cols = (w1, w2, x1.astype(F32), x2.astype(F32), rank1, rank2)
    packed = jnp.zeros(logits.shape, F32)
    for j, cvals in enumerate(cols):
        packed = jnp.where(lane == j, cvals, packed)
    return packed, last


def _dispatch_plan(expert, rank, counts, n_blocks):
    bm = MOE_ROWS
    flat = expert.reshape(-1)
    n_assign = flat.shape[0]
    padded = (counts + bm - 1) // bm * bm
    pad_end = jnp.cumsum(padded)
    dest = (pad_end - padded)[flat] + rank.reshape(-1)
    row_token = jnp.zeros((n_blocks * bm,), jnp.int32).at[dest].set(
        jnp.arange(n_assign, dtype=jnp.int32) // MOE_TOP_K)
    block_start = jnp.arange(n_blocks, dtype=jnp.int32) * bm
    block_expert = jnp.minimum(jnp.sum((pad_end[None, :] <= block_start[:, None]).astype(jnp.int32), axis=1),
                               MOE_EXPERTS - 1)
    n_used = (pad_end[-1] // bm).reshape(1)
    return dest.astype(jnp.int32), row_token, block_expert.astype(jnp.int32), n_used.astype(jnp.int32)


def _router_params(lp, d):
    w_r = jnp.zeros((d, LANES), F32).at[:, :MOE_GROUPS].set(lp['moe_gw'])
    w_r = w_r.at[:, MOE_GROUPS:MOE_GROUPS + MOE_EXPERTS].set(lp['moe_ew'])
    b_r = jnp.zeros((LANES,), F32).at[:MOE_GROUPS].set(lp['moe_gb'])
    b_r = b_r.at[MOE_GROUPS:MOE_GROUPS + MOE_EXPERTS].set(lp['moe_eb'])
    return w_r.astype(BF16), b_r


def _hier_moe(x, h, route, count_rows, gt, layer, ex_gate, ex_up, ex_down, next_norm):
    t, d = x.shape
    weights = route[:, 0:2]
    expert = route[:, 2:4].astype(jnp.int32)
    rank = route[:, 4:6].astype(jnp.int32)
    counts = count_rows[0, :MOE_EXPERTS].astype(jnp.int32)
    n_blocks = (t * MOE_TOP_K) // MOE_ROWS + MOE_EXPERTS
    dest, row_token, block_expert, n_used = _dispatch_plan(expert, rank, counts, n_blocks)
    hmid = _moe_up(h, row_token, ex_gate, ex_up, layer, block_expert, n_used, n_blocks)
    ys = _moe_down(hmid, ex_down, layer, block_expert, n_used, n_blocks)
    return _moe_combine(x, ys, dest, weights, gt, *next_norm[:3], emit_x=next_norm[3], modulate=next_norm[3],
                        h_dtype=BF16 if next_norm[3] else F32)


def kernel(x, c, ada_w, ada_b, norm_mix, norm_ffn, w_in, rw_mu, rw_w0, rw_w2, rw_a0, rw_a2, rw_g2, rw_kk, rw_ka, rw_rk, rw_lnw, rw_lnb, rw_v0, rw_v1, rw_v2, ml_conv_w, ml_conv_b, ml_ib, ml_fb, ml_norm, hg_lb, hg_norm, p_a, p_b, p_c, w_out, moe_gw, moe_gb, moe_ew, moe_eb, ex_gate, ex_up, ex_down, final_norm):
    batch, t, d = x.shape
    assert batch == 1 and d == D_MODEL and t % 512 == 0
    depth = ada_w.shape[0]
    x = x.reshape(t, d)
    lbs = jnp.cumsum(jax.nn.softmax(hg_lb.astype(F32), axis=0), axis=0)
    lbs = lbs - lbs[:1]
    mod = _adaln_mod(c, ada_w, ada_b)
    zeros_d = jnp.zeros((d,), F32)
    c_ml = RW_COLS
    c_mg = c_ml + 2 * ML_QK_WIDTH + 2 * ML_WIDTH
    c_hg = c_mg + 2 * ML_HEADS
    c_gt = c_hg + 4 * HG_WIDTH
    v_first = None
    for l in range(depth):
        sh_m, sc_m, gt_m, sh_f, sc_f, gt_f = jnp.split(mod[l], 6)
        lp = dict(rw_mu=rw_mu[l], rw_w0=rw_w0[l], rw_w2=rw_w2[l], rw_a0=rw_a0[l], rw_a2=rw_a2[l], rw_g2=rw_g2[l],
                  rw_kk=rw_kk[l], rw_ka=rw_ka[l], rw_rk=rw_rk[l], rw_lnw=rw_lnw[l], rw_lnb=rw_lnb[l],
                  ml_conv_w=ml_conv_w[l], ml_conv_b=ml_conv_b[l], ml_ib=ml_ib[l], ml_fb=ml_fb[l],
                  ml_norm=ml_norm[l], hg_norm=hg_norm[l],
                  moe_gw=moe_gw[l], moe_gb=moe_gb[l], moe_ew=moe_ew[l], moe_eb=moe_eb[l])
        if l > 0:
            lp.update(rw_v0=rw_v0[l - 1], rw_v1=rw_v1[l - 1], rw_v2=rw_v2[l - 1])
        if l == 0:
            h = _norm_mod(x, norm_mix[l], sc_m, sh_m)
        wl = w_in[l]
        u_rw = _mm(h, wl[:, :c_ml].astype(BF16), out_dtype=BF16, tn=c_ml)
        u_ml = _mm(h, wl[:, c_ml:c_mg].astype(BF16), out_dtype=BF16, tn=ML_WIDTH)
        w_mg = jnp.zeros((d, LANES), F32).at[:, :2 * ML_HEADS].set(wl[:, c_mg:c_hg])
        u_mg = _mm(h, w_mg.astype(BF16), out_dtype=F32, tn=LANES)
        u_hg = _mm(h, wl[:, c_hg:c_gt].astype(BF16), out_dtype=BF16, tn=ML_WIDTH)
        y_a, v_first = _rwkv_branch(u_rw, v_first, lp)
        y_b = _mlstm_branch(u_ml, u_mg, lp)
        y_c = _hgrn_branch(u_hg, lbs[l], lp)
        x, h, route, count_rows = _merge_out(
            h, y_a, y_b, y_c, wl[:, c_gt:].astype(BF16), p_a[l].astype(BF16), p_b[l].astype(BF16),
            p_c[l].astype(BF16), w_out[l].astype(BF16), x, gt_m, norm_ffn[l], sc_f, sh_f, *_router_params(lp, d))
        if l + 1 < depth:
            nxt = jnp.split(mod[l + 1], 6)
            next_norm = (norm_mix[l + 1], nxt[1], nxt[0], True)
        else:
            next_norm = (final_norm, zeros_d, zeros_d, False)
        x, h = _hier_moe(x, h, route, count_rows, gt_f, l, ex_gate, ex_up, ex_down, next_norm)
    return h.reshape(batch, t, d)
```

```python
import functools

import numpy as np
import jax
import jax.numpy as jnp
from jax import lax
from jax.experimental import pallas as pl
from jax.experimental.pallas import tpu as pltpu

F32 = jnp.float32
BF16 = jnp.bfloat16

D_MODEL = 2048
RW_HEADS, RW_HEAD_DIM, RW_WIDTH = 8, 64, 512
RW_LORA_DECAY, RW_LORA_ICLR, RW_LORA_GATE = 64, 64, 128
RW_LORA = RW_LORA_DECAY + RW_LORA_ICLR + RW_LORA_GATE
RW_COLS = 3 * RW_WIDTH + RW_LORA
RW_GN_EPS = 64e-5
RW_GROUP = 4
RW_CHUNKS_PER_STEP = 4
ML_HEADS, ML_QK_DIM, ML_V_DIM = 4, 128, 256
ML_QK_WIDTH, ML_WIDTH = 512, 1024
ML_CONV = 4
ML_SOFTCAP = 15.0
ML_CHUNKS_PER_STEP = 2
HG_HEADS, HG_DIM, HG_WIDTH = 4, 128, 512
HG_CHUNK = 128
CHUNK = 64
MOE_GROUPS, MOE_EPG, MOE_EXPERTS, MOE_TOP_K = 4, 8, 32, 2
NORM_EPS = 1e-6

LANES = 128
SUBLANES = 8
MOE_ROWS = 256
DMA_PRIORITIES = 2
WEIGHT_STREAMS = 2
VMEM_LIMIT = 56 * 1024 * 1024


def _params(sem, limit=VMEM_LIMIT):
    return pltpu.CompilerParams(dimension_semantics=sem, vmem_limit_bytes=limit)


def _bf(x):
    return x.astype(BF16)


def _dot(a, b):
    return jnp.dot(_bf(a), _bf(b), preferred_element_type=F32)


def _dot_nt(a, b):
    return lax.dot_general(_bf(a), _bf(b), (((1,), (1,)), ((), ())), preferred_element_type=F32)


def _dot_tn(a, b):
    return lax.dot_general(_bf(a), _bf(b), (((0,), (0,)), ((), ())), preferred_element_type=F32)


def _bf16_pieces(x, pieces):
    out = []
    rest = x
    for _ in range(pieces):
        part = rest.astype(BF16)
        out.append(part)
        rest = rest - part.astype(F32)
    return out


def _sel_dot(sel, x, pieces):
    acc = None
    for part in _bf16_pieces(x, pieces):
        term = jnp.dot(sel, part, preferred_element_type=F32)
        acc = term if acc is None else acc + term
    return acc


def _seg_sum(x, bd):
    acc = None
    for part in _bf16_pieces(x, 2):
        term = jnp.dot(part, bd, preferred_element_type=F32)
        acc = term if acc is None else acc + term
    return acc


def _sigmoid(x):
    return jax.nn.sigmoid(x)


def _silu(x):
    return x * jax.nn.sigmoid(x)


def _log_sigmoid(x):
    return jnp.minimum(x, 0.0) - jnp.log1p(jnp.exp(-jnp.abs(x)))


def _softplus(x):
    return jnp.maximum(x, 0.0) + jnp.log1p(jnp.exp(-jnp.abs(x)))


def _mod_kernel(c_ref, w0_ref, w1_ref, b_ref, o_ref):
    cond = _silu(c_ref[...])
    for s, w_ref in enumerate((w0_ref, w1_ref)):
        o_ref[0, s:s + 1, :] = jnp.sum(w_ref[0] * cond, axis=0, keepdims=True) + b_ref[0, s:s + 1, :]


def _adaln_mod(c, ada_w, ada_b):
    depth, d, n = ada_w.shape
    tn = 1024
    nb = n // (WEIGHT_STREAMS * tn)
    out = pl.pallas_call(
        _mod_kernel,
        grid=(depth, nb),
        in_specs=[pl.BlockSpec((d, 1), lambda l, j: (0, 0)),
                  pl.BlockSpec((1, d, tn), lambda l, j: (l, 0, j)),
                  pl.BlockSpec((1, d, tn), lambda l, j: (l, 0, nb + j)),
                  pl.BlockSpec((1, WEIGHT_STREAMS, tn), lambda l, j: (l, 0, j))],
        out_specs=pl.BlockSpec((1, WEIGHT_STREAMS, tn), lambda l, j: (l, 0, j)),
        out_shape=jax.ShapeDtypeStruct((depth, WEIGHT_STREAMS, n // WEIGHT_STREAMS), F32),
        compiler_params=_params(("arbitrary", "arbitrary")),
        name="adaln_mod",
    )(c.reshape(d, 1), ada_w, ada_w, ada_b.reshape(depth, WEIGHT_STREAMS, n // WEIGHT_STREAMS))
    return out.reshape(depth, n)


HI_HALF = -65536


def _pack_pairs(y):
    n = y.shape[1] // 2
    bits = lax.bitcast_convert_type(y.astype(BF16).astype(F32), jnp.int32)
    return lax.shift_right_logical(bits[:, :n], 16) | (bits[:, n:] & HI_HALF)


def _unpack_pairs(p):
    lo = lax.bitcast_convert_type(lax.shift_left(p, 16), F32)
    hi = lax.bitcast_convert_type(p & HI_HALF, F32)
    return jnp.concatenate([lo, hi], axis=1)


def _rms_mod(x, g_ref, sc_ref, sh_ref, modulate):
    y = x * lax.rsqrt(jnp.mean(x * x, axis=-1, keepdims=True) + NORM_EPS) * g_ref[...]
    if modulate:
        y = y * (1.0 + sc_ref[...]) + sh_ref[...]
    return y


def _norm_kernel(x_ref, g_ref, sc_ref, sh_ref, o_ref):
    o_ref[...] = _rms_mod(x_ref[...], g_ref, sc_ref, sh_ref, True).astype(o_ref.dtype)


def _norm_mod(x, g, sc, sh):
    t, d = x.shape
    tm = min(512, t)
    row = pl.BlockSpec((1, d), lambda i: (0, 0))
    return pl.pallas_call(
        _norm_kernel,
        grid=(t // tm,),
        in_specs=[pl.BlockSpec((tm, d), lambda i: (i, 0)), row, row, row],
        out_specs=pl.BlockSpec((tm, d), lambda i: (i, 0)),
        out_shape=jax.ShapeDtypeStruct((t, d), BF16),
        compiler_params=_params(("arbitrary",)),
        name="rmsnorm_mod",
    )(x, g.reshape(1, d), sc.reshape(1, d), sh.reshape(1, d))


def _mm_kernel(a_ref, w_ref, o_ref):
    o_ref[...] = _dot(a_ref[...], w_ref[...]).astype(o_ref.dtype)


def _row_tile(m):
    return 1024 if m % 1024 == 0 else min(512, m)


def _mm(a, w, *, out_dtype, tn):
    m, k = a.shape
    n = w.shape[1]
    tm = _row_tile(m)
    return pl.pallas_call(
        _mm_kernel,
        grid=(n // tn, m // tm),
        in_specs=[pl.BlockSpec((tm, k), lambda j, i: (i, 0)),
                  pl.BlockSpec((k, tn), lambda j, i: (0, j))],
        out_specs=pl.BlockSpec((tm, tn), lambda j, i: (i, j)),
        out_shape=jax.ShapeDtypeStruct((m, n), out_dtype),
        compiler_params=_params(("arbitrary", "arbitrary")),
        name="dense_matmul",
    )(a, w)


def _rwkv_kernel(*refs, has_vres):
    if has_vres:
        (u_ref, vf_ref, mu_ref, wl_ref, w0_ref, a0_ref, kk_ref, ka_ref, rk_ref, lnw_ref, lnb_ref,
         v0_ref, v1_ref, v2_ref, bd_ref, tril_ref, y_ref, xbuf, st_ref) = refs
    else:
        (u_ref, mu_ref, wl_ref, w0_ref, a0_ref, kk_ref, ka_ref, rk_ref, lnw_ref, lnb_ref,
         bd_ref, tril_ref, y_ref, vf_out_ref, xbuf, st_ref) = refs
    L = CHUNK
    W = RW_WIDTH
    TB = u_ref.shape[0]
    n_chunks = TB // L

    @pl.when(pl.program_id(0) == 0)
    def _():
        xbuf[0:SUBLANES, :] = jnp.zeros((SUBLANES, RW_COLS), F32)
        st_ref[...] = jnp.zeros(st_ref.shape, F32)

    u = u_ref[...].astype(F32)
    xbuf[SUBLANES:SUBLANES + TB, :] = u
    prev = xbuf[SUBLANES - 1:SUBLANES - 1 + TB, :]
    xbuf[0:SUBLANES, :] = u[TB - SUBLANES:TB, :]
    xs = u + mu_ref[...] * (prev - u)
    r = xs[:, 0:W]
    k = xs[:, W:2 * W]
    v = xs[:, 2 * W:3 * W]
    lr = xs[:, 3 * W:3 * W + RW_LORA]
    lane = lax.broadcasted_iota(jnp.int32, lr.shape, 1)
    act = jnp.where(lane < RW_LORA_DECAY, jnp.tanh(lr),
                    jnp.where(lane < RW_LORA_DECAY + RW_LORA_ICLR, lr, _sigmoid(lr)))
    lo = _dot(act, wl_ref[...])
    z = w0_ref[...] + lo[:, 0:W]
    log_w = -_softplus(-z) - 0.5
    ld = -jnp.exp(log_w)
    iclr = _sigmoid(a0_ref[...] + lo[:, W:2 * W])
    gate = lo[:, 2 * W:3 * W]
    if has_vres:
        vv = _dot(_dot(v, v1_ref[...]), v2_ref[...])
        v = v + (vf_ref[...] - v) * _sigmoid(v0_ref[...] + vv)
    else:
        vf_out_ref[...] = v
    bd = bd_ref[...]
    kk = k * kk_ref[...]
    k = k * (1.0 + (iclr - 1.0) * ka_ref[...])
    sums = _seg_sum(jnp.concatenate([kk * kk, r * k * rk_ref[...]], axis=0), bd)
    kk = kk / jnp.maximum(jnp.sqrt(sums[0:TB, :]), 1e-12)
    a = -kk
    b = kk * iclr
    bonus = sums[TB:2 * TB, :] * v

    cum = _sel_dot(tril_ref[...], ld, 2)
    cl_rows = [cum[(c + 1) * L - 1:(c + 1) * L, :] for c in range(n_chunks)]
    cl = jnp.concatenate([jnp.broadcast_to(x, (L, W)) for x in cl_rows], axis=0)
    e_neg = jnp.exp(-cum)
    e_last = jnp.exp(cl - cum)
    a_t = a * jnp.exp(cum - ld)
    r_t = r * jnp.exp(cum)
    b_t = b * e_neg
    k_t = k * e_neg
    b_h = b * e_last
    k_h = k * e_last

    G = RW_GROUP
    S = G * L
    row = lax.broadcasted_iota(jnp.int32, (S, S), 0)
    col = lax.broadcasted_iota(jnp.int32, (S, S), 1)
    same = (row // L) == (col // L)
    strict = same & (row > col)
    incl = same & (row >= col)
    eye = (row == col).astype(F32)
    lane_head = lax.broadcasted_iota(jnp.int32, (1, S), 1) // RW_HEAD_DIM
    head_masks = [(lane_head == j).astype(F32) for j in range(G)]

    n_groups = RW_HEADS // G
    units = [(c, g) for c in range(n_chunks) for g in range(n_groups)]

    def unit_rows(x, c, g):
        return x[c * L:(c + 1) * L, g * S:(g + 1) * S]

    def stack(x, c, g):
        xs_ = unit_rows(x, c, g)
        return jnp.concatenate([xs_ * m for m in head_masks], axis=0)

    def tile(x, c, g):
        return jnp.concatenate([unit_rows(x, c, g)] * G, axis=0)

    a_s = {u: stack(a_t, *u) for u in units}
    r_s = {u: stack(r_t, *u) for u in units}
    v_s = {u: stack(v, *u) for u in units}
    bk_h = {u: jnp.concatenate([stack(b_h, *u), stack(k_h, *u)], axis=0) for u in units}
    xm = {u: _dot_nt(jnp.concatenate([a_s[u], r_s[u]], axis=0),
                     jnp.concatenate([tile(b_t, *u), tile(k_t, *u)], axis=0)) for u in units}
    n_ab = {u: jnp.where(strict, xm[u][0:S, 0:S], 0.0) for u in units}
    n_ak = {u: jnp.where(strict, xm[u][0:S, S:2 * S], 0.0) for u in units}
    r_bk = {u: jnp.concatenate([jnp.where(incl, xm[u][S:2 * S, 0:S], 0.0),
                                jnp.where(incl, xm[u][S:2 * S, S:2 * S], 0.0)], axis=1) for u in units}
    tinv = {u: eye + n_ab[u] for u in units}
    pw = dict(n_ab)
    for _ in range(5):
        pw = {u: _dot(pw[u], pw[u]) for u in units}
        tinv = {u: tinv[u] + _dot(tinv[u], pw[u]) for u in units}
    nv = {u: _dot(n_ak[u], v_s[u]) for u in units}
    pq = {u: _dot(tinv[u], jnp.concatenate([a_s[u], nv[u]], axis=1)) for u in units}
    p_s = {u: pq[u][:, 0:S] for u in units}
    qv = {u: jnp.concatenate([pq[u][:, S:2 * S], v_s[u]], axis=0) for u in units}
    y1 = {u: r_s[u] + _dot(r_bk[u][:, 0:S], p_s[u]) for u in units}
    y0 = {u: _dot(r_bk[u], qv[u]) for u in units}
    gt = {u: _dot_tn(p_s[u], bk_h[u][0:S, :]) for u in units}
    ht = {u: _dot_tn(qv[u], bk_h[u]) for u in units}

    state = [st_ref[g] for g in range(n_groups)]
    y_rows = []
    for c in range(n_chunks):
        w_last = jnp.exp(cl_rows[c])
        ys = []
        for g in range(n_groups):
            u = (c, g)
            st = state[g]
            y_s = _dot_nt(y1[u], st) + y0[u]
            state[g] = st * w_last[:, g * S:(g + 1) * S] + _dot(st, gt[u]) + ht[u]
            y_g = y_s[0:L, :]
            for j in range(1, G):
                y_g = y_g + y_s[j * L:(j + 1) * L, :]
            ys.append(y_g)
        y_rows.append(jnp.concatenate(ys, axis=1))
    for g in range(n_groups):
        st_ref[g] = state[g]
    y = jnp.concatenate(y_rows, axis=0)

    inv_n = 1.0 / RW_HEAD_DIM
    mean = _seg_sum(y, bd) * inv_n
    dlt = y - mean
    var = _seg_sum(dlt * dlt, bd) * inv_n
    yn = dlt * lax.rsqrt(var + RW_GN_EPS) * lnw_ref[...] + lnb_ref[...]
    y_ref[...] = ((yn + bonus) * gate).astype(y_ref.dtype)


def _blockdiag_ones(width, head):
    idx = np.arange(width) // head
    return jnp.asarray((idx[:, None] == idx[None, :]).astype(np.float32), BF16)


def _tril_ones(n, blocks=1):
    return jnp.asarray(np.kron(np.eye(blocks, dtype=np.float32), np.tril(np.ones((n, n), np.float32))), BF16)


def _rwkv_branch(u_rw, v_first, lp):
    t = u_rw.shape[0]
    L = CHUNK
    has_vres = v_first is not None
    W = RW_WIDTH
    vec = lambda a: a.reshape(1, -1).astype(F32)
    full = lambda shape: pl.BlockSpec(shape, lambda i: (0,) * len(shape))
    tb = RW_CHUNKS_PER_STEP * L
    rows = lambda width: pl.BlockSpec((tb, width), lambda i: (i, 0))
    wl = jnp.zeros((RW_LORA, 3 * W), F32)
    r0, r1 = RW_LORA_DECAY, RW_LORA_DECAY + RW_LORA_ICLR
    wl = wl.at[0:r0, 0:W].set(lp['rw_w2']).at[r0:r1, W:2 * W].set(lp['rw_a2']).at[r1:, 2 * W:].set(lp['rw_g2'])
    ins = [u_rw]
    specs = [rows(RW_COLS)]
    if has_vres:
        ins.append(v_first)
        specs.append(rows(W))
    ins += [vec(lp['rw_mu']), wl.astype(BF16), vec(lp['rw_w0']), vec(lp['rw_a0']), vec(lp['rw_kk']),
            vec(lp['rw_ka']), vec(lp['rw_rk']), vec(lp['rw_lnw']), vec(lp['rw_lnb'])]
    specs += [full((1, RW_COLS)), full((RW_LORA, 3 * W))] + [full((1, W))] * 7
    if has_vres:
        v1 = jnp.zeros((W, LANES), F32).at[:, :lp['rw_v1'].shape[1]].set(lp['rw_v1'])
        v2 = jnp.zeros((LANES, W), F32).at[:lp['rw_v2'].shape[0], :].set(lp['rw_v2'])
        ins += [vec(lp['rw_v0']), v1.astype(BF16), v2.astype(BF16)]
        specs += [full((1, W)), full((W, LANES)), full((LANES, W))]
    ins += [_blockdiag_ones(W, RW_HEAD_DIM), _tril_ones(L, RW_CHUNKS_PER_STEP)]
    specs += [full((W, W)), full((tb, tb))]
    out_shape = [jax.ShapeDtypeStruct((t, W), BF16)]
    out_specs = [rows(W)]
    if not has_vres:
        out_shape.append(jax.ShapeDtypeStruct((t, W), F32))
        out_specs.append(rows(W))
    outs = pl.pallas_call(
        functools.partial(_rwkv_kernel, has_vres=has_vres),
        grid=(t // tb,),
        in_specs=specs,
        out_specs=out_specs,
        out_shape=out_shape,
        scratch_shapes=[pltpu.VMEM((tb + SUBLANES, RW_COLS), F32),
                        pltpu.VMEM((RW_HEADS // RW_GROUP, RW_GROUP * RW_HEAD_DIM, RW_GROUP * RW_HEAD_DIM), F32)],
        compiler_params=_params(("arbitrary",)),
        name="rwkv7_branch",
    )(*ins)
    if has_vres:
        return outs[0], v_first
    return outs[0], outs[1]


def _mlstm_kernel(qk_ref, v_ref, o_ref, g_ref, cw_ref, cb_ref, gb_ref, ng_ref, tril_ref,
                  y_ref, xbuf, c_ref, m_ref):
    L = CHUNK
    TB = qk_ref.shape[0]
    n_chunks = TB // L

    @pl.when(pl.program_id(0) == 0)
    def _():
        xbuf[0:SUBLANES, :] = jnp.zeros((SUBLANES, 2 * ML_QK_WIDTH), F32)
        c_ref[...] = jnp.zeros(c_ref.shape, F32)
        m_ref[...] = jnp.zeros(m_ref.shape, F32)

    x0 = qk_ref[...].astype(F32)
    xbuf[SUBLANES:SUBLANES + TB, :] = x0
    conv = cb_ref[...] + cw_ref[ML_CONV - 1:ML_CONV, :] * x0
    for dly in range(1, ML_CONV):
        conv = conv + cw_ref[ML_CONV - 1 - dly:ML_CONV - dly, :] * xbuf[SUBLANES - dly:SUBLANES - dly + TB, :]
    xbuf[0:SUBLANES, :] = x0[TB - SUBLANES:TB, :]
    qk = _silu(conv)
    q = qk[:, :ML_QK_WIDTH]
    k = qk[:, ML_QK_WIDTH:] * (ML_QK_DIM ** -0.5)

    pre = g_ref[...] + gb_ref[...]
    cap = ML_SOFTCAP * jnp.tanh(pre / ML_SOFTCAP)
    lane = lax.broadcasted_iota(jnp.int32, pre.shape, 1)
    gates = jnp.where(lane < ML_HEADS, cap, _log_sigmoid(cap))
    csum = _sel_dot(tril_ref[...], gates, 3)
    gates_t = gates.T
    csum_t = csum.T

    row = lax.broadcasted_iota(jnp.int32, (L, L), 0)
    col = lax.broadcasted_iota(jnp.int32, (L, L), 1)
    causal = row >= col
    ones_col = (lax.broadcasted_iota(jnp.int32, (L, LANES), 1) == 0).astype(F32)
    vv = v_ref[...].astype(F32)
    og = o_ref[...].astype(F32)
    units = [(c, h) for c in range(n_chunks) for h in range(ML_HEADS)]

    def rows(c):
        return slice(c * L, (c + 1) * L)

    qh = {(c, h): q[rows(c), h * ML_QK_DIM:(h + 1) * ML_QK_DIM] for c, h in units}
    kh = {(c, h): k[rows(c), h * ML_QK_DIM:(h + 1) * ML_QK_DIM] for c, h in units}
    vh = {(c, h): jnp.concatenate([vv[rows(c), h * ML_V_DIM:(h + 1) * ML_V_DIM], ones_col], axis=1)
          for c, h in units}
    b_col = {(c, h): csum[rows(c), ML_HEADS + h:ML_HEADS + h + 1] for c, h in units}
    li_col = {(c, h): gates[rows(c), h:h + 1] for c, h in units}
    dmat = {(c, h): jnp.where(causal, b_col[c, h] - csum_t[ML_HEADS + h:ML_HEADS + h + 1, rows(c)]
                              + gates_t[h:h + 1, rows(c)], -jnp.inf) for c, h in units}
    dmax = {u: jnp.max(dmat[u], axis=-1, keepdims=True) for u in units}
    s_loc = {u: _dot_nt(qh[u], kh[u]) * jnp.exp(dmat[u] - dmax[u]) for u in units}
    sv = {u: _dot(s_loc[u], vh[u]) for u in units}
    b_last = {u: b_col[u][L - 1:L, :] for u in units}
    g = {u: b_last[u] - b_col[u] + li_col[u] for u in units}
    gmax = {u: jnp.max(g[u], axis=0, keepdims=True) for u in units}
    kv = {u: _dot_tn(jnp.exp(g[u] - gmax[u]) * kh[u], vh[u]) for u in units}

    c_state = [c_ref[h] for h in range(ML_HEADS)]
    m_state = [m_ref[h][0:1, 0:1] for h in range(ML_HEADS)]
    out_rows = []
    for c in range(n_chunks):
        outs = []
        for h in range(ML_HEADS):
            u = (c, h)
            m_prev = m_state[h]
            m_inter = b_col[u] + m_prev
            m_t = jnp.maximum(m_inter, dmax[u])
            num_aug = jnp.exp(dmax[u] - m_t) * sv[u] + jnp.exp(m_inter - m_t) * _dot(qh[u], c_state[h])
            num = num_aug[:, :ML_V_DIM]
            den = num_aug[:, ML_V_DIM:ML_V_DIM + 1]
            hh = num / jnp.maximum(jnp.abs(den), jnp.exp(-m_t))
            m_new = jnp.maximum(b_last[u] + m_prev, gmax[u])
            c_state[h] = jnp.exp(b_last[u] + m_prev - m_new) * c_state[h] + jnp.exp(gmax[u] - m_new) * kv[u]
            m_state[h] = m_new
            ng = ng_ref[:, h * ML_V_DIM:(h + 1) * ML_V_DIM]
            hn = hh * lax.rsqrt(jnp.mean(hh * hh, axis=-1, keepdims=True) + NORM_EPS) * ng
            outs.append(hn * _sigmoid(og[rows(c), h * ML_V_DIM:(h + 1) * ML_V_DIM]))
        out_rows.append(jnp.concatenate(outs, axis=1))
    for h in range(ML_HEADS):
        c_ref[h] = c_state[h]
        m_ref[h] = jnp.broadcast_to(m_state[h], (SUBLANES, LANES))
    y_ref[...] = jnp.concatenate(out_rows, axis=0).astype(y_ref.dtype)


def _mlstm_branch(u_ml, u_mg, lp):
    t = u_ml.shape[0]
    L = CHUNK
    full = lambda shape: pl.BlockSpec(shape, lambda i: (0,) * len(shape))
    gb = jnp.zeros((1, LANES), F32).at[0, 0:ML_HEADS].set(lp['ml_ib']).at[0, ML_HEADS:2 * ML_HEADS].set(lp['ml_fb'])
    tb = ML_CHUNKS_PER_STEP * L
    return pl.pallas_call(
        _mlstm_kernel,
        grid=(t // tb,),
        in_specs=[pl.BlockSpec((tb, ML_WIDTH), lambda i: (i, 0)),
                  pl.BlockSpec((tb, ML_WIDTH), lambda i: (i, 1)),
                  pl.BlockSpec((tb, ML_WIDTH), lambda i: (i, 2)),
                  pl.BlockSpec((tb, LANES), lambda i: (i, 0)),
                  full((ML_CONV, 2 * ML_QK_WIDTH)), full((1, 2 * ML_QK_WIDTH)), full((1, LANES)),
                  full((1, ML_WIDTH)), full((tb, tb))],
        out_specs=pl.BlockSpec((tb, ML_WIDTH), lambda i: (i, 0)),
        out_shape=jax.ShapeDtypeStruct((t, ML_WIDTH), BF16),
        scratch_shapes=[pltpu.VMEM((tb + SUBLANES, 2 * ML_QK_WIDTH), F32),
                        pltpu.VMEM((ML_HEADS, ML_QK_DIM, ML_V_DIM + LANES), F32),
                        pltpu.VMEM((ML_HEADS, SUBLANES, LANES), F32)],
        compiler_params=_params(("arbitrary",)),
        name="mlstm_branch",
    )(u_ml, u_ml, u_ml, u_mg, lp['ml_conv_w'], lp['ml_conv_b'].reshape(1, -1), gb,
      lp['ml_norm'].reshape(1, -1), _tril_ones(L, ML_CHUNKS_PER_STEP))


def _hgrn_levels():
    L = HG_CHUNK
    t = np.arange(L)
    sel = [np.tril(np.ones((L, L), np.float32))]
    masks = []
    size = L
    while size >= 2:
        half = size // 2
        ref_row = (t // size) * size + half - 1
        sel.append((t[None, :] <= ref_row[:, None]).astype(np.float32))
        same = (t[:, None] // size) == (t[None, :] // size)
        masks.append((same & ((t[:, None] % size) >= half) & ((t[None, :] % size) < half)).astype(np.float32))
        size = half
    return np.concatenate(sel, axis=0), np.stack(masks)


def _hgrn_kernel(u_ref, la_ref, lc_ref, lb_ref, ng_ref, sel_ref, mask_ref, y_ref, s_ref, *, levels):
    L = HG_CHUNK
    W = HG_WIDTH

    @pl.when(pl.program_id(0) == 0)
    def _():
        s_ref[...] = jnp.zeros(s_ref.shape, F32)

    u = u_ref[...].astype(F32)
    q = _silu(u[:, 0:W])
    f_pre = u[:, W:2 * W]
    ii = u[:, 2 * W:3 * W]
    g_pre = u[:, 3 * W:4 * W]
    la = la_ref[...]
    lc = lc_ref[...] + _log_sigmoid(f_pre)
    log_f = jnp.maximum(la, lc) + jnp.log1p(jnp.exp(-jnp.abs(la - lc)))
    k = (1.0 - lb_ref[...]) * _sigmoid(-f_pre)

    cr = _sel_dot(sel_ref[...], log_f, 2)
    cum = cr[0:L, :]
    cl = cum[L - 1:L, :]
    q_in = q * jnp.exp(cum)
    k_out = k * jnp.exp(cl - cum)
    w_last = jnp.exp(cl)
    heads = range(HG_HEADS)
    sl = [slice(h * HG_DIM, (h + 1) * HG_DIM) for h in heads]
    att = [jnp.zeros((L, L), F32) for _ in heads]
    for lv in range(levels):
        ref = cr[(lv + 1) * L:(lv + 2) * L, :]
        qe = q * jnp.exp(jnp.minimum(cum - ref, 0.0))
        ke = k * jnp.exp(jnp.minimum(ref - cum, 0.0))
        mask = mask_ref[lv]
        att = [att[h] + mask * _dot_nt(qe[:, sl[h]], ke[:, sl[h]]) for h in heads]
    diag = [jnp.sum(q[:, sl[h]] * k[:, sl[h]], axis=-1, keepdims=True) for h in heads]
    intra = [_dot(att[h], ii[:, sl[h]]) + diag[h] * ii[:, sl[h]] for h in heads]
    kv = [_dot_tn(ii[:, sl[h]], k_out[:, sl[h]]) for h in heads]
    outs = []
    for h in heads:
        st = s_ref[h]
        o = intra[h] + _dot_nt(q_in[:, sl[h]], st)
        s_ref[h] = st * w_last[:, sl[h]] + kv[h]
        on = o * lax.rsqrt(jnp.mean(o * o, axis=-1, keepdims=True) + NORM_EPS) * ng_ref[:, sl[h]]
        outs.append(on * _silu(g_pre[:, sl[h]]))
    y_ref[...] = jnp.concatenate(outs, axis=1).astype(y_ref.dtype)


def _hgrn_branch(u_hg, lb, lp):
    t = u_hg.shape[0]
    L = HG_CHUNK
    sel, masks = _hgrn_levels()
    levels = masks.shape[0]
    full = lambda shape: pl.BlockSpec(shape, lambda i: (0,) * len(shape))
    lb = lb.reshape(1, -1).astype(F32)
    return pl.pallas_call(
        functools.partial(_hgrn_kernel, levels=levels),
        grid=(t // L,),
        in_specs=[pl.BlockSpec((L, 4 * HG_WIDTH), lambda i: (i, 0)),
                  full((1, HG_WIDTH)), full((1, HG_WIDTH)), full((1, HG_WIDTH)), full((1, HG_WIDTH)),
                  full(sel.shape), full(masks.shape)],
        out_specs=pl.BlockSpec((L, HG_WIDTH), lambda i: (i, 0)),
        out_shape=jax.ShapeDtypeStruct((t, HG_WIDTH), BF16),
        scratch_shapes=[pltpu.VMEM((HG_HEADS, HG_DIM, HG_DIM), F32)],
        compiler_params=_params(("arbitrary",)),
        name="hgrn2_branch",
    )(u_hg, jnp.log(lb), jnp.log1p(-lb), lb, lp['hg_norm'].reshape(1, -1),
      jnp.asarray(sel, BF16), jnp.asarray(masks, F32))


def _gate_merge_kernel(h_ref, ya_ref, yb_ref, yc_ref, wga_ref, wgb_ref, wgc_ref, pa_ref, pb_ref, pc_ref, o_ref):
    h = h_ref[...]
    merged = (_sigmoid(_dot(h, wga_ref[...])) * _dot(ya_ref[...], pa_ref[...])
              + _sigmoid(_dot(h, wgb_ref[...])) * _dot(yb_ref[...], pb_ref[...])
              + _sigmoid(_dot(h, wgc_ref[...])) * _dot(yc_ref[...], pc_ref[...]))
    o_ref[...] = merged.astype(o_ref.dtype)


def _residual_proj_kernel(a_ref, w_ref, x_ref, gt_ref, g_ref, sc_ref, sh_ref, wr_ref, br_ref, tril_ref,
                          xo_ref, h_ref, route_ref, count_ref, run_ref):
    @pl.when(pl.program_id(0) == 0)
    def _():
        run_ref[...] = jnp.zeros(run_ref.shape, F32)

    x_new = x_ref[...] + gt_ref[...] * _dot(a_ref[...], w_ref[...])
    xo_ref[...] = x_new
    h = _rms_mod(x_new, g_ref, sc_ref, sh_ref, True)
    h_ref[...] = _pack_pairs(h)
    route, counts = _route_rows(_dot(h, wr_ref[...]) + br_ref[...], tril_ref, run_ref)
    route_ref[...] = route
    count_ref[...] = jnp.broadcast_to(counts, count_ref.shape)


def _merge_out(h, ya, yb, yc, w_gates, pa, pb, pc, wo, x, gt, g_ffn, sc_ffn, sh_ffn, w_r, b_r):
    t, d = x.shape
    tm = _row_tile(t)
    tn = 512
    nb = d // tn
    merged = pl.pallas_call(
        _gate_merge_kernel,
        grid=(nb, t // tm),
        in_specs=[pl.BlockSpec((tm, d), lambda n, i: (i, 0)),
                  pl.BlockSpec((tm, RW_WIDTH), lambda n, i: (i, 0)),
                  pl.BlockSpec((tm, ML_WIDTH), lambda n, i: (i, 0)),
                  pl.BlockSpec((tm, HG_WIDTH), lambda n, i: (i, 0)),
                  pl.BlockSpec((d, tn), lambda n, i: (0, n)),
                  pl.BlockSpec((d, tn), lambda n, i: (0, nb + n)),
                  pl.BlockSpec((d, tn), lambda n, i: (0, 2 * nb + n)),
                  pl.BlockSpec((RW_WIDTH, tn), lambda n, i: (0, n)),
                  pl.BlockSpec((ML_WIDTH, tn), lambda n, i: (0, n)),
                  pl.BlockSpec((HG_WIDTH, tn), lambda n, i: (0, n))],
        out_specs=pl.BlockSpec((tm, tn), lambda n, i: (i, n)),
        out_shape=jax.ShapeDtypeStruct((t, d), BF16),
        compiler_params=_params(("arbitrary", "arbitrary")),
        name="gate_merge",
    )(h, ya, yb, yc, w_gates, w_gates, w_gates, pa, pb, pc)
    tm2 = min(512, t)
    row = pl.BlockSpec((1, d), lambda i: (0, 0))
    return pl.pallas_call(
        _residual_proj_kernel,
        grid=(t // tm2,),
        in_specs=[pl.BlockSpec((tm2, d), lambda i: (i, 0)),
                  pl.BlockSpec((d, d), lambda i: (0, 0)),
                  pl.BlockSpec((tm2, d), lambda i: (i, 0)),
                  row, row, row, row,
                  pl.BlockSpec((d, LANES), lambda i: (0, 0)),
                  pl.BlockSpec((1, LANES), lambda i: (0, 0)),
                  pl.BlockSpec((tm2, tm2), lambda i: (0, 0))],
        out_specs=[pl.BlockSpec((tm2, d), lambda i: (i, 0)),
                   pl.BlockSpec((tm2, d // 2), lambda i: (i, 0)),
                   pl.BlockSpec((tm2, LANES), lambda i: (i, 0)),
                   pl.BlockSpec((SUBLANES, LANES), lambda i: (0, 0))],
        out_shape=[jax.ShapeDtypeStruct((t, d), F32), jax.ShapeDtypeStruct((t, d // 2), jnp.int32),
                   jax.ShapeDtypeStruct((t, LANES), F32), jax.ShapeDtypeStruct((SUBLANES, LANES), F32)],
        scratch_shapes=[pltpu.VMEM((SUBLANES, LANES), F32)],
        compiler_params=_params(("arbitrary",)),
        name="residual_proj",
    )(merged, wo, x, gt.reshape(1, d), g_ffn.reshape(1, d), sc_ffn.reshape(1, d), sh_ffn.reshape(1, d),
      w_r, b_r.reshape(1, LANES), _tril_ones(tm2))


def _slab_index(layer, slab, i, be, nu):
    return (layer, be[i], slab, 0)


def _moe_up_kernel(be_ref, nu_ref, tok_ref, nxt_ref, h_ref, wg0_ref, wg1_ref, wu0_ref, wu1_ref, o_ref,
                   xbuf, wg_bf, wu_bf, sem):
    i = pl.program_id(0)
    n_used = nu_ref[0]
    rows = xbuf.shape[1]
    slot = lax.rem(i, 2)

    def row_copy(s, r, src_row):
        return pltpu.make_async_copy(h_ref.at[pl.ds(src_row, 1)], xbuf.at[s, pl.ds(r, 1)], sem.at[s])

    def start_block(s, ids_ref):
        for r in range(rows):
            row_copy(s, r, ids_ref[0, 0, r]).start(priority=r % DMA_PRIORITIES)

    def wait_block(s):
        def body(r, c):
            row_copy(s, r, 0).wait()
            return c

        lax.fori_loop(0, rows, body, 0, unroll=8)

    @pl.when((i == 0) & (n_used > 0))
    def _():
        start_block(0, tok_ref)

    @pl.when(i + 1 < n_used)
    def _():
        start_block(1 - slot, nxt_ref)

    prev = be_ref[jnp.maximum(i - 1, 0)]

    @pl.when((i == 0) | (be_ref[i] != prev))
    def _():
        half = wg_bf.shape[0] // WEIGHT_STREAMS
        wg_bf[0:half, :] = wg0_ref[0, 0].astype(BF16)
        wg_bf[half:, :] = wg1_ref[0, 0].astype(BF16)
        wu_bf[0:half, :] = wu0_ref[0, 0].astype(BF16)
        wu_bf[half:, :] = wu1_ref[0, 0].astype(BF16)

    @pl.when(i < n_used)
    def _():
        wait_block(slot)
        x = _bf(_unpack_pairs(xbuf[slot]))
        g = jnp.dot(x, wg_bf[...], preferred_element_type=F32)
        u = jnp.dot(x, wu_bf[...], preferred_element_type=F32)
        o_ref[...] = (_silu(g) * u).astype(o_ref.dtype)

    @pl.when(i >= n_used)
    def _():
        o_ref[...] = jnp.zeros(o_ref.shape, o_ref.dtype)


def _moe_up(h, row_token, w_gate, w_up, layer, block_expert, n_used, n_blocks):
    d, ff = w_gate.shape[2], w_gate.shape[3]
    bm = MOE_ROWS
    tok = row_token.reshape(n_blocks, 1, bm)
    last = n_blocks - 1
    return pl.pallas_call(
        _moe_up_kernel,
        grid_spec=pltpu.PrefetchScalarGridSpec(
            num_scalar_prefetch=2,
            grid=(n_blocks,),
            in_specs=[pl.BlockSpec((1, 1, bm), lambda i, be, nu: (i, 0, 0), memory_space=pltpu.SMEM),
                      pl.BlockSpec((1, 1, bm), lambda i, be, nu: (jnp.minimum(i + 1, last), 0, 0),
                                   memory_space=pltpu.SMEM),
                      pl.BlockSpec(memory_space=pl.ANY)]
                     + [pl.BlockSpec((1, 1, d // WEIGHT_STREAMS, ff), functools.partial(_slab_index, layer, s))
                        for s in range(WEIGHT_STREAMS)] * 2,
            out_specs=pl.BlockSpec((bm, ff), lambda i, be, nu: (i, 0)),
            scratch_shapes=[pltpu.VMEM((2, bm, d // 2), jnp.int32),
                            pltpu.VMEM((d, ff), BF16), pltpu.VMEM((d, ff), BF16),
                            pltpu.SemaphoreType.DMA((2,))]),
        out_shape=jax.ShapeDtypeStruct((n_blocks * bm, ff), BF16),
        compiler_params=_params(("arbitrary",)),
        name="moe_up",
    )(block_expert, n_used, tok, tok, h, w_gate, w_gate, w_up, w_up)


def _moe_down_kernel(be_ref, nu_ref, x_ref, wd0_ref, wd1_ref, o_ref, wd_bf):
    i = pl.program_id(0)
    prev = be_ref[jnp.maximum(i - 1, 0)]

    @pl.when((i == 0) | (be_ref[i] != prev))
    def _():
        half = wd_bf.shape[0] // WEIGHT_STREAMS
        wd_bf[0:half, :] = wd0_ref[0, 0].astype(BF16)
        wd_bf[half:, :] = wd1_ref[0, 0].astype(BF16)

    @pl.when(i < nu_ref[0])
    def _():
        o_ref[...] = _pack_pairs(jnp.dot(x_ref[...], wd_bf[...], preferred_element_type=F32))

    @pl.when(i >= nu_ref[0])
    def _():
        o_ref[...] = jnp.zeros(o_ref.shape, o_ref.dtype)


def _moe_down(hmid, w_down, layer, block_expert, n_used, n_blocks):
    ff = hmid.shape[1]
    d = w_down.shape[3]
    bm = MOE_ROWS
    return pl.pallas_call(
        _moe_down_kernel,
        grid_spec=pltpu.PrefetchScalarGridSpec(
            num_scalar_prefetch=2,
            grid=(n_blocks,),
            in_specs=[pl.BlockSpec((bm, ff), lambda i, be, nu: (i, 0))]
                     + [pl.BlockSpec((1, 1, ff // WEIGHT_STREAMS, d), functools.partial(_slab_index, layer, s))
                        for s in range(WEIGHT_STREAMS)],
            out_specs=pl.BlockSpec((bm, d // 2), lambda i, be, nu: (i, 0)),
            scratch_shapes=[pltpu.VMEM((ff, d), BF16)]),
        out_shape=jax.ShapeDtypeStruct((n_blocks * bm, d // 2), jnp.int32),
        compiler_params=_params(("arbitrary",)),
        name="moe_down",
    )(block_expert, n_used, hmid, w_down, w_down)


def _combine_kernel(dest_ref, nxt_ref, w_ref, x_ref, gt_ref, g_ref, sc_ref, sh_ref, ys_ref, *rest, emit_x, modulate):
    if emit_x:
        xo_ref, h_ref, buf, sem = rest
    else:
        h_ref, buf, sem = rest
    tc = x_ref.shape[0]
    i = pl.program_id(0)
    slot = lax.rem(i, 2)

    def row_copy(s, tok, choice, src_row):
        return pltpu.make_async_copy(ys_ref.at[pl.ds(src_row, 1)], buf.at[s, choice, pl.ds(tok, 1)], sem.at[s])

    def start_tile(s, ids_ref):
        for tok in range(tc):
            for choice in range(MOE_TOP_K):
                row_copy(s, tok, choice, ids_ref[0, 0, MOE_TOP_K * tok + choice]).start(priority=choice)

    def wait_tile(s):
        def body(j, c):
            row_copy(s, 0, 0, 0).wait()
            return c

        lax.fori_loop(0, MOE_TOP_K * tc, body, 0, unroll=8)

    @pl.when(i == 0)
    def _():
        start_tile(0, dest_ref)

    @pl.when(i + 1 < pl.num_programs(0))
    def _():
        start_tile(1 - slot, nxt_ref)

    wait_tile(slot)
    w = w_ref[...]
    y = w[:, 0:1] * _unpack_pairs(buf[slot, 0]) + w[:, 1:2] * _unpack_pairs(buf[slot, 1])
    x_new = x_ref[...] + gt_ref[...] * y
    if emit_x:
        xo_ref[...] = x_new
    h_ref[...] = _rms_mod(x_new, g_ref, sc_ref, sh_ref, modulate).astype(h_ref.dtype)


def _moe_combine(x, ys, dest, weights, gt, g, sc, sh, *, emit_x, modulate, h_dtype):
    t, d = x.shape
    tc = min(256, t)
    n_tiles = t // tc
    ids = dest.reshape(n_tiles, 1, MOE_TOP_K * tc)
    row = pl.BlockSpec((1, d), lambda i: (0, 0))
    tile = pl.BlockSpec((tc, d), lambda i: (i, 0))
    out_specs = [tile]
    out_shape = [jax.ShapeDtypeStruct((t, d), h_dtype)]
    if emit_x:
        out_specs = [tile, tile]
        out_shape = [jax.ShapeDtypeStruct((t, d), F32)] + out_shape
    outs = pl.pallas_call(
        functools.partial(_combine_kernel, emit_x=emit_x, modulate=modulate),
        grid=(n_tiles,),
        in_specs=[pl.BlockSpec((1, 1, MOE_TOP_K * tc), lambda i: (i, 0, 0), memory_space=pltpu.SMEM),
                  pl.BlockSpec((1, 1, MOE_TOP_K * tc), lambda i: (jnp.minimum(i + 1, n_tiles - 1), 0, 0),
                               memory_space=pltpu.SMEM),
                  pl.BlockSpec((tc, MOE_TOP_K), lambda i: (i, 0)),
                  tile, row, row, row, row,
                  pl.BlockSpec(memory_space=pl.ANY)],
        out_specs=out_specs,
        out_shape=out_shape,
        scratch_shapes=[pltpu.VMEM((2, MOE_TOP_K, tc, d // 2), jnp.int32), pltpu.SemaphoreType.DMA((2,))],
        compiler_params=_params(("arbitrary",)),
        name="moe_combine",
    )(ids, ids, weights, x, gt.reshape(1, d), g.reshape(1, d), sc.reshape(1, d), sh.reshape(1, d), ys)
    if emit_x:
        return outs[0], outs[1]
    return None, outs[0]


def _route_rows(logits, tril_ref, run_ref):
    lane = lax.broadcasted_iota(jnp.int32, logits.shape, 1)
    neg = -jnp.inf
    far = jnp.int32(LANES)

    def first_max(vals):
        top = jnp.max(vals, axis=-1, keepdims=True)
        return top, jnp.min(jnp.where(vals == top, lane, far), axis=-1, keepdims=True)

    is_group = lane < MOE_GROUPS
    g_top, g_idx = first_max(jnp.where(is_group, logits, neg))
    p_group = 1.0 / jnp.sum(jnp.where(is_group, jnp.exp(logits - g_top), 0.0), axis=-1, keepdims=True)
    first = MOE_GROUPS + MOE_EPG * g_idx
    cand = jnp.where((lane >= first) & (lane < first + MOE_EPG), logits, neg)
    e1, i1 = first_max(cand)
    e2, i2 = first_max(jnp.where(lane == i1, neg, cand))
    z = jnp.exp(e2 - e1)
    w1 = p_group / (1.0 + z)
    w2 = p_group * z / (1.0 + z)
    x1 = i1 - MOE_GROUPS
    x2 = i2 - MOE_GROUPS
    hit1 = lane == x1
    hit2 = lane == x2
    onehot = (hit1 | hit2).astype(F32)
    cum = jnp.dot(tril_ref[...], _bf(onehot), preferred_element_type=F32) + run_ref[0:1, :]
    rank1 = jnp.sum(jnp.where(hit1, cum, 0.0), axis=-1, keepdims=True) - 1.0
    rank2 = jnp.sum(jnp.where(hit2, cum, 0.0), axis=-1, keepdims=True) - 1.0
    last = cum[cum.shape[0] - 1:cum.shape[0], :]
    run_ref[...] = jnp.broadcast_to(last, run_ref.shape)
    cols = (w1, w2, x1.astype(F32), x2.astype(F32), rank1, rank2)
    packed = jnp.zeros(logits.shape, F32)
    for j, cvals in enumerate(cols):
        packed = jnp.where(lane == j, cvals, packed)
    return packed, last


def _dispatch_plan(expert, rank, counts, n_blocks):
    bm = MOE_ROWS
    flat = expert.reshape(-1)
    n_assign = flat.shape[0]
    padded = (counts + bm - 1) // bm * bm
    pad_end = jnp.cumsum(padded)
    dest = (pad_end - padded)[flat] + rank.reshape(-1)
    row_token = jnp.zeros((n_blocks * bm,), jnp.int32).at[dest].set(
        jnp.arange(n_assign, dtype=jnp.int32) // MOE_TOP_K)
    block_start = jnp.arange(n_blocks, dtype=jnp.int32) * bm
    block_expert = jnp.minimum(jnp.sum((pad_end[None, :] <= block_start[:, None]).astype(jnp.int32), axis=1),
                               MOE_EXPERTS - 1)
    n_used = (pad_end[-1] // bm).reshape(1)
    return dest.astype(jnp.int32), row_token, block_expert.astype(jnp.int32), n_used.astype(jnp.int32)


def _router_params(lp, d):
    w_r = jnp.zeros((d, LANES), F32).at[:, :MOE_GROUPS].set(lp['moe_gw'])
    w_r = w_r.at[:, MOE_GROUPS:MOE_GROUPS + MOE_EXPERTS].set(lp['moe_ew'])
    b_r = jnp.zeros((LANES,), F32).at[:MOE_GROUPS].set(lp['moe_gb'])
    b_r = b_r.at[MOE_GROUPS:MOE_GROUPS + MOE_EXPERTS].set(lp['moe_eb'])
    return w_r.astype(BF16), b_r


def _hier_moe(x, h, route, count_rows, gt, layer, ex_gate, ex_up, ex_down, next_norm):
    t, d = x.shape
    weights = route[:, 0:2]
    expert = route[:, 2:4].astype(jnp.int32)
    rank = route[:, 4:6].astype(jnp.int32)
    counts = count_rows[0, :MOE_EXPERTS].astype(jnp.int32)
    n_blocks = (t * MOE_TOP_K) // MOE_ROWS + MOE_EXPERTS
    dest, row_token, block_expert, n_used = _dispatch_plan(expert, rank, counts, n_blocks)
    hmid = _moe_up(h, row_token, ex_gate, ex_up, layer, block_expert, n_used, n_blocks)
    ys = _moe_down(hmid, ex_down, layer, block_expert, n_used, n_blocks)
    return _moe_combine(x, ys, dest, weights, gt, *next_norm[:3], emit_x=next_norm[3], modulate=next_norm[3],
                        h_dtype=BF16 if next_norm[3] else F32)


def kernel(x, c, ada_w, ada_b, norm_mix, norm_ffn, w_in, rw_mu, rw_w0, rw_w2, rw_a0, rw_a2, rw_g2, rw_kk, rw_ka, rw_rk, rw_lnw, rw_lnb, rw_v0, rw_v1, rw_v2, ml_conv_w, ml_conv_b, ml_ib, ml_fb, ml_norm, hg_lb, hg_norm, p_a, p_b, p_c, w_out, moe_gw, moe_gb, moe_ew, moe_eb, ex_gate, ex_up, ex_down, final_norm):
    batch, t, d = x.shape
    assert batch == 1 and d == D_MODEL and t % 512 == 0
    depth = ada_w.shape[0]
    x = x.reshape(t, d)
    lbs = jnp.cumsum(jax.nn.softmax(hg_lb.astype(F32), axis=0), axis=0)
    lbs = lbs - lbs[:1]
    mod = _adaln_mod(c, ada_w, ada_b)
    zeros_d = jnp.zeros((d,), F32)
    c_ml = RW_COLS
    c_mg = c_ml + 2 * ML_QK_WIDTH + 2 * ML_WIDTH
    c_hg = c_mg + 2 * ML_HEADS
    c_gt = c_hg + 4 * HG_WIDTH
    v_first = None
    for l in range(depth):
        sh_m, sc_m, gt_m, sh_f, sc_f, gt_f = jnp.split(mod[l], 6)
        lp = dict(rw_mu=rw_mu[l], rw_w0=rw_w0[l], rw_w2=rw_w2[l], rw_a0=rw_a0[l], rw_a2=rw_a2[l], rw_g2=rw_g2[l],
                  rw_kk=rw_kk[l], rw_ka=rw_ka[l], rw_rk=rw_rk[l], rw_lnw=rw_lnw[l], rw_lnb=rw_lnb[l],
                  ml_conv_w=ml_conv_w[l], ml_conv_b=ml_conv_b[l], ml_ib=ml_ib[l], ml_fb=ml_fb[l],
                  ml_norm=ml_norm[l], hg_norm=hg_norm[l],
                  moe_gw=moe_gw[l], moe_gb=moe_gb[l], moe_ew=moe_ew[l], moe_eb=moe_eb[l])
        if l > 0:
            lp.update(rw_v0=rw_v0[l - 1], rw_v1=rw_v1[l - 1], rw_v2=rw_v2[l - 1])
        if l == 0:
            h = _norm_mod(x, norm_mix[l], sc_m, sh_m)
        wl = w_in[l]
        u_rw = _mm(h, wl[:, :c_ml].astype(BF16), out_dtype=BF16, tn=c_ml)
        u_ml = _mm(h, wl[:, c_ml:c_mg].astype(BF16), out_dtype=BF16, tn=ML_WIDTH)
        w_mg = jnp.zeros((d, LANES), F32).at[:, :2 * ML_HEADS].set(wl[:, c_mg:c_hg])
        u_mg = _mm(h, w_mg.astype(BF16), out_dtype=F32, tn=LANES)
        u_hg = _mm(h, wl[:, c_hg:c_gt].astype(BF16), out_dtype=BF16, tn=ML_WIDTH)
        y_a, v_first = _rwkv_branch(u_rw, v_first, lp)
        y_b = _mlstm_branch(u_ml, u_mg, lp)
        y_c = _hgrn_branch(u_hg, lbs[l], lp)
        x, h, route, count_rows = _merge_out(
            h, y_a, y_b, y_c, wl[:, c_gt:].astype(BF16), p_a[l].astype(BF16), p_b[l].astype(BF16),
            p_c[l].astype(BF16), w_out[l].astype(BF16), x, gt_m, norm_ffn[l], sc_f, sh_f, *_router_params(lp, d))
        if l + 1 < depth:
            nxt = jnp.split(mod[l + 1], 6)
            next_norm = (norm_mix[l + 1], nxt[1], nxt[0], True)
        else:
            next_norm = (final_norm, zeros_d, zeros_d, False)
        x, h = _hier_moe(x, h, route, count_rows, gt_f, l, ex_gate, ex_up, ex_down, next_norm)
    return h.reshape(batch, t, d)
```

```python
import functools

import numpy as np
import jax
import jax.numpy as jnp
from jax import lax
from jax.experimental import pallas as pl
from jax.experimental.pallas import tpu as pltpu

F32 = jnp.float32
BF16 = jnp.bfloat16

D_MODEL = 2048
RW_HEADS, RW_HEAD_DIM, RW_WIDTH = 8, 64, 512
RW_LORA_DECAY, RW_LORA_ICLR, RW_LORA_GATE = 64, 64, 128
RW_LORA = RW_LORA_DECAY + RW_LORA_ICLR + RW_LORA_GATE
RW_COLS = 3 * RW_WIDTH + RW_LORA
RW_GN_EPS = 64e-5
RW_GROUP = 2
RW_CHUNKS_PER_STEP = 4
ML_HEADS, ML_QK_DIM, ML_V_DIM = 4, 128, 256
ML_QK_WIDTH, ML_WIDTH = 512, 1024
ML_CONV = 4
ML_SOFTCAP = 15.0
ML_CHUNKS_PER_STEP = 2
HG_HEADS, HG_DIM, HG_WIDTH = 4, 128, 512
HG_CHUNK = 128
CHUNK = 64
MOE_GROUPS, MOE_EPG, MOE_EXPERTS, MOE_TOP_K = 4, 8, 32, 2
NORM_EPS = 1e-6

LANES = 128
SUBLANES = 8
MOE_ROWS = 256
DMA_PRIORITIES = 2
VMEM_LIMIT = 56 * 1024 * 1024


def _params(sem, limit=VMEM_LIMIT):
    return pltpu.CompilerParams(dimension_semantics=sem, vmem_limit_bytes=limit)


def _bf(x):
    return x.astype(BF16)


def _dot(a, b):
    return jnp.dot(_bf(a), _bf(b), preferred_element_type=F32)


def _dot_nt(a, b):
    return lax.dot_general(_bf(a), _bf(b), (((1,), (1,)), ((), ())), preferred_element_type=F32)


def _dot_tn(a, b):
    return lax.dot_general(_bf(a), _bf(b), (((0,), (0,)), ((), ())), preferred_element_type=F32)


def _bf16_pieces(x, pieces):
    out = []
    rest = x
    for _ in range(pieces):
        part = rest.astype(BF16)
        out.append(part)
        rest = rest - part.astype(F32)
    return out


def _sel_dot(sel, x, pieces):
    acc = None
    for part in _bf16_pieces(x, pieces):
        term = jnp.dot(sel, part, preferred_element_type=F32)
        acc = term if acc is None else acc + term
    return acc


def _seg_sum(x, bd):
    acc = None
    for part in _bf16_pieces(x, 2):
        term = jnp.dot(part, bd, preferred_element_type=F32)
        acc = term if acc is None else acc + term
    return acc


def _sigmoid(x):
    return jax.nn.sigmoid(x)


def _silu(x):
    return x * jax.nn.sigmoid(x)


def _log_sigmoid(x):
    return jnp.minimum(x, 0.0) - jnp.log1p(jnp.exp(-jnp.abs(x)))


def _softplus(x):
    return jnp.maximum(x, 0.0) + jnp.log1p(jnp.exp(-jnp.abs(x)))


def _mod_kernel(c_ref, w_ref, b_ref, o_ref):
    cond = _silu(c_ref[...])
    o_ref[0] = jnp.sum(w_ref[0] * cond, axis=0, keepdims=True) + b_ref[0]


def _adaln_mod(c, ada_w, ada_b):
    depth, d, n = ada_w.shape
    tn = 2048
    out = pl.pallas_call(
        _mod_kernel,
        grid=(depth, n // tn),
        in_specs=[pl.BlockSpec((d, 1), lambda l, j: (0, 0)),
                  pl.BlockSpec((1, d, tn), lambda l, j: (l, 0, j)),
                  pl.BlockSpec((1, 1, tn), lambda l, j: (l, 0, j))],
        out_specs=pl.BlockSpec((1, 1, tn), lambda l, j: (l, 0, j)),
        out_shape=jax.ShapeDtypeStruct((depth, 1, n), F32),
        compiler_params=_params(("arbitrary", "arbitrary")),
        name="adaln_mod",
    )(c.reshape(d, 1), ada_w, ada_b.reshape(depth, 1, n))
    return out[:, 0, :]


HI_HALF = -65536


def _pack_pairs(y):
    n = y.shape[1] // 2
    bits = lax.bitcast_convert_type(y.astype(BF16).astype(F32), jnp.int32)
    return lax.shift_right_logical(bits[:, :n], 16) | (bits[:, n:] & HI_HALF)


def _unpack_pairs(p):
    lo = lax.bitcast_convert_type(lax.shift_left(p, 16), F32)
    hi = lax.bitcast_convert_type(p & HI_HALF, F32)
    return jnp.concatenate([lo, hi], axis=1)


def _rms_mod(x, g_ref, sc_ref, sh_ref, modulate):
    y = x * lax.rsqrt(jnp.mean(x * x, axis=-1, keepdims=True) + NORM_EPS) * g_ref[...]
    if modulate:
        y = y * (1.0 + sc_ref[...]) + sh_ref[...]
    return y


def _norm_kernel(x_ref, g_ref, sc_ref, sh_ref, o_ref):
    o_ref[...] = _rms_mod(x_ref[...], g_ref, sc_ref, sh_ref, True).astype(o_ref.dtype)


def _norm_mod(x, g, sc, sh):
    t, d = x.shape
    tm = min(512, t)
    row = pl.BlockSpec((1, d), lambda i: (0, 0))
    return pl.pallas_call(
        _norm_kernel,
        grid=(t // tm,),
        in_specs=[pl.BlockSpec((tm, d), lambda i: (i, 0)), row, row, row],
        out_specs=pl.BlockSpec((tm, d), lambda i: (i, 0)),
        out_shape=jax.ShapeDtypeStruct((t, d), BF16),
        compiler_params=_params(("arbitrary",)),
        name="rmsnorm_mod",
    )(x, g.reshape(1, d), sc.reshape(1, d), sh.reshape(1, d))


def _mm_kernel(a_ref, w_ref, o_ref):
    o_ref[...] = _dot(a_ref[...], w_ref[...]).astype(o_ref.dtype)


def _row_tile(m):
    return 1024 if m % 1024 == 0 else min(512, m)


def _mm(a, w, *, out_dtype, tn):
    m, k = a.shape
    n = w.shape[1]
    tm = _row_tile(m)
    return pl.pallas_call(
        _mm_kernel,
        grid=(n // tn, m // tm),
        in_specs=[pl.BlockSpec((tm, k), lambda j, i: (i, 0)),
                  pl.BlockSpec((k, tn), lambda j, i: (0, j))],
        out_specs=pl.BlockSpec((tm, tn), lambda j, i: (i, j)),
        out_shape=jax.ShapeDtypeStruct((m, n), out_dtype),
        compiler_params=_params(("arbitrary", "arbitrary")),
        name="dense_matmul",
    )(a, w)


def _rwkv_kernel(*refs, has_vres):
    if has_vres:
        (u_ref, vf_ref, mu_ref, wl_ref, w0_ref, a0_ref, kk_ref, ka_ref, rk_ref, lnw_ref, lnb_ref,
         v0_ref, v1_ref, v2_ref, bd_ref, tril_ref, y_ref, xbuf, st_ref) = refs
    else:
        (u_ref, mu_ref, wl_ref, w0_ref, a0_ref, kk_ref, ka_ref, rk_ref, lnw_ref, lnb_ref,
         bd_ref, tril_ref, y_ref, vf_out_ref, xbuf, st_ref) = refs
    L = CHUNK
    W = RW_WIDTH
    TB = u_ref.shape[0]
    n_chunks = TB // L

    @pl.when(pl.program_id(0) == 0)
    def _():
        xbuf[0:SUBLANES, :] = jnp.zeros((SUBLANES, RW_COLS), F32)
        st_ref[...] = jnp.zeros(st_ref.shape, F32)

    u = u_ref[...].astype(F32)
    xbuf[SUBLANES:SUBLANES + TB, :] = u
    prev = xbuf[SUBLANES - 1:SUBLANES - 1 + TB, :]
    xbuf[0:SUBLANES, :] = u[TB - SUBLANES:TB, :]
    xs = u + mu_ref[...] * (prev - u)
    r = xs[:, 0:W]
    k = xs[:, W:2 * W]
    v = xs[:, 2 * W:3 * W]
    lr = xs[:, 3 * W:3 * W + RW_LORA]
    lane = lax.broadcasted_iota(jnp.int32, lr.shape, 1)
    act = jnp.where(lane < RW_LORA_DECAY, jnp.tanh(lr),
                    jnp.where(lane < RW_LORA_DECAY + RW_LORA_ICLR, lr, _sigmoid(lr)))
    lo = _dot(act, wl_ref[...])
    z = w0_ref[...] + lo[:, 0:W]
    log_w = -_softplus(-z) - 0.5
    ld = -jnp.exp(log_w)
    iclr = _sigmoid(a0_ref[...] + lo[:, W:2 * W])
    gate = lo[:, 2 * W:3 * W]
    if has_vres:
        vv = _dot(_dot(v, v1_ref[...]), v2_ref[...])
        v = v + (vf_ref[...] - v) * _sigmoid(v0_ref[...] + vv)
    else:
        vf_out_ref[...] = v
    bd = bd_ref[...]
    kk = k * kk_ref[...]
    k = k * (1.0 + (iclr - 1.0) * ka_ref[...])
    sums = _seg_sum(jnp.concatenate([kk * kk, r * k * rk_ref[...]], axis=0), bd)
    kk = kk / jnp.maximum(jnp.sqrt(sums[0:TB, :]), 1e-12)
    a = -kk
    b = kk * iclr
    bonus = sums[TB:2 * TB, :] * v

    cum = _sel_dot(tril_ref[...], ld, 2)
    cl_rows = [cum[(c + 1) * L - 1:(c + 1) * L, :] for c in range(n_chunks)]
    cl = jnp.concatenate([jnp.broadcast_to(x, (L, W)) for x in cl_rows], axis=0)
    e_neg = jnp.exp(-cum)
    e_last = jnp.exp(cl - cum)
    a_t = a * jnp.exp(cum - ld)
    r_t = r * jnp.exp(cum)
    b_t = b * e_neg
    k_t = k * e_neg
    b_h = b * e_last
    k_h = k * e_last

    G = RW_GROUP
    S = G * L
    row = lax.broadcasted_iota(jnp.int32, (S, S), 0)
    col = lax.broadcasted_iota(jnp.int32, (S, S), 1)
    same = (row // L) == (col // L)
    strict = same & (row > col)
    incl = same & (row >= col)
    eye = (row == col).astype(F32)
    lane_head = lax.broadcasted_iota(jnp.int32, (1, S), 1) // RW_HEAD_DIM
    head_masks = [(lane_head == j).astype(F32) for j in range(G)]

    n_groups = RW_HEADS // G
    units = [(c, g) for c in range(n_chunks) for g in range(n_groups)]

    def unit_rows(x, c, g):
        return x[c * L:(c + 1) * L, g * S:(g + 1) * S]

    def stack(x, c, g):
        xs_ = unit_rows(x, c, g)
        return jnp.concatenate([xs_ * m for m in head_masks], axis=0)

    def tile(x, c, g):
        return jnp.concatenate([unit_rows(x, c, g)] * G, axis=0)

    a_s = {u: stack(a_t, *u) for u in units}
    r_s = {u: stack(r_t, *u) for u in units}
    v_s = {u: stack(v, *u) for u in units}
    bk_h = {u: jnp.concatenate([stack(b_h, *u), stack(k_h, *u)], axis=0) for u in units}
    xm = {u: _dot_nt(jnp.concatenate([a_s[u], r_s[u]], axis=0),
                     jnp.concatenate([tile(b_t, *u), tile(k_t, *u)], axis=0)) for u in units}
    n_ab = {u: jnp.where(strict, xm[u][0:S, 0:S], 0.0) for u in units}
    n_ak = {u: jnp.where(strict, xm[u][0:S, S:2 * S], 0.0) for u in units}
    r_bk = {u: jnp.concatenate([jnp.where(incl, xm[u][S:2 * S, 0:S], 0.0),
                                jnp.where(incl, xm[u][S:2 * S, S:2 * S], 0.0)], axis=1) for u in units}
    tinv = {u: eye + n_ab[u] for u in units}
    pw = dict(n_ab)
    for _ in range(5):
        pw = {u: _dot(pw[u], pw[u]) for u in units}
        tinv = {u: tinv[u] + _dot(tinv[u], pw[u]) for u in units}
    nv = {u: _dot(n_ak[u], v_s[u]) for u in units}
    pq = {u: _dot(tinv[u], jnp.concatenate([a_s[u], nv[u]], axis=1)) for u in units}
    p_s = {u: pq[u][:, 0:S] for u in units}
    qv = {u: jnp.concatenate([pq[u][:, S:2 * S], v_s[u]], axis=0) for u in units}
    y1 = {u: r_s[u] + _dot(r_bk[u][:, 0:S], p_s[u]) for u in units}
    y0 = {u: _dot(r_bk[u], qv[u]) for u in units}
    gt = {u: _dot_tn(p_s[u], bk_h[u][0:S, :]) for u in units}
    ht = {u: _dot_tn(qv[u], bk_h[u]) for u in units}

    state = [st_ref[g] for g in range(n_groups)]
    y_rows = []
    for c in range(n_chunks):
        w_last = jnp.exp(cl_rows[c])
        ys = []
        for g in range(n_groups):
            u = (c, g)
            st = state[g]
            y_s = _dot_nt(y1[u], st) + y0[u]
            state[g] = st * w_last[:, g * S:(g + 1) * S] + _dot(st, gt[u]) + ht[u]
            y_g = y_s[0:L, :]
            for j in range(1, G):
                y_g = y_g + y_s[j * L:(j + 1) * L, :]
            ys.append(y_g)
        y_rows.append(jnp.concatenate(ys, axis=1))
    for g in range(n_groups):
        st_ref[g] = state[g]
    y = jnp.concatenate(y_rows, axis=0)

    inv_n = 1.0 / RW_HEAD_DIM
    mean = _seg_sum(y, bd) * inv_n
    dlt = y - mean
    var = _seg_sum(dlt * dlt, bd) * inv_n
    yn = dlt * lax.rsqrt(var + RW_GN_EPS) * lnw_ref[...] + lnb_ref[...]
    y_ref[...] = ((yn + bonus) * gate).astype(y_ref.dtype)


def _blockdiag_ones(width, head):
    idx = np.arange(width) // head
    return jnp.asarray((idx[:, None] == idx[None, :]).astype(np.float32), BF16)


def _tril_ones(n, blocks=1):
    return jnp.asarray(np.kron(np.eye(blocks, dtype=np.float32), np.tril(np.ones((n, n), np.float32))), BF16)


def _rwkv_branch(u_rw, v_first, lp):
    t = u_rw.shape[0]
    L = CHUNK
    has_vres = v_first is not None
    W = RW_WIDTH
    vec = lambda a: a.reshape(1, -1).astype(F32)
    full = lambda shape: pl.BlockSpec(shape, lambda i: (0,) * len(shape))
    tb = RW_CHUNKS_PER_STEP * L
    rows = lambda width: pl.BlockSpec((tb, width), lambda i: (i, 0))
    wl = jnp.zeros((RW_LORA, 3 * W), F32)
    r0, r1 = RW_LORA_DECAY, RW_LORA_DECAY + RW_LORA_ICLR
    wl = wl.at[0:r0, 0:W].set(lp['rw_w2']).at[r0:r1, W:2 * W].set(lp['rw_a2']).at[r1:, 2 * W:].set(lp['rw_g2'])
    ins = [u_rw]
    specs = [rows(RW_COLS)]
    if has_vres:
        ins.append(v_first)
        specs.append(rows(W))
    ins += [vec(lp['rw_mu']), wl.astype(BF16), vec(lp['rw_w0']), vec(lp['rw_a0']), vec(lp['rw_kk']),
            vec(lp['rw_ka']), vec(lp['rw_rk']), vec(lp['rw_lnw']), vec(lp['rw_lnb'])]
    specs += [full((1, RW_COLS)), full((RW_LORA, 3 * W))] + [full((1, W))] * 7
    if has_vres:
        v1 = jnp.zeros((W, LANES), F32).at[:, :lp['rw_v1'].shape[1]].set(lp['rw_v1'])
        v2 = jnp.zeros((LANES, W), F32).at[:lp['rw_v2'].shape[0], :].set(lp['rw_v2'])
        ins += [vec(lp['rw_v0']), v1.astype(BF16), v2.astype(BF16)]
        specs += [full((1, W)), full((W, LANES)), full((LANES, W))]
    ins += [_blockdiag_ones(W, RW_HEAD_DIM), _tril_ones(L, RW_CHUNKS_PER_STEP)]
    specs += [full((W, W)), full((tb, tb))]
    out_shape = [jax.ShapeDtypeStruct((t, W), BF16)]
    out_specs = [rows(W)]
    if not has_vres:
        out_shape.append(jax.ShapeDtypeStruct((t, W), F32))
        out_specs.append(rows(W))
    outs = pl.pallas_call(
        functools.partial(_rwkv_kernel, has_vres=has_vres),
        grid=(t // tb,),
        in_specs=specs,
        out_specs=out_specs,
        out_shape=out_shape,
        scratch_shapes=[pltpu.VMEM((tb + SUBLANES, RW_COLS), F32),
                        pltpu.VMEM((RW_HEADS // RW_GROUP, RW_GROUP * RW_HEAD_DIM, RW_GROUP * RW_HEAD_DIM), F32)],
        compiler_params=_params(("arbitrary",)),
        name="rwkv7_branch",
    )(*ins)
    if has_vres:
        return outs[0], v_first
    return outs[0], outs[1]


def _mlstm_kernel(qk_ref, v_ref, o_ref, g_ref, cw_ref, cb_ref, gb_ref, ng_ref, tril_ref,
                  y_ref, xbuf, c_ref, m_ref):
    L = CHUNK
    TB = qk_ref.shape[0]
    n_chunks = TB // L

    @pl.when(pl.program_id(0) == 0)
    def _():
        xbuf[0:SUBLANES, :] = jnp.zeros((SUBLANES, 2 * ML_QK_WIDTH), F32)
        c_ref[...] = jnp.zeros(c_ref.shape, F32)
        m_ref[...] = jnp.zeros(m_ref.shape, F32)

    x0 = qk_ref[...].astype(F32)
    xbuf[SUBLANES:SUBLANES + TB, :] = x0
    conv = cb_ref[...] + cw_ref[ML_CONV - 1:ML_CONV, :] * x0
    for dly in range(1, ML_CONV):
        conv = conv + cw_ref[ML_CONV - 1 - dly:ML_CONV - dly, :] * xbuf[SUBLANES - dly:SUBLANES - dly + TB, :]
    xbuf[0:SUBLANES, :] = x0[TB - SUBLANES:TB, :]
    qk = _silu(conv)
    q = qk[:, :ML_QK_WIDTH]
    k = qk[:, ML_QK_WIDTH:] * (ML_QK_DIM ** -0.5)

    pre = g_ref[...] + gb_ref[...]
    cap = ML_SOFTCAP * jnp.tanh(pre / ML_SOFTCAP)
    lane = lax.broadcasted_iota(jnp.int32, pre.shape, 1)
    gates = jnp.where(lane < ML_HEADS, cap, _log_sigmoid(cap))
    csum = _sel_dot(tril_ref[...], gates, 3)
    gates_t = gates.T
    csum_t = csum.T

    row = lax.broadcasted_iota(jnp.int32, (L, L), 0)
    col = lax.broadcasted_iota(jnp.int32, (L, L), 1)
    causal = row >= col
    ones_col = (lax.broadcasted_iota(jnp.int32, (L, LANES), 1) == 0).astype(F32)
    vv = v_ref[...].astype(F32)
    og = o_ref[...].astype(F32)
    units = [(c, h) for c in range(n_chunks) for h in range(ML_HEADS)]

    def rows(c):
        return slice(c * L, (c + 1) * L)

    qh = {(c, h): q[rows(c), h * ML_QK_DIM:(h + 1) * ML_QK_DIM] for c, h in units}
    kh = {(c, h): k[rows(c), h * ML_QK_DIM:(h + 1) * ML_QK_DIM] for c, h in units}
    vh = {(c, h): jnp.concatenate([vv[rows(c), h * ML_V_DIM:(h + 1) * ML_V_DIM], ones_col], axis=1)
          for c, h in units}
    b_col = {(c, h): csum[rows(c), ML_HEADS + h:ML_HEADS + h + 1] for c, h in units}
    li_col = {(c, h): gates[rows(c), h:h + 1] for c, h in units}
    dmat = {(c, h): jnp.where(causal, b_col[c, h] - csum_t[ML_HEADS + h:ML_HEADS + h + 1, rows(c)]
                              + gates_t[h:h + 1, rows(c)], -jnp.inf) for c, h in units}
    dmax = {u: jnp.max(dmat[u], axis=-1, keepdims=True) for u in units}
    s_loc = {u: _dot_nt(qh[u], kh[u]) * jnp.exp(dmat[u] - dmax[u]) for u in units}
    sv = {u: _dot(s_loc[u], vh[u]) for u in units}
    b_last = {u: b_col[u][L - 1:L, :] for u in units}
    g = {u: b_last[u] - b_col[u] + li_col[u] for u in units}
    gmax = {u: jnp.max(g[u], axis=0, keepdims=True) for u in units}
    kv = {u: _dot_tn(jnp.exp(g[u] - gmax[u]) * kh[u], vh[u]) for u in units}

    c_state = [c_ref[h] for h in range(ML_HEADS)]
    m_state = [m_ref[h][0:1, 0:1] for h in range(ML_HEADS)]
    out_rows = []
    for c in range(n_chunks):
        outs = []
        for h in range(ML_HEADS):
            u = (c, h)
            m_prev = m_state[h]
            m_inter = b_col[u] + m_prev
            m_t = jnp.maximum(m_inter, dmax[u])
            num_aug = jnp.exp(dmax[u] - m_t) * sv[u] + jnp.exp(m_inter - m_t) * _dot(qh[u], c_state[h])
            num = num_aug[:, :ML_V_DIM]
            den = num_aug[:, ML_V_DIM:ML_V_DIM + 1]
            hh = num / jnp.maximum(jnp.abs(den), jnp.exp(-m_t))
            m_new = jnp.maximum(b_last[u] + m_prev, gmax[u])
            c_state[h] = jnp.exp(b_last[u] + m_prev - m_new) * c_state[h] + jnp.exp(gmax[u] - m_new) * kv[u]
            m_state[h] = m_new
            ng = ng_ref[:, h * ML_V_DIM:(h + 1) * ML_V_DIM]
            hn = hh * lax.rsqrt(jnp.mean(hh * hh, axis=-1, keepdims=True) + NORM_EPS) * ng
            outs.append(hn * _sigmoid(og[rows(c), h * ML_V_DIM:(h + 1) * ML_V_DIM]))
        out_rows.append(jnp.concatenate(outs, axis=1))
    for h in range(ML_HEADS):
        c_ref[h] = c_state[h]
        m_ref[h] = jnp.broadcast_to(m_state[h], (SUBLANES, LANES))
    y_ref[...] = jnp.concatenate(out_rows, axis=0).astype(y_ref.dtype)


def _mlstm_branch(u_ml, u_mg, lp):
    t = u_ml.shape[0]
    L = CHUNK
    full = lambda shape: pl.BlockSpec(shape, lambda i: (0,) * len(shape))
    gb = jnp.zeros((1, LANES), F32).at[0, 0:ML_HEADS].set(lp['ml_ib']).at[0, ML_HEADS:2 * ML_HEADS].set(lp['ml_fb'])
    tb = ML_CHUNKS_PER_STEP * L
    return pl.pallas_call(
        _mlstm_kernel,
        grid=(t // tb,),
        in_specs=[pl.BlockSpec((tb, ML_WIDTH), lambda i: (i, 0)),
                  pl.BlockSpec((tb, ML_WIDTH), lambda i: (i, 1)),
                  pl.BlockSpec((tb, ML_WIDTH), lambda i: (i, 2)),
                  pl.BlockSpec((tb, LANES), lambda i: (i, 0)),
                  full((ML_CONV, 2 * ML_QK_WIDTH)), full((1, 2 * ML_QK_WIDTH)), full((1, LANES)),
                  full((1, ML_WIDTH)), full((tb, tb))],
        out_specs=pl.BlockSpec((tb, ML_WIDTH), lambda i: (i, 0)),
        out_shape=jax.ShapeDtypeStruct((t, ML_WIDTH), BF16),
        scratch_shapes=[pltpu.VMEM((tb + SUBLANES, 2 * ML_QK_WIDTH), F32),
                        pltpu.VMEM((ML_HEADS, ML_QK_DIM, ML_V_DIM + LANES), F32),
                        pltpu.VMEM((ML_HEADS, SUBLANES, LANES), F32)],
        compiler_params=_params(("arbitrary",)),
        name="mlstm_branch",
    )(u_ml, u_ml, u_ml, u_mg, lp['ml_conv_w'], lp['ml_conv_b'].reshape(1, -1), gb,
      lp['ml_norm'].reshape(1, -1), _tril_ones(L, ML_CHUNKS_PER_STEP))


def _hgrn_levels():
    L = HG_CHUNK
    t = np.arange(L)
    sel = [np.tril(np.ones((L, L), np.float32))]
    masks = []
    size = L
    while size >= 2:
        half = size // 2
        ref_row = (t // size) * size + half - 1
        sel.append((t[None, :] <= ref_row[:, None]).astype(np.float32))
        same = (t[:, None] // size) == (t[None, :] // size)
        masks.append((same & ((t[:, None] % size) >= half) & ((t[None, :] % size) < half)).astype(np.float32))
        size = half
    return np.concatenate(sel, axis=0), np.stack(masks)


def _hgrn_kernel(u_ref, la_ref, lc_ref, lb_ref, ng_ref, sel_ref, mask_ref, y_ref, s_ref, *, levels):
    L = HG_CHUNK
    W = HG_WIDTH

    @pl.when(pl.program_id(0) == 0)
    def _():
        s_ref[...] = jnp.zeros(s_ref.shape, F32)

    u = u_ref[...].astype(F32)
    q = _silu(u[:, 0:W])
    f_pre = u[:, W:2 * W]
    ii = u[:, 2 * W:3 * W]
    g_pre = u[:, 3 * W:4 * W]
    la = la_ref[...]
    lc = lc_ref[...] + _log_sigmoid(f_pre)
    log_f = jnp.maximum(la, lc) + jnp.log1p(jnp.exp(-jnp.abs(la - lc)))
    k = (1.0 - lb_ref[...]) * _sigmoid(-f_pre)

    cr = _sel_dot(sel_ref[...], log_f, 2)
    cum = cr[0:L, :]
    cl = cum[L - 1:L, :]
    q_in = q * jnp.exp(cum)
    k_out = k * jnp.exp(cl - cum)
    w_last = jnp.exp(cl)
    heads = range(HG_HEADS)
    sl = [slice(h * HG_DIM, (h + 1) * HG_DIM) for h in heads]
    att = [jnp.zeros((L, L), F32) for _ in heads]
    for lv in range(levels):
        ref = cr[(lv + 1) * L:(lv + 2) * L, :]
        qe = q * jnp.exp(jnp.minimum(cum - ref, 0.0))
        ke = k * jnp.exp(jnp.minimum(ref - cum, 0.0))
        mask = mask_ref[lv]
        att = [att[h] + mask * _dot_nt(qe[:, sl[h]], ke[:, sl[h]]) for h in heads]
    diag = [jnp.sum(q[:, sl[h]] * k[:, sl[h]], axis=-1, keepdims=True) for h in heads]
    intra = [_dot(att[h], ii[:, sl[h]]) + diag[h] * ii[:, sl[h]] for h in heads]
    kv = [_dot_tn(ii[:, sl[h]], k_out[:, sl[h]]) for h in heads]
    outs = []
    for h in heads:
        st = s_ref[h]
        o = intra[h] + _dot_nt(q_in[:, sl[h]], st)
        s_ref[h] = st * w_last[:, sl[h]] + kv[h]
        on = o * lax.rsqrt(jnp.mean(o * o, axis=-1, keepdims=True) + NORM_EPS) * ng_ref[:, sl[h]]
        outs.append(on * _silu(g_pre[:, sl[h]]))
    y_ref[...] = jnp.concatenate(outs, axis=1).astype(y_ref.dtype)


def _hgrn_branch(u_hg, lb, lp):
    t = u_hg.shape[0]
    L = HG_CHUNK
    sel, masks = _hgrn_levels()
    levels = masks.shape[0]
    full = lambda shape: pl.BlockSpec(shape, lambda i: (0,) * len(shape))
    lb = lb.reshape(1, -1).astype(F32)
    return pl.pallas_call(
        functools.partial(_hgrn_kernel, levels=levels),
        grid=(t // L,),
        in_specs=[pl.BlockSpec((L, 4 * HG_WIDTH), lambda i: (i, 0)),
                  full((1, HG_WIDTH)), full((1, HG_WIDTH)), full((1, HG_WIDTH)), full((1, HG_WIDTH)),
                  full(sel.shape), full(masks.shape)],
        out_specs=pl.BlockSpec((L, HG_WIDTH), lambda i: (i, 0)),
        out_shape=jax.ShapeDtypeStruct((t, HG_WIDTH), BF16),
        scratch_shapes=[pltpu.VMEM((HG_HEADS, HG_DIM, HG_DIM), F32)],
        compiler_params=_params(("arbitrary",)),
        name="hgrn2_branch",
    )(u_hg, jnp.log(lb), jnp.log1p(-lb), lb, lp['hg_norm'].reshape(1, -1),
      jnp.asarray(sel, BF16), jnp.asarray(masks, F32))


def _gate_merge_kernel(h_ref, ya_ref, yb_ref, yc_ref, wga_ref, wgb_ref, wgc_ref, pa_ref, pb_ref, pc_ref, o_ref):
    h = h_ref[...]
    merged = (_sigmoid(_dot(h, wga_ref[...])) * _dot(ya_ref[...], pa_ref[...])
              + _sigmoid(_dot(h, wgb_ref[...])) * _dot(yb_ref[...], pb_ref[...])
              + _sigmoid(_dot(h, wgc_ref[...])) * _dot(yc_ref[...], pc_ref[...]))
    o_ref[...] = merged.astype(o_ref.dtype)


def _residual_proj_kernel(a_ref, w_ref, x_ref, gt_ref, g_ref, sc_ref, sh_ref, wr_ref, br_ref, tril_ref,
                          xo_ref, h_ref, route_ref, count_ref, run_ref):
    @pl.when(pl.program_id(0) == 0)
    def _():
        run_ref[...] = jnp.zeros(run_ref.shape, F32)

    x_new = x_ref[...] + gt_ref[...] * _dot(a_ref[...], w_ref[...])
    xo_ref[...] = x_new
    h = _rms_mod(x_new, g_ref, sc_ref, sh_ref, True)
    h_ref[...] = _pack_pairs(h)
    route, counts = _route_rows(_dot(h, wr_ref[...]) + br_ref[...], tril_ref, run_ref)
    route_ref[...] = route
    count_ref[...] = jnp.broadcast_to(counts, count_ref.shape)


def _merge_out(h, ya, yb, yc, w_gates, pa, pb, pc, wo, x, gt, g_ffn, sc_ffn, sh_ffn, w_r, b_r):
    t, d = x.shape
    tm = _row_tile(t)
    tn = 512
    nb = d // tn
    merged = pl.pallas_call(
        _gate_merge_kernel,
        grid=(nb, t // tm),
        in_specs=[pl.BlockSpec((tm, d), lambda n, i: (i, 0)),
                  pl.BlockSpec((tm, RW_WIDTH), lambda n, i: (i, 0)),
                  pl.BlockSpec((tm, ML_WIDTH), lambda n, i: (i, 0)),
                  pl.BlockSpec((tm, HG_WIDTH), lambda n, i: (i, 0)),
                  pl.BlockSpec((d, tn), lambda n, i: (0, n)),
                  pl.BlockSpec((d, tn), lambda n, i: (0, nb + n)),
                  pl.BlockSpec((d, tn), lambda n, i: (0, 2 * nb + n)),
                  pl.BlockSpec((RW_WIDTH, tn), lambda n, i: (0, n)),
                  pl.BlockSpec((ML_WIDTH, tn), lambda n, i: (0, n)),
                  pl.BlockSpec((HG_WIDTH, tn), lambda n, i: (0, n))],
        out_specs=pl.BlockSpec((tm, tn), lambda n, i: (i, n)),
        out_shape=jax.ShapeDtypeStruct((t, d), BF16),
        compiler_params=_params(("arbitrary", "arbitrary")),
        name="gate_merge",
    )(h, ya, yb, yc, w_gates, w_gates, w_gates, pa, pb, pc)
    tm2 = min(512, t)
    row = pl.BlockSpec((1, d), lambda i: (0, 0))
    return pl.pallas_call(
        _residual_proj_kernel,
        grid=(t // tm2,),
        in_specs=[pl.BlockSpec((tm2, d), lambda i: (i, 0)),
                  pl.BlockSpec((d, d), lambda i: (0, 0)),
                  pl.BlockSpec((tm2, d), lambda i: (i, 0)),
                  row, row, row, row,
                  pl.BlockSpec((d, LANES), lambda i: (0, 0)),
                  pl.BlockSpec((1, LANES), lambda i: (0, 0)),
                  pl.BlockSpec((tm2, tm2), lambda i: (0, 0))],
        out_specs=[pl.BlockSpec((tm2, d), lambda i: (i, 0)),
                   pl.BlockSpec((tm2, d // 2), lambda i: (i, 0)),
                   pl.BlockSpec((tm2, LANES), lambda i: (i, 0)),
                   pl.BlockSpec((SUBLANES, LANES), lambda i: (0, 0))],
        out_shape=[jax.ShapeDtypeStruct((t, d), F32), jax.ShapeDtypeStruct((t, d // 2), jnp.int32),
                   jax.ShapeDtypeStruct((t, LANES), F32), jax.ShapeDtypeStruct((SUBLANES, LANES), F32)],
        scratch_shapes=[pltpu.VMEM((SUBLANES, LANES), F32)],
        compiler_params=_params(("arbitrary",)),
        name="residual_proj",
    )(merged, wo, x, gt.reshape(1, d), g_ffn.reshape(1, d), sc_ffn.reshape(1, d), sh_ffn.reshape(1, d),
      w_r, b_r.reshape(1, LANES), _tril_ones(tm2))


def _moe_up_kernel(be_ref, nu_ref, tok_ref, nxt_ref, h_ref, wg_ref, wu_ref, o_ref, xbuf, wg_bf, wu_bf, sem):
    i = pl.program_id(0)
    n_used = nu_ref[0]
    rows = xbuf.shape[1]
    slot = lax.rem(i, 2)

    def row_copy(s, r, src_row):
        return pltpu.make_async_copy(h_ref.at[pl.ds(src_row, 1)], xbuf.at[s, pl.ds(r, 1)], sem.at[s])

    def start_block(s, ids_ref):
        for r in range(rows):
            row_copy(s, r, ids_ref[0, 0, r]).start(priority=r % DMA_PRIORITIES)

    def wait_block(s):
        def body(r, c):
            row_copy(s, r, 0).wait()
            return c

        lax.fori_loop(0, rows, body, 0, unroll=8)

    @pl.when((i == 0) & (n_used > 0))
    def _():
        start_block(0, tok_ref)

    @pl.when(i + 1 < n_used)
    def _():
        start_block(1 - slot, nxt_ref)

    prev = be_ref[jnp.maximum(i - 1, 0)]

    @pl.when((i == 0) | (be_ref[i] != prev))
    def _():
        wg_bf[...] = wg_ref[0, 0].astype(BF16)
        wu_bf[...] = wu_ref[0, 0].astype(BF16)

    @pl.when(i < n_used)
    def _():
        wait_block(slot)
        x = _bf(_unpack_pairs(xbuf[slot]))
        g = jnp.dot(x, wg_bf[...], preferred_element_type=F32)
        u = jnp.dot(x, wu_bf[...], preferred_element_type=F32)
        o_ref[...] = (_silu(g) * u).astype(o_ref.dtype)

    @pl.when(i >= n_used)
    def _():
        o_ref[...] = jnp.zeros(o_ref.shape, o_ref.dtype)


def _moe_up(h, row_token, w_gate, w_up, layer, block_expert, n_used, n_blocks):
    d, ff = w_gate.shape[2], w_gate.shape[3]
    bm = MOE_ROWS
    tok = row_token.reshape(n_blocks, 1, bm)
    last = n_blocks - 1
    return pl.pallas_call(
        _moe_up_kernel,
        grid_spec=pltpu.PrefetchScalarGridSpec(
            num_scalar_prefetch=2,
            grid=(n_blocks,),
            in_specs=[pl.BlockSpec((1, 1, bm), lambda i, be, nu: (i, 0, 0), memory_space=pltpu.SMEM),
                      pl.BlockSpec((1, 1, bm), lambda i, be, nu: (jnp.minimum(i + 1, last), 0, 0),
                                   memory_space=pltpu.SMEM),
                      pl.BlockSpec(memory_space=pl.ANY),
                      pl.BlockSpec((1, 1, d, ff), lambda i, be, nu: (layer, be[i], 0, 0)),
                      pl.BlockSpec((1, 1, d, ff), lambda i, be, nu: (layer, be[i], 0, 0))],
            out_specs=pl.BlockSpec((bm, ff), lambda i, be, nu: (i, 0)),
            scratch_shapes=[pltpu.VMEM((2, bm, d // 2), jnp.int32),
                            pltpu.VMEM((d, ff), BF16), pltpu.VMEM((d, ff), BF16),
                            pltpu.SemaphoreType.DMA((2,))]),
        out_shape=jax.ShapeDtypeStruct((n_blocks * bm, ff), BF16),
        compiler_params=_params(("arbitrary",)),
        name="moe_up",
    )(block_expert, n_used, tok, tok, h, w_gate, w_up)


def _moe_down_kernel(be_ref, nu_ref, x_ref, wd_ref, o_ref, wd_bf):
    i = pl.program_id(0)
    prev = be_ref[jnp.maximum(i - 1, 0)]

    @pl.when((i == 0) | (be_ref[i] != prev))
    def _():
        wd_bf[...] = wd_ref[0, 0].astype(BF16)

    @pl.when(i < nu_ref[0])
    def _():
        o_ref[...] = _pack_pairs(jnp.dot(x_ref[...], wd_bf[...], preferred_element_type=F32))

    @pl.when(i >= nu_ref[0])
    def _():
        o_ref[...] = jnp.zeros(o_ref.shape, o_ref.dtype)


def _moe_down(hmid, w_down, layer, block_expert, n_used, n_blocks):
    ff = hmid.shape[1]
    d = w_down.shape[3]
    bm = MOE_ROWS
    return pl.pallas_call(
        _moe_down_kernel,
        grid_spec=pltpu.PrefetchScalarGridSpec(
            num_scalar_prefetch=2,
            grid=(n_blocks,),
            in_specs=[pl.BlockSpec((bm, ff), lambda i, be, nu: (i, 0)),
                      pl.BlockSpec((1, 1, ff, d), lambda i, be, nu: (layer, be[i], 0, 0))],
            out_specs=pl.BlockSpec((bm, d // 2), lambda i, be, nu: (i, 0)),
            scratch_shapes=[pltpu.VMEM((ff, d), BF16)]),
        out_shape=jax.ShapeDtypeStruct((n_blocks * bm, d // 2), jnp.int32),
        compiler_params=_params(("arbitrary",)),
        name="moe_down",
    )(block_expert, n_used, hmid, w_down)


def _combine_kernel(dest_ref, nxt_ref, w_ref, x_ref, gt_ref, g_ref, sc_ref, sh_ref, ys_ref, *rest, emit_x, modulate):
    if emit_x:
        xo_ref, h_ref, buf, sem = rest
    else:
        h_ref, buf, sem = rest
    tc = x_ref.shape[0]
    i = pl.program_id(0)
    slot = lax.rem(i, 2)

    def row_copy(s, tok, choice, src_row):
        return pltpu.make_async_copy(ys_ref.at[pl.ds(src_row, 1)], buf.at[s, choice, pl.ds(tok, 1)], sem.at[s])

    def start_tile(s, ids_ref):
        for tok in range(tc):
            for choice in range(MOE_TOP_K):
                row_copy(s, tok, choice, ids_ref[0, 0, MOE_TOP_K * tok + choice]).start(priority=choice)

    def wait_tile(s):
        def body(j, c):
            row_copy(s, 0, 0, 0).wait()
            return c

        lax.fori_loop(0, MOE_TOP_K * tc, body, 0, unroll=8)

    @pl.when(i == 0)
    def _():
        start_tile(0, dest_ref)

    @pl.when(i + 1 < pl.num_programs(0))
    def _():
        start_tile(1 - slot, nxt_ref)

    wait_tile(slot)
    w = w_ref[...]
    y = w[:, 0:1] * _unpack_pairs(buf[slot, 0]) + w[:, 1:2] * _unpack_pairs(buf[slot, 1])
    x_new = x_ref[...] + gt_ref[...] * y
    if emit_x:
        xo_ref[...] = x_new
    h_ref[...] = _rms_mod(x_new, g_ref, sc_ref, sh_ref, modulate).astype(h_ref.dtype)


def _moe_combine(x, ys, dest, weights, gt, g, sc, sh, *, emit_x, modulate, h_dtype):
    t, d = x.shape
    tc = min(256, t)
    n_tiles = t // tc
    ids = dest.reshape(n_tiles, 1, MOE_TOP_K * tc)
    row = pl.BlockSpec((1, d), lambda i: (0, 0))
    tile = pl.BlockSpec((tc, d), lambda i: (i, 0))
    out_specs = [tile]
    out_shape = [jax.ShapeDtypeStruct((t, d), h_dtype)]
    if emit_x:
        out_specs = [tile, tile]
        out_shape = [jax.ShapeDtypeStruct((t, d), F32)] + out_shape
    outs = pl.pallas_call(
        functools.partial(_combine_kernel, emit_x=emit_x, modulate=modulate),
        grid=(n_tiles,),
        in_specs=[pl.BlockSpec((1, 1, MOE_TOP_K * tc), lambda i: (i, 0, 0), memory_space=pltpu.SMEM),
                  pl.BlockSpec((1, 1, MOE_TOP_K * tc), lambda i: (jnp.minimum(i + 1, n_tiles - 1), 0, 0),
                               memory_space=pltpu.SMEM),
                  pl.BlockSpec((tc, MOE_TOP_K), lambda i: (i, 0)),
                  tile, row, row, row, row,
                  pl.BlockSpec(memory_space=pl.ANY)],
        out_specs=out_specs,
        out_shape=out_shape,
        scratch_shapes=[pltpu.VMEM((2, MOE_TOP_K, tc, d // 2), jnp.int32), pltpu.SemaphoreType.DMA((2,))],
        compiler_params=_params(("arbitrary",)),
        name="moe_combine",
    )(ids, ids, weights, x, gt.reshape(1, d), g.reshape(1, d), sc.reshape(1, d), sh.reshape(1, d), ys)
    if emit_x:
        return outs[0], outs[1]
    return None, outs[0]


def _route_rows(logits, tril_ref, run_ref):
    lane = lax.broadcasted_iota(jnp.int32, logits.shape, 1)
    neg = -jnp.inf
    far = jnp.int32(LANES)

    def first_max(vals):
        top = jnp.max(vals, axis=-1, keepdims=True)
        return top, jnp.min(jnp.where(vals == top, lane, far), axis=-1, keepdims=True)

    is_group = lane < MOE_GROUPS
    g_top, g_idx = first_max(jnp.where(is_group, logits, neg))
    p_group = 1.0 / jnp.sum(jnp.where(is_group, jnp.exp(logits - g_top), 0.0), axis=-1, keepdims=True)
    first = MOE_GROUPS + MOE_EPG * g_idx
    cand = jnp.where((lane >= first) & (lane < first + MOE_EPG), logits, neg)
    e1, i1 = first_max(cand)
    e2, i2 = first_max(jnp.where(lane == i1, neg, cand))
    z = jnp.exp(e2 - e1)
    w1 = p_group / (1.0 + z)
    w2 = p_group * z / (1.0 + z)
    x1 = i1 - MOE_GROUPS
    x2 = i2 - MOE_GROUPS
    hit1 = lane == x1
    hit2 = lane == x2
    onehot = (hit1 | hit2).astype(F32)
    cum = jnp.dot(tril_ref[...], _bf(onehot), preferred_element_type=F32) + run_ref[0:1, :]
    rank1 = jnp.sum(jnp.where(hit1, cum, 0.0), axis=-1, keepdims=True) - 1.0
    rank2 = jnp.sum(jnp.where(hit2, cum, 0.0), axis=-1, keepdims=True) - 1.0
    last = cum[cum.shape[0] - 1:cum.shape[0], :]
    run_ref[...] = jnp.broadcast_to(last, run_ref.shape)
    cols = (w1, w2, x1.astype(F32), x2.astype(F32), rank1, rank2)
    packed = jnp.zeros(logits.shape, F32)
    for j, cvals in enumerate(cols):
        packed = jnp.where(lane == j, cvals, packed)
    return packed, last


def _dispatch_plan(expert, rank, counts, n_blocks):
    bm = MOE_ROWS
    flat = expert.reshape(-1)
    n_assign = flat.shape[0]
    padded = (counts + bm - 1) // bm * bm
    pad_end = jnp.cumsum(padded)
    dest = (pad_end - padded)[flat] + rank.reshape(-1)
    row_token = jnp.zeros((n_blocks * bm,), jnp.int32).at[dest].set(
        jnp.arange(n_assign, dtype=jnp.int32) // MOE_TOP_K)
    block_start = jnp.arange(n_blocks, dtype=jnp.int32) * bm
    block_expert = jnp.minimum(jnp.sum((pad_end[None, :] <= block_start[:, None]).astype(jnp.int32), axis=1),
                               MOE_EXPERTS - 1)
    n_used = (pad_end[-1] // bm).reshape(1)
    return dest.astype(jnp.int32), row_token, block_expert.astype(jnp.int32), n_used.astype(jnp.int32)


def _router_params(lp, d):
    w_r = jnp.zeros((d, LANES), F32).at[:, :MOE_GROUPS].set(lp['moe_gw'])
    w_r = w_r.at[:, MOE_GROUPS:MOE_GROUPS + MOE_EXPERTS].set(lp['moe_ew'])
    b_r = jnp.zeros((LANES,), F32).at[:MOE_GROUPS].set(lp['moe_gb'])
    b_r = b_r.at[MOE_GROUPS:MOE_GROUPS + MOE_EXPERTS].set(lp['moe_eb'])
    return w_r.astype(BF16), b_r


def _hier_moe(x, h, route, count_rows, gt, layer, ex_gate, ex_up, ex_down, next_norm):
    t, d = x.shape
    weights = route[:, 0:2]
    expert = route[:, 2:4].astype(jnp.int32)
    rank = route[:, 4:6].astype(jnp.int32)
    counts = count_rows[0, :MOE_EXPERTS].astype(jnp.int32)
    n_blocks = (t * MOE_TOP_K) // MOE_ROWS + MOE_EXPERTS
    dest, row_token, block_expert, n_used = _dispatch_plan(expert, rank, counts, n_blocks)
    hmid = _moe_up(h, row_token, ex_gate, ex_up, layer, block_expert, n_used, n_blocks)
    ys = _moe_down(hmid, ex_down, layer, block_expert, n_used, n_blocks)
    return _moe_combine(x, ys, dest, weights, gt, *next_norm[:3], emit_x=next_norm[3], modulate=next_norm[3],
                        h_dtype=BF16 if next_norm[3] else F32)


def kernel(x, c, ada_w, ada_b, norm_mix, norm_ffn, w_in, rw_mu, rw_w0, rw_w2, rw_a0, rw_a2, rw_g2, rw_kk, rw_ka, rw_rk, rw_lnw, rw_lnb, rw_v0, rw_v1, rw_v2, ml_conv_w, ml_conv_b, ml_ib, ml_fb, ml_norm, hg_lb, hg_norm, p_a, p_b, p_c, w_out, moe_gw, moe_gb, moe_ew, moe_eb, ex_gate, ex_up, ex_down, final_norm):
    batch, t, d = x.shape
    assert batch == 1 and d == D_MODEL and t % 512 == 0
    depth = ada_w.shape[0]
    x = x.reshape(t, d)
    lbs = jnp.cumsum(jax.nn.softmax(hg_lb.astype(F32), axis=0), axis=0)
    lbs = lbs - lbs[:1]
    mod = _adaln_mod(c, ada_w, ada_b)
    zeros_d = jnp.zeros((d,), F32)
    c_ml = RW_COLS
    c_mg = c_ml + 2 * ML_QK_WIDTH + 2 * ML_WIDTH
    c_hg = c_mg + 2 * ML_HEADS
    c_gt = c_hg + 4 * HG_WIDTH
    v_first = None
    for l in range(depth):
        sh_m, sc_m, gt_m, sh_f, sc_f, gt_f = jnp.split(mod[l], 6)
        lp = dict(rw_mu=rw_mu[l], rw_w0=rw_w0[l], rw_w2=rw_w2[l], rw_a0=rw_a0[l], rw_a2=rw_a2[l], rw_g2=rw_g2[l],
                  rw_kk=rw_kk[l], rw_ka=rw_ka[l], rw_rk=rw_rk[l], rw_lnw=rw_lnw[l], rw_lnb=rw_lnb[l],
                  ml_conv_w=ml_conv_w[l], ml_conv_b=ml_conv_b[l], ml_ib=ml_ib[l], ml_fb=ml_fb[l],
                  ml_norm=ml_norm[l], hg_norm=hg_norm[l],
                  moe_gw=moe_gw[l], moe_gb=moe_gb[l], moe_ew=moe_ew[l], moe_eb=moe_eb[l])
        if l > 0:
            lp.update(rw_v0=rw_v0[l - 1], rw_v1=rw_v1[l - 1], rw_v2=rw_v2[l - 1])
        if l == 0:
            h = _norm_mod(x, norm_mix[l], sc_m, sh_m)
        wl = w_in[l]
        u_rw = _mm(h, wl[:, :c_ml].astype(BF16), out_dtype=BF16, tn=c_ml)
        u_ml = _mm(h, wl[:, c_ml:c_mg].astype(BF16), out_dtype=BF16, tn=ML_WIDTH)
        w_mg = jnp.zeros((d, LANES), F32).at[:, :2 * ML_HEADS].set(wl[:, c_mg:c_hg])
        u_mg = _mm(h, w_mg.astype(BF16), out_dtype=F32, tn=LANES)
        u_hg = _mm(h, wl[:, c_hg:c_gt].astype(BF16), out_dtype=BF16, tn=ML_WIDTH)
        y_a, v_first = _rwkv_branch(u_rw, v_first, lp)
        y_b = _mlstm_branch(u_ml, u_mg, lp)
        y_c = _hgrn_branch(u_hg, lbs[l], lp)
        x, h, route, count_rows = _merge_out(
            h, y_a, y_b, y_c, wl[:, c_gt:].astype(BF16), p_a[l].astype(BF16), p_b[l].astype(BF16),
            p_c[l].astype(BF16), w_out[l].astype(BF16), x, gt_m, norm_ffn[l], sc_f, sh_f, *_router_params(lp, d))
        if l + 1 < depth:
            nxt = jnp.split(mod[l + 1], 6)
            next_norm = (norm_mix[l + 1], nxt[1], nxt[0], True)
        else:
            next_norm = (final_norm, zeros_d, zeros_d, False)
        x, h = _hier_moe(x, h, route, count_rows, gt_f, l, ex_gate, ex_up, ex_down, next_norm)
    return h.reshape(batch, t, d)
```

```python
import functools

import numpy as np
import jax
import jax.numpy as jnp
from jax import lax
from jax.experimental import pallas as pl
from jax.experimental.pallas import tpu as pltpu

F32 = jnp.float32
BF16 = jnp.bfloat16

D_MODEL = 2048
RW_HEADS, RW_HEAD_DIM, RW_WIDTH = 8, 64, 512
RW_LORA_DECAY, RW_LORA_ICLR, RW_LORA_GATE = 64, 64, 128
RW_LORA = RW_LORA_DECAY + RW_LORA_ICLR + RW_LORA_GATE
RW_COLS = 3 * RW_WIDTH + RW_LORA
RW_GN_EPS = 64e-5
RW_GROUP = 2
RW_CHUNKS_PER_STEP = 4
ML_HEADS, ML_QK_DIM, ML_V_DIM = 4, 128, 256
ML_QK_WIDTH, ML_WIDTH = 512, 1024
ML_CONV = 4
ML_SOFTCAP = 15.0
ML_CHUNKS_PER_STEP = 2
HG_HEADS, HG_DIM, HG_WIDTH = 4, 128, 512
HG_CHUNK = 128
CHUNK = 64
MOE_GROUPS, MOE_EPG, MOE_EXPERTS, MOE_TOP_K = 4, 8, 32, 2
NORM_EPS = 1e-6

LANES = 128
SUBLANES = 8
MOE_ROWS = 256
DMA_PRIORITIES = 2
VMEM_LIMIT = 56 * 1024 * 1024


def _params(sem, limit=VMEM_LIMIT):
    return pltpu.CompilerParams(dimension_semantics=sem, vmem_limit_bytes=limit)


def _bf(x):
    return x.astype(BF16)


def _dot(a, b):
    return jnp.dot(_bf(a), _bf(b), preferred_element_type=F32)


def _dot_nt(a, b):
    return lax.dot_general(_bf(a), _bf(b), (((1,), (1,)), ((), ())), preferred_element_type=F32)


def _dot_tn(a, b):
    return lax.dot_general(_bf(a), _bf(b), (((0,), (0,)), ((), ())), preferred_element_type=F32)


def _bf16_pieces(x, pieces):
    out = []
    rest = x
    for _ in range(pieces):
        part = rest.astype(BF16)
        out.append(part)
        rest = rest - part.astype(F32)
    return out


def _sel_dot(sel, x, pieces):
    acc = None
    for part in _bf16_pieces(x, pieces):
        term = jnp.dot(sel, part, preferred_element_type=F32)
        acc = term if acc is None else acc + term
    return acc


def _seg_sum(x, bd):
    acc = None
    for part in _bf16_pieces(x, 2):
        term = jnp.dot(part, bd, preferred_element_type=F32)
        acc = term if acc is None else acc + term
    return acc


def _sigmoid(x):
    return jax.nn.sigmoid(x)


def _silu(x):
    return x * jax.nn.sigmoid(x)


def _log_sigmoid(x):
    return jnp.minimum(x, 0.0) - jnp.log1p(jnp.exp(-jnp.abs(x)))


def _softplus(x):
    return jnp.maximum(x, 0.0) + jnp.log1p(jnp.exp(-jnp.abs(x)))


def _mod_kernel(c_ref, w_ref, b_ref, o_ref):
    cond = _silu(c_ref[...])
    o_ref[0] = jnp.sum(w_ref[0] * cond, axis=0, keepdims=True) + b_ref[0]


def _adaln_mod(c, ada_w, ada_b):
    depth, d, n = ada_w.shape
    tn = 2048
    out = pl.pallas_call(
        _mod_kernel,
        grid=(depth, n // tn),
        in_specs=[pl.BlockSpec((d, 1), lambda l, j: (0, 0)),
                  pl.BlockSpec((1, d, tn), lambda l, j: (l, 0, j)),
                  pl.BlockSpec((1, 1, tn), lambda l, j: (l, 0, j))],
        out_specs=pl.BlockSpec((1, 1, tn), lambda l, j: (l, 0, j)),
        out_shape=jax.ShapeDtypeStruct((depth, 1, n), F32),
        compiler_params=_params(("arbitrary", "arbitrary")),
        name="adaln_mod",
    )(c.reshape(d, 1), ada_w, ada_b.reshape(depth, 1, n))
    return out[:, 0, :]


HI_HALF = -65536


def _pack_pairs(y):
    n = y.shape[1] // 2
    bits = lax.bitcast_convert_type(y.astype(BF16).astype(F32), jnp.int32)
    return lax.shift_right_logical(bits[:, :n], 16) | (bits[:, n:] & HI_HALF)


def _unpack_pairs(p):
    lo = lax.bitcast_convert_type(lax.shift_left(p, 16), F32)
    hi = lax.bitcast_convert_type(p & HI_HALF, F32)
    return jnp.concatenate([lo, hi], axis=1)


def _rms_mod(x, g_ref, sc_ref, sh_ref, modulate):
    y = x * lax.rsqrt(jnp.mean(x * x, axis=-1, keepdims=True) + NORM_EPS) * g_ref[...]
    if modulate:
        y = y * (1.0 + sc_ref[...]) + sh_ref[...]
    return y


def _norm_kernel(x_ref, g_ref, sc_ref, sh_ref, o_ref):
    o_ref[...] = _rms_mod(x_ref[...], g_ref, sc_ref, sh_ref, True).astype(o_ref.dtype)


def _norm_mod(x, g, sc, sh):
    t, d = x.shape
    tm = min(512, t)
    row = pl.BlockSpec((1, d), lambda i: (0, 0))
    return pl.pallas_call(
        _norm_kernel,
        grid=(t // tm,),
        in_specs=[pl.BlockSpec((tm, d), lambda i: (i, 0)), row, row, row],
        out_specs=pl.BlockSpec((tm, d), lambda i: (i, 0)),
        out_shape=jax.ShapeDtypeStruct((t, d), BF16),
        compiler_params=_params(("arbitrary",)),
        name="rmsnorm_mod",
    )(x, g.reshape(1, d), sc.reshape(1, d), sh.reshape(1, d))


def _mm_kernel(a_ref, w_ref, o_ref):
    o_ref[...] = _dot(a_ref[...], w_ref[...]).astype(o_ref.dtype)


def _row_tile(m):
    return 1024 if m % 1024 == 0 else min(512, m)


def _mm(a, w, *, out_dtype, tn):
    m, k = a.shape
    n = w.shape[1]
    tm = _row_tile(m)
    return pl.pallas_call(
        _mm_kernel,
        grid=(n // tn, m // tm),
        in_specs=[pl.BlockSpec((tm, k), lambda j, i: (i, 0)),
                  pl.BlockSpec((k, tn), lambda j, i: (0, j))],
        out_specs=pl.BlockSpec((tm, tn), lambda j, i: (i, j)),
        out_shape=jax.ShapeDtypeStruct((m, n), out_dtype),
        compiler_params=_params(("arbitrary", "arbitrary")),
        name="dense_matmul",
    )(a, w)


def _rwkv_kernel(*refs, has_vres):
    if has_vres:
        (u_ref, vf_ref, mu_ref, wl_ref, w0_ref, a0_ref, kk_ref, ka_ref, rk_ref, lnw_ref, lnb_ref,
         v0_ref, v1_ref, v2_ref, bd_ref, tril_ref, y_ref, xbuf, st_ref) = refs
    else:
        (u_ref, mu_ref, wl_ref, w0_ref, a0_ref, kk_ref, ka_ref, rk_ref, lnw_ref, lnb_ref,
         bd_ref, tril_ref, y_ref, vf_out_ref, xbuf, st_ref) = refs
    L = CHUNK
    W = RW_WIDTH
    TB = u_ref.shape[0]
    n_chunks = TB // L

    @pl.when(pl.program_id(0) == 0)
    def _():
        xbuf[0:SUBLANES, :] = jnp.zeros((SUBLANES, RW_COLS), F32)
        st_ref[...] = jnp.zeros(st_ref.shape, F32)

    u = u_ref[...].astype(F32)
    xbuf[SUBLANES:SUBLANES + TB, :] = u
    prev = xbuf[SUBLANES - 1:SUBLANES - 1 + TB, :]
    xbuf[0:SUBLANES, :] = u[TB - SUBLANES:TB, :]
    xs = u + mu_ref[...] * (prev - u)
    r = xs[:, 0:W]
    k = xs[:, W:2 * W]
    v = xs[:, 2 * W:3 * W]
    lr = xs[:, 3 * W:3 * W + RW_LORA]
    lane = lax.broadcasted_iota(jnp.int32, lr.shape, 1)
    act = jnp.where(lane < RW_LORA_DECAY, jnp.tanh(lr),
                    jnp.where(lane < RW_LORA_DECAY + RW_LORA_ICLR, lr, _sigmoid(lr)))
    lo = _dot(act, wl_ref[...])
    z = w0_ref[...] + lo[:, 0:W]
    log_w = -_softplus(-z) - 0.5
    ld = -jnp.exp(log_w)
    iclr = _sigmoid(a0_ref[...] + lo[:, W:2 * W])
    gate = lo[:, 2 * W:3 * W]
    if has_vres:
        vv = _dot(_dot(v, v1_ref[...]), v2_ref[...])
        v = v + (vf_ref[...] - v) * _sigmoid(v0_ref[...] + vv)
    else:
        vf_out_ref[...] = v
    bd = bd_ref[...]
    kk = k * kk_ref[...]
    k = k * (1.0 + (iclr - 1.0) * ka_ref[...])
    sums = _seg_sum(jnp.concatenate([kk * kk, r * k * rk_ref[...]], axis=0), bd)
    kk = kk / jnp.maximum(jnp.sqrt(sums[0:TB, :]), 1e-12)
    a = -kk
    b = kk * iclr
    bonus = sums[TB:2 * TB, :] * v

    cum = _sel_dot(tril_ref[...], ld, 2)
    cl_rows = [cum[(c + 1) * L - 1:(c + 1) * L, :] for c in range(n_chunks)]
    cl = jnp.concatenate([jnp.broadcast_to(x, (L, W)) for x in cl_rows], axis=0)
    e_neg = jnp.exp(-cum)
    e_last = jnp.exp(cl - cum)
    a_t = a * jnp.exp(cum - ld)
    r_t = r * jnp.exp(cum)
    b_t = b * e_neg
    k_t = k * e_neg
    b_h = b * e_last
    k_h = k * e_last

    G = RW_GROUP
    S = G * L
    row = lax.broadcasted_iota(jnp.int32, (S, S), 0)
    col = lax.broadcasted_iota(jnp.int32, (S, S), 1)
    same = (row // L) == (col // L)
    strict = same & (row > col)
    incl = same & (row >= col)
    eye = (row == col).astype(F32)
    lane_head = lax.broadcasted_iota(jnp.int32, (1, S), 1) // RW_HEAD_DIM
    head_masks = [(lane_head == j).astype(F32) for j in range(G)]

    n_groups = RW_HEADS // G
    units = [(c, g) for c in range(n_chunks) for g in range(n_groups)]

    def unit_rows(x, c, g):
        return x[c * L:(c + 1) * L, g * S:(g + 1) * S]

    def stack(x, c, g):
        xs_ = unit_rows(x, c, g)
        return jnp.concatenate([xs_ * m for m in head_masks], axis=0)

    def tile(x, c, g):
        return jnp.concatenate([unit_rows(x, c, g)] * G, axis=0)

    a_s = {u: stack(a_t, *u) for u in units}
    r_s = {u: stack(r_t, *u) for u in units}
    v_s = {u: stack(v, *u) for u in units}
    bk_h = {u: jnp.concatenate([stack(b_h, *u), stack(k_h, *u)], axis=0) for u in units}
    xm = {u: _dot_nt(jnp.concatenate([a_s[u], r_s[u]], axis=0),
                     jnp.concatenate([tile(b_t, *u), tile(k_t, *u)], axis=0)) for u in units}
    n_ab = {u: jnp.where(strict, xm[u][0:S, 0:S], 0.0) for u in units}
    n_ak = {u: jnp.where(strict, xm[u][0:S, S:2 * S], 0.0) for u in units}
    r_bk = {u: jnp.concatenate([jnp.where(incl, xm[u][S:2 * S, 0:S], 0.0),
                                jnp.where(incl, xm[u][S:2 * S, S:2 * S], 0.0)], axis=1) for u in units}
    tinv = {u: eye + n_ab[u] for u in units}
    pw = dict(n_ab)
    for _ in range(5):
        pw = {u: _dot(pw[u], pw[u]) for u in units}
        tinv = {u: tinv[u] + _dot(tinv[u], pw[u]) for u in units}
    nv = {u: _dot(n_ak[u], v_s[u]) for u in units}
    pq = {u: _dot(tinv[u], jnp.concatenate([a_s[u], nv[u]], axis=1)) for u in units}
    p_s = {u: pq[u][:, 0:S] for u in units}
    qv = {u: jnp.concatenate([pq[u][:, S:2 * S], v_s[u]], axis=0) for u in units}
    y1 = {u: r_s[u] + _dot(r_bk[u][:, 0:S], p_s[u]) for u in units}
    y0 = {u: _dot(r_bk[u], qv[u]) for u in units}
    gt = {u: _dot_tn(p_s[u], bk_h[u][0:S, :]) for u in units}
    ht = {u: _dot_tn(qv[u], bk_h[u]) for u in units}

    state = [st_ref[g] for g in range(n_groups)]
    y_rows = []
    for c in range(n_chunks):
        w_last = jnp.exp(cl_rows[c])
        ys = []
        for g in range(n_groups):
            u = (c, g)
            st = state[g]
            y_s = _dot_nt(y1[u], st) + y0[u]
            state[g] = st * w_last[:, g * S:(g + 1) * S] + _dot(st, gt[u]) + ht[u]
            y_g = y_s[0:L, :]
            for j in range(1, G):
                y_g = y_g + y_s[j * L:(j + 1) * L, :]
            ys.append(y_g)
        y_rows.append(jnp.concatenate(ys, axis=1))
    for g in range(n_groups):
        st_ref[g] = state[g]
    y = jnp.concatenate(y_rows, axis=0)

    inv_n = 1.0 / RW_HEAD_DIM
    mean = _seg_sum(y, bd) * inv_n
    dlt = y - mean
    var = _seg_sum(dlt * dlt, bd) * inv_n
    yn = dlt * lax.rsqrt(var + RW_GN_EPS) * lnw_ref[...] + lnb_ref[...]
    y_ref[...] = ((yn + bonus) * gate).astype(y_ref.dtype)


def _blockdiag_ones(width, head):
    idx = np.arange(width) // head
    return jnp.asarray((idx[:, None] == idx[None, :]).astype(np.float32), BF16)


def _tril_ones(n, blocks=1):
    return jnp.asarray(np.kron(np.eye(blocks, dtype=np.float32), np.tril(np.ones((n, n), np.float32))), BF16)


def _rwkv_branch(u_rw, v_first, lp):
    t = u_rw.shape[0]
    L = CHUNK
    has_vres = v_first is not None
    W = RW_WIDTH
    vec = lambda a: a.reshape(1, -1).astype(F32)
    full = lambda shape: pl.BlockSpec(shape, lambda i: (0,) * len(shape))
    tb = RW_CHUNKS_PER_STEP * L
    rows = lambda width: pl.BlockSpec((tb, width), lambda i: (i, 0))
    wl = jnp.zeros((RW_LORA, 3 * W), F32)
    r0, r1 = RW_LORA_DECAY, RW_LORA_DECAY + RW_LORA_ICLR
    wl = wl.at[0:r0, 0:W].set(lp['rw_w2']).at[r0:r1, W:2 * W].set(lp['rw_a2']).at[r1:, 2 * W:].set(lp['rw_g2'])
    ins = [u_rw]
    specs = [rows(RW_COLS)]
    if has_vres:
        ins.append(v_first)
        specs.append(rows(W))
    ins += [vec(lp['rw_mu']), wl.astype(BF16), vec(lp['rw_w0']), vec(lp['rw_a0']), vec(lp['rw_kk']),
            vec(lp['rw_ka']), vec(lp['rw_rk']), vec(lp['rw_lnw']), vec(lp['rw_lnb'])]
    specs += [full((1, RW_COLS)), full((RW_LORA, 3 * W))] + [full((1, W))] * 7
    if has_vres:
        v1 = jnp.zeros((W, LANES), F32).at[:, :lp['rw_v1'].shape[1]].set(lp['rw_v1'])
        v2 = jnp.zeros((LANES, W), F32).at[:lp['rw_v2'].shape[0], :].set(lp['rw_v2'])
        ins += [vec(lp['rw_v0']), v1.astype(BF16), v2.astype(BF16)]
        specs += [full((1, W)), full((W, LANES)), full((LANES, W))]
    ins += [_blockdiag_ones(W, RW_HEAD_DIM), _tril_ones(L, RW_CHUNKS_PER_STEP)]
    specs += [full((W, W)), full((tb, tb))]
    out_shape = [jax.ShapeDtypeStruct((t, W), BF16)]
    out_specs = [rows(W)]
    if not has_vres:
        out_shape.append(jax.ShapeDtypeStruct((t, W), F32))
        out_specs.append(rows(W))
    outs = pl.pallas_call(
        functools.partial(_rwkv_kernel, has_vres=has_vres),
        grid=(t // tb,),
        in_specs=specs,
        out_specs=out_specs,
        out_shape=out_shape,
        scratch_shapes=[pltpu.VMEM((tb + SUBLANES, RW_COLS), F32),
                        pltpu.VMEM((RW_HEADS // RW_GROUP, RW_GROUP * RW_HEAD_DIM, RW_GROUP * RW_HEAD_DIM), F32)],
        compiler_params=_params(("arbitrary",)),
        name="rwkv7_branch",
    )(*ins)
    if has_vres:
        return outs[0], v_first
    return outs[0], outs[1]


def _mlstm_kernel(qk_ref, v_ref, o_ref, g_ref, cw_ref, cb_ref, gb_ref, ng_ref, tril_ref,
                  y_ref, xbuf, c_ref, m_ref):
    L = CHUNK
    TB = qk_ref.shape[0]
    n_chunks = TB // L

    @pl.when(pl.program_id(0) == 0)
    def _():
        xbuf[0:SUBLANES, :] = jnp.zeros((SUBLANES, 2 * ML_QK_WIDTH), F32)
        c_ref[...] = jnp.zeros(c_ref.shape, F32)
        m_ref[...] = jnp.zeros(m_ref.shape, F32)

    x0 = qk_ref[...].astype(F32)
    xbuf[SUBLANES:SUBLANES + TB, :] = x0
    conv = cb_ref[...] + cw_ref[ML_CONV - 1:ML_CONV, :] * x0
    for dly in range(1, ML_CONV):
        conv = conv + cw_ref[ML_CONV - 1 - dly:ML_CONV - dly, :] * xbuf[SUBLANES - dly:SUBLANES - dly + TB, :]
    xbuf[0:SUBLANES, :] = x0[TB - SUBLANES:TB, :]
    qk = _silu(conv)
    q = qk[:, :ML_QK_WIDTH]
    k = qk[:, ML_QK_WIDTH:] * (ML_QK_DIM ** -0.5)

    pre = g_ref[...] + gb_ref[...]
    cap = ML_SOFTCAP * jnp.tanh(pre / ML_SOFTCAP)
    lane = lax.broadcasted_iota(jnp.int32, pre.shape, 1)
    gates = jnp.where(lane < ML_HEADS, cap, _log_sigmoid(cap))
    csum = _sel_dot(tril_ref[...], gates, 3)
    gates_t = gates.T
    csum_t = csum.T

    row = lax.broadcasted_iota(jnp.int32, (L, L), 0)
    col = lax.broadcasted_iota(jnp.int32, (L, L), 1)
    causal = row >= col
    ones_col = (lax.broadcasted_iota(jnp.int32, (L, LANES), 1) == 0).astype(F32)
    vv = v_ref[...].astype(F32)
    og = o_ref[...].astype(F32)
    units = [(c, h) for c in range(n_chunks) for h in range(ML_HEADS)]

    def rows(c):
        return slice(c * L, (c + 1) * L)

    qh = {(c, h): q[rows(c), h * ML_QK_DIM:(h + 1) * ML_QK_DIM] for c, h in units}
    kh = {(c, h): k[rows(c), h * ML_QK_DIM:(h + 1) * ML_QK_DIM] for c, h in units}
    vh = {(c, h): jnp.concatenate([vv[rows(c), h * ML_V_DIM:(h + 1) * ML_V_DIM], ones_col], axis=1)
          for c, h in units}
    b_col = {(c, h): csum[rows(c), ML_HEADS + h:ML_HEADS + h + 1] for c, h in units}
    li_col = {(c, h): gates[rows(c), h:h + 1] for c, h in units}
    dmat = {(c, h): jnp.where(causal, b_col[c, h] - csum_t[ML_HEADS + h:ML_HEADS + h + 1, rows(c)]
                              + gates_t[h:h + 1, rows(c)], -jnp.inf) for c, h in units}
    dmax = {u: jnp.max(dmat[u], axis=-1, keepdims=True) for u in units}
    s_loc = {u: _dot_nt(qh[u], kh[u]) * jnp.exp(dmat[u] - dmax[u]) for u in units}
    sv = {u: _dot(s_loc[u], vh[u]) for u in units}
    b_last = {u: b_col[u][L - 1:L, :] for u in units}
    g = {u: b_last[u] - b_col[u] + li_col[u] for u in units}
    gmax = {u: jnp.max(g[u], axis=0, keepdims=True) for u in units}
    kv = {u: _dot_tn(jnp.exp(g[u] - gmax[u]) * kh[u], vh[u]) for u in units}

    c_state = [c_ref[h] for h in range(ML_HEADS)]
    m_state = [m_ref[h][0:1, 0:1] for h in range(ML_HEADS)]
    out_rows = []
    for c in range(n_chunks):
        outs = []
        for h in range(ML_HEADS):
            u = (c, h)
            m_prev = m_state[h]
            m_inter = b_col[u] + m_prev
            m_t = jnp.maximum(m_inter, dmax[u])
            num_aug = jnp.exp(dmax[u] - m_t) * sv[u] + jnp.exp(m_inter - m_t) * _dot(qh[u], c_state[h])
            num = num_aug[:, :ML_V_DIM]
            den = num_aug[:, ML_V_DIM:ML_V_DIM + 1]
            hh = num / jnp.maximum(jnp.abs(den), jnp.exp(-m_t))
            m_new = jnp.maximum(b_last[u] + m_prev, gmax[u])
            c_state[h] = jnp.exp(b_last[u] + m_prev - m_new) * c_state[h] + jnp.exp(gmax[u] - m_new) * kv[u]
            m_state[h] = m_new
            ng = ng_ref[:, h * ML_V_DIM:(h + 1) * ML_V_DIM]
            hn = hh * lax.rsqrt(jnp.mean(hh * hh, axis=-1, keepdims=True) + NORM_EPS) * ng
            outs.append(hn * _sigmoid(og[rows(c), h * ML_V_DIM:(h + 1) * ML_V_DIM]))
        out_rows.append(jnp.concatenate(outs, axis=1))
    for h in range(ML_HEADS):
        c_ref[h] = c_state[h]
        m_ref[h] = jnp.broadcast_to(m_state[h], (SUBLANES, LANES))
    y_ref[...] = jnp.concatenate(out_rows, axis=0).astype(y_ref.dtype)


def _mlstm_branch(u_ml, u_mg, lp):
    t = u_ml.shape[0]
    L = CHUNK
    full = lambda shape: pl.BlockSpec(shape, lambda i: (0,) * len(shape))
    gb = jnp.zeros((1, LANES), F32).at[0, 0:ML_HEADS].set(lp['ml_ib']).at[0, ML_HEADS:2 * ML_HEADS].set(lp['ml_fb'])
    tb = ML_CHUNKS_PER_STEP * L
    return pl.pallas_call(
        _mlstm_kernel,
        grid=(t // tb,),
        in_specs=[pl.BlockSpec((tb, ML_WIDTH), lambda i: (i, 0)),
                  pl.BlockSpec((tb, ML_WIDTH), lambda i: (i, 1)),
                  pl.BlockSpec((tb, ML_WIDTH), lambda i: (i, 2)),
                  pl.BlockSpec((tb, LANES), lambda i: (i, 0)),
                  full((ML_CONV, 2 * ML_QK_WIDTH)), full((1, 2 * ML_QK_WIDTH)), full((1, LANES)),
                  full((1, ML_WIDTH)), full((tb, tb))],
        out_specs=pl.BlockSpec((tb, ML_WIDTH), lambda i: (i, 0)),
        out_shape=jax.ShapeDtypeStruct((t, ML_WIDTH), BF16),
        scratch_shapes=[pltpu.VMEM((tb + SUBLANES, 2 * ML_QK_WIDTH), F32),
                        pltpu.VMEM((ML_HEADS, ML_QK_DIM, ML_V_DIM + LANES), F32),
                        pltpu.VMEM((ML_HEADS, SUBLANES, LANES), F32)],
        compiler_params=_params(("arbitrary",)),
        name="mlstm_branch",
    )(u_ml, u_ml, u_ml, u_mg, lp['ml_conv_w'], lp['ml_conv_b'].reshape(1, -1), gb,
      lp['ml_norm'].reshape(1, -1), _tril_ones(L, ML_CHUNKS_PER_STEP))


def _hgrn_levels():
    L = HG_CHUNK
    t = np.arange(L)
    sel = [np.tril(np.ones((L, L), np.float32))]
    masks = []
    size = L
    while size >= 2:
        half = size // 2
        ref_row = (t // size) * size + half - 1
        sel.append((t[None, :] <= ref_row[:, None]).astype(np.float32))
        same = (t[:, None] // size) == (t[None, :] // size)
        masks.append((same & ((t[:, None] % size) >= half) & ((t[None, :] % size) < half)).astype(np.float32))
        size = half
    return np.concatenate(sel, axis=0), np.stack(masks)


def _hgrn_kernel(u_ref, la_ref, lc_ref, lb_ref, ng_ref, sel_ref, mask_ref, y_ref, s_ref, *, levels):
    L = HG_CHUNK
    W = HG_WIDTH

    @pl.when(pl.program_id(0) == 0)
    def _():
        s_ref[...] = jnp.zeros(s_ref.shape, F32)

    u = u_ref[...].astype(F32)
    q = _silu(u[:, 0:W])
    f_pre = u[:, W:2 * W]
    ii = u[:, 2 * W:3 * W]
    g_pre = u[:, 3 * W:4 * W]
    la = la_ref[...]
    lc = lc_ref[...] + _log_sigmoid(f_pre)
    log_f = jnp.maximum(la, lc) + jnp.log1p(jnp.exp(-jnp.abs(la - lc)))
    k = (1.0 - lb_ref[...]) * _sigmoid(-f_pre)

    cr = _sel_dot(sel_ref[...], log_f, 2)
    cum = cr[0:L, :]
    cl = cum[L - 1:L, :]
    q_in = q * jnp.exp(cum)
    k_out = k * jnp.exp(cl - cum)
    w_last = jnp.exp(cl)
    heads = range(HG_HEADS)
    sl = [slice(h * HG_DIM, (h + 1) * HG_DIM) for h in heads]
    att = [jnp.zeros((L, L), F32) for _ in heads]
    for lv in range(levels):
        ref = cr[(lv + 1) * L:(lv + 2) * L, :]
        qe = q * jnp.exp(jnp.minimum(cum - ref, 0.0))
        ke = k * jnp.exp(jnp.minimum(ref - cum, 0.0))
        mask = mask_ref[lv]
        att = [att[h] + mask * _dot_nt(qe[:, sl[h]], ke[:, sl[h]]) for h in heads]
    diag = [jnp.sum(q[:, sl[h]] * k[:, sl[h]], axis=-1, keepdims=True) for h in heads]
    intra = [_dot(att[h], ii[:, sl[h]]) + diag[h] * ii[:, sl[h]] for h in heads]
    kv = [_dot_tn(ii[:, sl[h]], k_out[:, sl[h]]) for h in heads]
    outs = []
    for h in heads:
        st = s_ref[h]
        o = intra[h] + _dot_nt(q_in[:, sl[h]], st)
        s_ref[h] = st * w_last[:, sl[h]] + kv[h]
        on = o * lax.rsqrt(jnp.mean(o * o, axis=-1, keepdims=True) + NORM_EPS) * ng_ref[:, sl[h]]
        outs.append(on * _silu(g_pre[:, sl[h]]))
    y_ref[...] = jnp.concatenate(outs, axis=1).astype(y_ref.dtype)


def _hgrn_branch(u_hg, lb, lp):
    t = u_hg.shape[0]
    L = HG_CHUNK
    sel, masks = _hgrn_levels()
    levels = masks.shape[0]
    full = lambda shape: pl.BlockSpec(shape, lambda i: (0,) * len(shape))
    lb = lb.reshape(1, -1).astype(F32)
    return pl.pallas_call(
        functools.partial(_hgrn_kernel, levels=levels),
        grid=(t // L,),
        in_specs=[pl.BlockSpec((L, 4 * HG_WIDTH), lambda i: (i, 0)),
                  full((1, HG_WIDTH)), full((1, HG_WIDTH)), full((1, HG_WIDTH)), full((1, HG_WIDTH)),
                  full(sel.shape), full(masks.shape)],
        out_specs=pl.BlockSpec((L, HG_WIDTH), lambda i: (i, 0)),
        out_shape=jax.ShapeDtypeStruct((t, HG_WIDTH), BF16),
        scratch_shapes=[pltpu.VMEM((HG_HEADS, HG_DIM, HG_DIM), F32)],
        compiler_params=_params(("arbitrary",)),
        name="hgrn2_branch",
    )(u_hg, jnp.log(lb), jnp.log1p(-lb), lb, lp['hg_norm'].reshape(1, -1),
      jnp.asarray(sel, BF16), jnp.asarray(masks, F32))


def _gate_merge_kernel(h_ref, ya_ref, yb_ref, yc_ref, wga_ref, wgb_ref, wgc_ref, pa_ref, pb_ref, pc_ref, o_ref):
    h = h_ref[...]
    merged = (_sigmoid(_dot(h, wga_ref[...])) * _dot(ya_ref[...], pa_ref[...])
              + _sigmoid(_dot(h, wgb_ref[...])) * _dot(yb_ref[...], pb_ref[...])
              + _sigmoid(_dot(h, wgc_ref[...])) * _dot(yc_ref[...], pc_ref[...]))
    o_ref[...] = merged.astype(o_ref.dtype)


def _residual_proj_kernel(a_ref, w_ref, x_ref, gt_ref, g_ref, sc_ref, sh_ref, wr_ref, br_ref, tril_ref,
                          xo_ref, h_ref, route_ref, count_ref, run_ref):
    @pl.when(pl.program_id(0) == 0)
    def _():
        run_ref[...] = jnp.zeros(run_ref.shape, F32)

    x_new = x_ref[...] + gt_ref[...] * _dot(a_ref[...], w_ref[...])
    xo_ref[...] = x_new
    h = _rms_mod(x_new, g_ref, sc_ref, sh_ref, True)
    h_ref[...] = _pack_pairs(h)
    route, counts = _route_rows(_dot(h, wr_ref[...]) + br_ref[...], tril_ref, run_ref)
    route_ref[...] = route
    count_ref[...] = jnp.broadcast_to(counts, count_ref.shape)


def _merge_out(h, ya, yb, yc, w_gates, pa, pb, pc, wo, x, gt, g_ffn, sc_ffn, sh_ffn, w_r, b_r):
    t, d = x.shape
    tm = _row_tile(t)
    tn = 512
    nb = d // tn
    merged = pl.pallas_call(
        _gate_merge_kernel,
        grid=(nb, t // tm),
        in_specs=[pl.BlockSpec((tm, d), lambda n, i: (i, 0)),
                  pl.BlockSpec((tm, RW_WIDTH), lambda n, i: (i, 0)),
                  pl.BlockSpec((tm, ML_WIDTH), lambda n, i: (i, 0)),
                  pl.BlockSpec((tm, HG_WIDTH), lambda n, i: (i, 0)),
                  pl.BlockSpec((d, tn), lambda n, i: (0, n)),
                  pl.BlockSpec((d, tn), lambda n, i: (0, nb + n)),
                  pl.BlockSpec((d, tn), lambda n, i: (0, 2 * nb + n)),
                  pl.BlockSpec((RW_WIDTH, tn), lambda n, i: (0, n)),
                  pl.BlockSpec((ML_WIDTH, tn), lambda n, i: (0, n)),
                  pl.BlockSpec((HG_WIDTH, tn), lambda n, i: (0, n))],
        out_specs=pl.BlockSpec((tm, tn), lambda n, i: (i, n)),
        out_shape=jax.ShapeDtypeStruct((t, d), BF16),
        compiler_params=_params(("arbitrary", "arbitrary")),
        name="gate_merge",
    )(h, ya, yb, yc, w_gates, w_gates, w_gates, pa, pb, pc)
    tm2 = min(512, t)
    row = pl.BlockSpec((1, d), lambda i: (0, 0))
    return pl.pallas_call(
        _residual_proj_kernel,
        grid=(t // tm2,),
        in_specs=[pl.BlockSpec((tm2, d), lambda i: (i, 0)),
                  pl.BlockSpec((d, d), lambda i: (0, 0)),
                  pl.BlockSpec((tm2, d), lambda i: (i, 0)),
                  row, row, row, row,
                  pl.BlockSpec((d, LANES), lambda i: (0, 0)),
                  pl.BlockSpec((1, LANES), lambda i: (0, 0)),
                  pl.BlockSpec((tm2, tm2), lambda i: (0, 0))],
        out_specs=[pl.BlockSpec((tm2, d), lambda i: (i, 0)),
                   pl.BlockSpec((tm2, d // 2), lambda i: (i, 0)),
                   pl.BlockSpec((tm2, LANES), lambda i: (i, 0)),
                   pl.BlockSpec((SUBLANES, LANES), lambda i: (0, 0))],
        out_shape=[jax.ShapeDtypeStruct((t, d), F32), jax.ShapeDtypeStruct((t, d // 2), jnp.int32),
                   jax.ShapeDtypeStruct((t, LANES), F32), jax.ShapeDtypeStruct((SUBLANES, LANES), F32)],
        scratch_shapes=[pltpu.VMEM((SUBLANES, LANES), F32)],
        compiler_params=_params(("arbitrary",)),
        name="residual_proj",
    )(merged, wo, x, gt.reshape(1, d), g_ffn.reshape(1, d), sc_ffn.reshape(1, d), sh_ffn.reshape(1, d),
      w_r, b_r.reshape(1, LANES), _tril_ones(tm2))


def _refresh_expert_weights(i, be_ref, nx_ref, layer, weights, wsem):
    e_cur = be_ref[i]
    e_nxt = nx_ref[i]
    prev = be_ref[jnp.maximum(i - 1, 0)]

    def copies(e):
        return [pltpu.make_async_copy(hbm.at[layer, e], stage, wsem.at[j])
                for j, (hbm, stage, _) in enumerate(weights)]

    @pl.when(i == 0)
    def _():
        for cp in copies(e_cur):
            cp.start()

    @pl.when((i == 0) | (e_cur != prev))
    def _():
        for cp in copies(e_cur):
            cp.wait()
        for _, stage, work in weights:
            work[...] = stage[...].astype(BF16)

        @pl.when(e_nxt != e_cur)
        def _():
            for cp in copies(e_nxt):
                cp.start()


def _next_expert(block_expert):
    n = block_expert.shape[0]
    idx = jnp.sum((block_expert[None, :] <= block_expert[:, None]).astype(jnp.int32), axis=1)
    return block_expert[jnp.minimum(idx, n - 1)]


def _moe_up_kernel(be_ref, nx_ref, nu_ref, tok_ref, nxt_ref, h_ref, wg_ref, wu_ref, o_ref,
                   xbuf, stage_g, stage_u, wg_bf, wu_bf, sem, wsem, *, layer):
    i = pl.program_id(0)
    n_used = nu_ref[0]
    rows = xbuf.shape[1]
    slot = lax.rem(i, 2)

    def row_copy(s, r, src_row):
        return pltpu.make_async_copy(h_ref.at[pl.ds(src_row, 1)], xbuf.at[s, pl.ds(r, 1)], sem.at[s])

    def start_block(s, ids_ref):
        for r in range(rows):
            row_copy(s, r, ids_ref[0, 0, r]).start(priority=r % DMA_PRIORITIES)

    def wait_block(s):
        def body(r, c):
            row_copy(s, r, 0).wait()
            return c

        lax.fori_loop(0, rows, body, 0, unroll=8)

    @pl.when((i == 0) & (n_used > 0))
    def _():
        start_block(0, tok_ref)

    @pl.when(i + 1 < n_used)
    def _():
        start_block(1 - slot, nxt_ref)

    _refresh_expert_weights(i, be_ref, nx_ref, layer, ((wg_ref, stage_g, wg_bf), (wu_ref, stage_u, wu_bf)), wsem)

    @pl.when(i < n_used)
    def _():
        wait_block(slot)
        x = _bf(_unpack_pairs(xbuf[slot]))
        g = jnp.dot(x, wg_bf[...], preferred_element_type=F32)
        u = jnp.dot(x, wu_bf[...], preferred_element_type=F32)
        o_ref[...] = (_silu(g) * u).astype(o_ref.dtype)

    @pl.when(i >= n_used)
    def _():
        o_ref[...] = jnp.zeros(o_ref.shape, o_ref.dtype)


def _moe_up(h, row_token, w_gate, w_up, layer, block_expert, n_used, n_blocks):
    d, ff = w_gate.shape[2], w_gate.shape[3]
    bm = MOE_ROWS
    tok = row_token.reshape(n_blocks, 1, bm)
    last = n_blocks - 1
    any_space = pl.BlockSpec(memory_space=pl.ANY)
    return pl.pallas_call(
        functools.partial(_moe_up_kernel, layer=layer),
        grid_spec=pltpu.PrefetchScalarGridSpec(
            num_scalar_prefetch=3,
            grid=(n_blocks,),
            in_specs=[pl.BlockSpec((1, 1, bm), lambda i, be, nx, nu: (i, 0, 0), memory_space=pltpu.SMEM),
                      pl.BlockSpec((1, 1, bm), lambda i, be, nx, nu: (jnp.minimum(i + 1, last), 0, 0),
                                   memory_space=pltpu.SMEM),
                      any_space, any_space, any_space],
            out_specs=pl.BlockSpec((bm, ff), lambda i, be, nx, nu: (i, 0)),
            scratch_shapes=[pltpu.VMEM((2, bm, d // 2), jnp.int32),
                            pltpu.VMEM((d, ff), F32), pltpu.VMEM((d, ff), F32),
                            pltpu.VMEM((d, ff), BF16), pltpu.VMEM((d, ff), BF16),
                            pltpu.SemaphoreType.DMA((2,)), pltpu.SemaphoreType.DMA((2,))]),
        out_shape=jax.ShapeDtypeStruct((n_blocks * bm, ff), BF16),
        compiler_params=_params(("arbitrary",)),
        name="moe_up",
    )(block_expert, _next_expert(block_expert), n_used, tok, tok, h, w_gate, w_up)


def _moe_down_kernel(be_ref, nx_ref, nu_ref, x_ref, wd_ref, o_ref, stage_d, wd_bf, wsem, *, layer):
    i = pl.program_id(0)
    _refresh_expert_weights(i, be_ref, nx_ref, layer, ((wd_ref, stage_d, wd_bf),), wsem)

    @pl.when(i < nu_ref[0])
    def _():
        o_ref[...] = _pack_pairs(jnp.dot(x_ref[...], wd_bf[...], preferred_element_type=F32))

    @pl.when(i >= nu_ref[0])
    def _():
        o_ref[...] = jnp.zeros(o_ref.shape, o_ref.dtype)


def _moe_down(hmid, w_down, layer, block_expert, n_used, n_blocks):
    ff = hmid.shape[1]
    d = w_down.shape[3]
    bm = MOE_ROWS
    return pl.pallas_call(
        functools.partial(_moe_down_kernel, layer=layer),
        grid_spec=pltpu.PrefetchScalarGridSpec(
            num_scalar_prefetch=3,
            grid=(n_blocks,),
            in_specs=[pl.BlockSpec((bm, ff), lambda i, be, nx, nu: (i, 0)),
                      pl.BlockSpec(memory_space=pl.ANY)],
            out_specs=pl.BlockSpec((bm, d // 2), lambda i, be, nx, nu: (i, 0)),
            scratch_shapes=[pltpu.VMEM((ff, d), F32), pltpu.VMEM((ff, d), BF16),
                            pltpu.SemaphoreType.DMA((1,))]),
        out_shape=jax.ShapeDtypeStruct((n_blocks * bm, d // 2), jnp.int32),
        compiler_params=_params(("arbitrary",)),
        name="moe_down",
    )(block_expert, _next_expert(block_expert), n_used, hmid, w_down)


def _combine_kernel(dest_ref, nxt_ref, w_ref, x_ref, gt_ref, g_ref, sc_ref, sh_ref, ys_ref, *rest, emit_x, modulate):
    if emit_x:
        xo_ref, h_ref, buf, sem = rest
    else:
        h_ref, buf, sem = rest
    tc = x_ref.shape[0]
    i = pl.program_id(0)
    slot = lax.rem(i, 2)

    def row_copy(s, tok, choice, src_row):
        return pltpu.make_async_copy(ys_ref.at[pl.ds(src_row, 1)], buf.at[s, choice, pl.ds(tok, 1)], sem.at[s])

    def start_tile(s, ids_ref):
        for tok in range(tc):
            for choice in range(MOE_TOP_K):
                row_copy(s, tok, choice, ids_ref[0, 0, MOE_TOP_K * tok + choice]).start(priority=choice)

    def wait_tile(s):
        def body(j, c):
            row_copy(s, 0, 0, 0).wait()
            return c

        lax.fori_loop(0, MOE_TOP_K * tc, body, 0, unroll=8)

    @pl.when(i == 0)
    def _():
        start_tile(0, dest_ref)

    @pl.when(i + 1 < pl.num_programs(0))
    def _():
        start_tile(1 - slot, nxt_ref)

    wait_tile(slot)
    w = w_ref[...]
    y = w[:, 0:1] * _unpack_pairs(buf[slot, 0]) + w[:, 1:2] * _unpack_pairs(buf[slot, 1])
    x_new = x_ref[...] + gt_ref[...] * y
    if emit_x:
        xo_ref[...] = x_new
    h_ref[...] = _rms_mod(x_new, g_ref, sc_ref, sh_ref, modulate).astype(h_ref.dtype)


def _moe_combine(x, ys, dest, weights, gt, g, sc, sh, *, emit_x, modulate, h_dtype):
    t, d = x.shape
    tc = min(256, t)
    n_tiles = t // tc
    ids = dest.reshape(n_tiles, 1, MOE_TOP_K * tc)
    row = pl.BlockSpec((1, d), lambda i: (0, 0))
    tile = pl.BlockSpec((tc, d), lambda i: (i, 0))
    out_specs = [tile]
    out_shape = [jax.ShapeDtypeStruct((t, d), h_dtype)]
    if emit_x:
        out_specs = [tile, tile]
        out_shape = [jax.ShapeDtypeStruct((t, d), F32)] + out_shape
    outs = pl.pallas_call(
        functools.partial(_combine_kernel, emit_x=emit_x, modulate=modulate),
        grid=(n_tiles,),
        in_specs=[pl.BlockSpec((1, 1, MOE_TOP_K * tc), lambda i: (i, 0, 0), memory_space=pltpu.SMEM),
                  pl.BlockSpec((1, 1, MOE_TOP_K * tc), lambda i: (jnp.minimum(i + 1, n_tiles - 1), 0, 0),
                               memory_space=pltpu.SMEM),
                  pl.BlockSpec((tc, MOE_TOP_K), lambda i: (i, 0)),
                  tile, row, row, row, row,
                  pl.BlockSpec(memory_space=pl.ANY)],
        out_specs=out_specs,
        out_shape=out_shape,
        scratch_shapes=[pltpu.VMEM((2, MOE_TOP_K, tc, d // 2), jnp.int32), pltpu.SemaphoreType.DMA((2,))],
        compiler_params=_params(("arbitrary",)),
        name="moe_combine",
    )(ids, ids, weights, x, gt.reshape(1, d), g.reshape(1, d), sc.reshape(1, d), sh.reshape(1, d), ys)
    if emit_x:
        return outs[0], outs[1]
    return None, outs[0]


def _route_rows(logits, tril_ref, run_ref):
    lane = lax.broadcasted_iota(jnp.int32, logits.shape, 1)
    neg = -jnp.inf
    far = jnp.int32(LANES)

    def first_max(vals):
        top = jnp.max(vals, axis=-1, keepdims=True)
        return top, jnp.min(jnp.where(vals == top, lane, far), axis=-1, keepdims=True)

    is_group = lane < MOE_GROUPS
    g_top, g_idx = first_max(jnp.where(is_group, logits, neg))
    p_group = 1.0 / jnp.sum(jnp.where(is_group, jnp.exp(logits - g_top), 0.0), axis=-1, keepdims=True)
    first = MOE_GROUPS + MOE_EPG * g_idx
    cand = jnp.where((lane >= first) & (lane < first + MOE_EPG), logits, neg)
    e1, i1 = first_max(cand)
    e2, i2 = first_max(jnp.where(lane == i1, neg, cand))
    z = jnp.exp(e2 - e1)
    w1 = p_group / (1.0 + z)
    w2 = p_group * z / (1.0 + z)
    x1 = i1 - MOE_GROUPS
    x2 = i2 - MOE_GROUPS
    hit1 = lane == x1
    hit2 = lane == x2
    onehot = (hit1 | hit2).astype(F32)
    cum = jnp.dot(tril_ref[...], _bf(onehot), preferred_element_type=F32) + run_ref[0:1, :]
    rank1 = jnp.sum(jnp.where(hit1, cum, 0.0), axis=-1, keepdims=True) - 1.0
    rank2 = jnp.sum(jnp.where(hit2, cum, 0.0), axis=-1, keepdims=True) - 1.0
    last = cum[cum.shape[0] - 1:cum.shape[0], :]
    run_ref[...] = jnp.broadcast_to(last, run_ref.shape)
    cols = (w1, w2, x1.astype(F32), x2.astype(F32), rank1, rank2)
    packed = jnp.zeros(logits.shape, F32)
    for j, cvals in enumerate(cols):
        packed = jnp.where(lane == j, cvals, packed)
    return packed, last


def _dispatch_plan(expert, rank, counts, n_blocks):
    bm = MOE_ROWS
    flat = expert.reshape(-1)
    n_assign = flat.shape[0]
    padded = (counts + bm - 1) // bm * bm
    pad_end = jnp.cumsum(padded)
    dest = (pad_end - padded)[flat] + rank.reshape(-1)
    row_token = jnp.zeros((n_blocks * bm,), jnp.int32).at[dest].set(
        jnp.arange(n_assign, dtype=jnp.int32) // MOE_TOP_K)
    block_start = jnp.arange(n_blocks, dtype=jnp.int32) * bm
    block_expert = jnp.minimum(jnp.sum((pad_end[None, :] <= block_start[:, None]).astype(jnp.int32), axis=1),
                               MOE_EXPERTS - 1)
    n_used = (pad_end[-1] // bm).reshape(1)
    return dest.astype(jnp.int32), row_token, block_expert.astype(jnp.int32), n_used.astype(jnp.int32)


def _router_params(lp, d):
    w_r = jnp.zeros((d, LANES), F32).at[:, :MOE_GROUPS].set(lp['moe_gw'])
    w_r = w_r.at[:, MOE_GROUPS:MOE_GROUPS + MOE_EXPERTS].set(lp['moe_ew'])
    b_r = jnp.zeros((LANES,), F32).at[:MOE_GROUPS].set(lp['moe_gb'])
    b_r = b_r.at[MOE_GROUPS:MOE_GROUPS + MOE_EXPERTS].set(lp['moe_eb'])
    return w_r.astype(BF16), b_r


def _hier_moe(x, h, route, count_rows, gt, layer, ex_gate, ex_up, ex_down, next_norm):
    t, d = x.shape
    weights = route[:, 0:2]
    expert = route[:, 2:4].astype(jnp.int32)
    rank = route[:, 4:6].astype(jnp.int32)
    counts = count_rows[0, :MOE_EXPERTS].astype(jnp.int32)
    n_blocks = (t * MOE_TOP_K) // MOE_ROWS + MOE_EXPERTS
    dest, row_token, block_expert, n_used = _dispatch_plan(expert, rank, counts, n_blocks)
    hmid = _moe_up(h, row_token, ex_gate, ex_up, layer, block_expert, n_used, n_blocks)
    ys = _moe_down(hmid, ex_down, layer, block_expert, n_used, n_blocks)
    return _moe_combine(x, ys, dest, weights, gt, *next_norm[:3], emit_x=next_norm[3], modulate=next_norm[3],
                        h_dtype=BF16 if next_norm[3] else F32)


def kernel(x, c, ada_w, ada_b, norm_mix, norm_ffn, w_in, rw_mu, rw_w0, rw_w2, rw_a0, rw_a2, rw_g2, rw_kk, rw_ka, rw_rk, rw_lnw, rw_lnb, rw_v0, rw_v1, rw_v2, ml_conv_w, ml_conv_b, ml_ib, ml_fb, ml_norm, hg_lb, hg_norm, p_a, p_b, p_c, w_out, moe_gw, moe_gb, moe_ew, moe_eb, ex_gate, ex_up, ex_down, final_norm):
    batch, t, d = x.shape
    assert batch == 1 and d == D_MODEL and t % 512 == 0
    depth = ada_w.shape[0]
    x = x.reshape(t, d)
    lbs = jnp.cumsum(jax.nn.softmax(hg_lb.astype(F32), axis=0), axis=0)
    lbs = lbs - lbs[:1]
    mod = _adaln_mod(c, ada_w, ada_b)
    zeros_d = jnp.zeros((d,), F32)
    c_ml = RW_COLS
    c_mg = c_ml + 2 * ML_QK_WIDTH + 2 * ML_WIDTH
    c_hg = c_mg + 2 * ML_HEADS
    c_gt = c_hg + 4 * HG_WIDTH
    v_first = None
    for l in range(depth):
        sh_m, sc_m, gt_m, sh_f, sc_f, gt_f = jnp.split(mod[l], 6)
        lp = dict(rw_mu=rw_mu[l], rw_w0=rw_w0[l], rw_w2=rw_w2[l], rw_a0=rw_a0[l], rw_a2=rw_a2[l], rw_g2=rw_g2[l],
                  rw_kk=rw_kk[l], rw_ka=rw_ka[l], rw_rk=rw_rk[l], rw_lnw=rw_lnw[l], rw_lnb=rw_lnb[l],
                  ml_conv_w=ml_conv_w[l], ml_conv_b=ml_conv_b[l], ml_ib=ml_ib[l], ml_fb=ml_fb[l],
                  ml_norm=ml_norm[l], hg_norm=hg_norm[l],
                  moe_gw=moe_gw[l], moe_gb=moe_gb[l], moe_ew=moe_ew[l], moe_eb=moe_eb[l])
        if l > 0:
            lp.update(rw_v0=rw_v0[l - 1], rw_v1=rw_v1[l - 1], rw_v2=rw_v2[l - 1])
        if l == 0:
            h = _norm_mod(x, norm_mix[l], sc_m, sh_m)
        wl = w_in[l]
        u_rw = _mm(h, wl[:, :c_ml].astype(BF16), out_dtype=BF16, tn=c_ml)
        u_ml = _mm(h, wl[:, c_ml:c_mg].astype(BF16), out_dtype=BF16, tn=ML_WIDTH)
        w_mg = jnp.zeros((d, LANES), F32).at[:, :2 * ML_HEADS].set(wl[:, c_mg:c_hg])
        u_mg = _mm(h, w_mg.astype(BF16), out_dtype=F32, tn=LANES)
        u_hg = _mm(h, wl[:, c_hg:c_gt].astype(BF16), out_dtype=BF16, tn=ML_WIDTH)
        y_a, v_first = _rwkv_branch(u_rw, v_first, lp)
        y_b = _mlstm_branch(u_ml, u_mg, lp)
        y_c = _hgrn_branch(u_hg, lbs[l], lp)
        x, h, route, count_rows = _merge_out(
            h, y_a, y_b, y_c, wl[:, c_gt:].astype(BF16), p_a[l].astype(BF16), p_b[l].astype(BF16),
            p_c[l].astype(BF16), w_out[l].astype(BF16), x, gt_m, norm_ffn[l], sc_f, sh_f, *_router_params(lp, d))
        if l + 1 < depth:
            nxt = jnp.split(mod[l + 1], 6)
            next_norm = (norm_mix[l + 1], nxt[1], nxt[0], True)
        else:
            next_norm = (final_norm, zeros_d, zeros_d, False)
        x, h = _hier_moe(x, h, route, count_rows, gt_f, l, ex_gate, ex_up, ex_down, next_norm)
    return h.reshape(batch, t, d)
```
